```python
import math
import jax, jax.numpy as jnp
from jax import lax
import numpy as np

D_MODEL = 1024
BATCH = 2
SEQ = 8192
DEPTH = 4

N_MIXERS = 2
N_HGRN_LAYERS = (DEPTH + 1) // 2
N_HYENA_LAYERS = DEPTH // 2
HG_EXPAND = 128
HG_HEADS = D_MODEL // HG_EXPAND
HG_DK = HG_EXPAND
HG_DV = D_MODEL // HG_HEADS
HG_CHUNK = 64
HY_SHORT_CONV = 3
HY_FILTER_WIDTH = 64
HY_FILTER_INNER = 2
HY_EMB_DIM = 33
HY_BANDS = (HY_EMB_DIM - 1) // 2
HY_MAX_DECAY = math.log(1e-2) / 0.3
HY_MIN_DECAY = math.log(1e-2) / 1.5
D_FF = 4 * D_MODEL
NORM_EPS = 1e-6

kernel_name = "hybrid_hgrn2_hyena_adaln_encoder"


def rms_norm(x, g):
    xf = x.astype(jnp.float32)
    y = xf * lax.rsqrt(jnp.mean(xf * xf, axis=-1, keepdims=True) + NORM_EPS)
    return (y * g.astype(jnp.float32)).astype(x.dtype)


def adaln_params(c, w, b):
    m = jax.nn.silu(c) @ w + b
    shift, scale, gate = jnp.split(m[:, None, :], 3, axis=-1)
    return shift, scale, gate


def gla_chunk_scan(q, k, v, log_f):
    n, L, H, DK = q.shape
    DV = v.shape[-1]
    C = HG_CHUNK
    nc = L // C

    def to_chunks(a):
        return a.reshape(n, nc, C, H, a.shape[-1]).transpose(1, 0, 3, 2, 4)

    qc, kc, vc, gc = to_chunks(q), to_chunks(k), to_chunks(v), to_chunks(log_f)
    bc = jnp.cumsum(gc, axis=-2)
    lower = jnp.tril(jnp.ones((C, C), dtype=bool))[:, :, None]

    def step(S, inp):
        qt, kt, vt, bt = inp
        rel = bt[:, :, :, None, :] - bt[:, :, None, :, :]
        decay = jnp.where(lower, jnp.exp(jnp.where(lower, rel, 0.0)), 0.0)
        attn = jnp.einsum('nhtd,nhsd,nhtsd->nhts', qt, kt, decay)
        o = (jnp.einsum('nhts,nhsv->nhtv', attn, vt)
             + jnp.einsum('nhtd,nhdv->nhtv', qt * jnp.exp(bt), S))
        b_end = bt[:, :, -1:, :]
        S = (jnp.exp(b_end[:, :, 0, :])[..., None] * S
             + jnp.einsum('nhsd,nhsv->nhdv', kt * jnp.exp(b_end - bt), vt))
        return S, o

    S0 = jnp.zeros((n, H, DK, DV), jnp.float32)
    _, o = lax.scan(step, S0, (qc, kc, vc, bc))
    return o.transpose(1, 0, 3, 2, 4).reshape(n, L, H, DV)


def hgrn2_mixer(h, w_in, lower_bound, norm_g, w_out):
    B, L, _ = h.shape
    proj = (h @ w_in).astype(jnp.float32)
    q, i, z_fwd, z_bwd, gate = jnp.split(proj, 5, axis=-1)
    q = jax.nn.silu(q)
    z = jnp.stack([z_fwd, z_bwd], axis=0)
    lb = lower_bound.astype(jnp.float32)[:, None, None, :]
    log_f = jnp.log(lb + (1.0 - lb) * jax.nn.sigmoid(z))
    k = (1.0 - lb) * jax.nn.sigmoid(-z)
    q2 = jnp.concatenate([q, q[:, ::-1]], axis=0)
    v2 = jnp.concatenate([i, i[:, ::-1]], axis=0)
    k2 = jnp.concatenate([k[0], k[1][:, ::-1]], axis=0)
    g2 = jnp.concatenate([log_f[0], log_f[1][:, ::-1]], axis=0)
    heads = lambda a: a.reshape(2 * B, L, HG_HEADS, -1)
    o = gla_chunk_scan(heads(q2), heads(k2), heads(v2), heads(g2))
    o = o[:B] + o[B:, ::-1]
    o = o * lax.rsqrt(jnp.mean(o * o, axis=-1, keepdims=True) + NORM_EPS)
    o = o * norm_g.astype(jnp.float32) * jax.nn.silu(gate.reshape(B, L, HG_HEADS, HG_DV))
    return (o.reshape(B, L, D_MODEL) @ w_out.astype(jnp.float32)).astype(h.dtype)


def short_conv_centred(u, w, b):
    up = jnp.pad(u, ((0, 0), (1, 1), (0, 0)))
    return up[:, :-2] * w[0] + up[:, 1:-1] * w[1] + up[:, 2:] * w[2] + b


def hyena_filter(L, w_in, w_mid, b, freq, w_out):
    f32 = jnp.float32
    t = jnp.linspace(0.0, 1.0, L, dtype=f32)[:, None]
    bands = jnp.linspace(1e-4, HY_BANDS - 1.0, HY_BANDS, dtype=f32)
    ang = (2.0 * math.pi / L) * jnp.arange(L, dtype=f32)[:, None] * bands
    emb = jnp.concatenate([t, jnp.cos(ang), -jnp.sin(ang)], axis=-1)
    freq = freq.astype(f32)
    b = b.astype(f32)
    hdn = jnp.sin(freq * (emb @ w_in.astype(f32) + b[0]))
    for m in range(HY_FILTER_INNER):
        hdn = jnp.sin(freq * (hdn @ w_mid[m].astype(f32) + b[m + 1]))
    hf = (hdn @ w_out.astype(f32)).reshape(L, 2, D_MODEL)
    deltas = jnp.abs(jnp.linspace(HY_MIN_DECAY, HY_MAX_DECAY, D_MODEL, dtype=f32))
    hf = hf * jnp.exp(-t * deltas)[:, None, :]
    h_fwd, h_bwd = hf[:, 0], hf[:, 1]
    return jnp.concatenate([h_fwd, jnp.zeros((1, D_MODEL), f32), h_bwd[:0:-1]], axis=0)


def hyena_mixer(h, w_in, b_in, conv_w, conv_b, f_w_in, f_w_mid, f_b, f_freq, f_w_out, skip, w_out):
    B, L, _ = h.shape
    u = short_conv_centred(h @ w_in + b_in, conv_w, conv_b)
    x0, x1, v = jnp.split(u, 3, axis=-1)
    z = (x1 * v).astype(jnp.float32)
    filt = hyena_filter(L, f_w_in, f_w_mid, f_b, f_freq, f_w_out)
    zf = jnp.fft.rfft(z, n=2 * L, axis=1)
    hf = jnp.fft.rfft(filt, axis=0)
    y = jnp.fft.irfft(zf * hf[None], n=2 * L, axis=1)[:, :L]
    y = y + skip.astype(jnp.float32) * z
    y = (x0.astype(jnp.float32) * y).astype(h.dtype)
    return y @ w_out


def setup_inputs(seed: int = 0) -> dict:
    key = jax.random.key(seed)
    ks = jax.random.split(key, 24)
    D, NA, NB, W = D_MODEL, N_HGRN_LAYERS, N_HYENA_LAYERS, HY_FILTER_WIDTH
    nrm = lambda k, s: jax.random.normal(k, s, jnp.float32)
    return {
        "x": nrm(ks[0], (BATCH, SEQ, D)),
        "c": nrm(ks[1], (BATCH, D)),
        "ada_w": nrm(ks[2], (DEPTH, 2, D, 3 * D)) * (0.5 * D ** -0.5),
        "ada_b": nrm(ks[3], (DEPTH, 2, 3 * D)) * 0.02,
        "norm_g": 1.0 + 0.02 * nrm(ks[4], (DEPTH, 2, D)),
        "hg_w_in": nrm(ks[5], (NA, D, 5 * D)) * D ** -0.5,
        "hg_lower_bounds": nrm(ks[6], (2, NA, D)) * 0.1,
        "hg_norm_g": 1.0 + 0.02 * nrm(ks[7], (NA, HG_DV)),
        "hg_w_out": nrm(ks[8], (NA, D, D)) * D ** -0.5,
        "hy_w_in": nrm(ks[9], (NB, D, 3 * D)) * D ** -0.5,
        "hy_b_in": nrm(ks[10], (NB, 3 * D)) * 0.02,
        "hy_conv_w": nrm(ks[11], (NB, HY_SHORT_CONV, 3 * D)) * HY_SHORT_CONV ** -0.5,
        "hy_conv_b": nrm(ks[12], (NB, 3 * D)) * 0.02,
        "hy_filt_w_in": nrm(ks[13], (NB, HY_EMB_DIM, W)) * HY_EMB_DIM ** -0.5,
        "hy_filt_w_mid": nrm(ks[14], (NB, HY_FILTER_INNER, W, W)) * W ** -0.5,
        "hy_filt_b": nrm(ks[15], (NB, HY_FILTER_INNER + 1, W)) * 0.1,
        "hy_filt_freq": 1.0 + 0.1 * nrm(ks[16], (NB, W)),
        "hy_filt_w_out": nrm(ks[17], (NB, W, 2 * D)) * (0.05 * W ** -0.5),
        "hy_skip": nrm(ks[18], (NB, D)),
        "hy_w_out": nrm(ks[19], (NB, D, D)) * D ** -0.5,
        "mlp_w1": nrm(ks[20], (DEPTH, D, D_FF)) * D ** -0.5,
        "mlp_w2": nrm(ks[21], (DEPTH, D_FF, D)) * D_FF ** -0.5,
        "final_g": 1.0 + 0.02 * nrm(ks[22], (D,)),
    }


def reference(x, c, ada_w, ada_b, norm_g, hg_w_in, hg_lower_bounds, hg_norm_g, hg_w_out,
              hy_w_in, hy_b_in, hy_conv_w, hy_conv_b, hy_filt_w_in, hy_filt_w_mid, hy_filt_b,
              hy_filt_freq, hy_filt_w_out, hy_skip, hy_w_out, mlp_w1, mlp_w2, final_g):
    lbs = jax.nn.softmax(hg_lower_bounds.astype(jnp.float32), axis=1)
    lbs = jnp.cumsum(lbs, axis=1) - lbs[:, :1]
    for i in range(DEPTH):
        j = i // N_MIXERS
        shift, scale, gate = adaln_params(c, ada_w[i, 0], ada_b[i, 0])
        hn = rms_norm(x, norm_g[i, 0]) * (1.0 + scale) + shift
        if i % N_MIXERS == 0:
            mix = hgrn2_mixer(hn, hg_w_in[j], lbs[:, j], hg_norm_g[j], hg_w_out[j])
        else:
            mix = hyena_mixer(hn, hy_w_in[j], hy_b_in[j], hy_conv_w[j], hy_conv_b[j],
                              hy_filt_w_in[j], hy_filt_w_mid[j], hy_filt_b[j], hy_filt_freq[j],
                              hy_filt_w_out[j], hy_skip[j], hy_w_out[j])
        x = x + gate * mix
        shift, scale, gate = adaln_params(c, ada_w[i, 1], ada_b[i, 1])
        hn = rms_norm(x, norm_g[i, 1]) * (1.0 + scale) + shift
        x = x + gate * (jnp.square(jax.nn.relu(hn @ mlp_w1[i])) @ mlp_w2[i])
    return rms_norm(x, final_g)
```

```python
import functools
import math

import jax
import jax.numpy as jnp
from jax import lax
from jax.experimental import pallas as pl
from jax.experimental.pallas import tpu as pltpu

F32 = jnp.float32
BF16 = jnp.bfloat16

NORM_EPS = 1e-6
HEAD_DIM = 128
GLA_CHUNK = 64
GLA_SUB = 16
SUBLANES = 8
FFT_N2 = 128
HY_FILTER_PAD = 64
HY_MAX_DECAY = math.log(1e-2) / 0.3
HY_MIN_DECAY = math.log(1e-2) / 1.5
VMEM_LIMIT = 56 * 1024 * 1024


def _params(*sem):
    return pltpu.CompilerParams(dimension_semantics=sem, vmem_limit_bytes=VMEM_LIMIT)


def _tile(n, target, unit):
    t = min(n, target) // unit * unit
    while n % t:
        t -= unit
    return t


def _rms_mod(x, g, scale, shift):
    ms = jnp.mean(x * x, axis=-1, keepdims=True)
    y = x * lax.rsqrt(ms + NORM_EPS) * g
    return y * (1.0 + scale) + shift


def _adaln_kernel(s_ref, w_ref, b_ref, o_ref):
    s = s_ref[...]
    s = s * jax.nn.sigmoid(s)
    o_ref[...] = jnp.dot(s, w_ref[...], preferred_element_type=F32,
                         precision=lax.Precision.HIGHEST) + b_ref[...]


def adaln_all(c, ada_w, ada_b):
    depth, two, d, d3 = ada_w.shape
    n = depth * two
    b = c.shape[0]
    rows = -(-b // SUBLANES) * SUBLANES
    cp = jnp.zeros((rows, d), F32).at[:b].set(c)
    tn = _tile(d3, 1536, 128)
    out = pl.pallas_call(
        _adaln_kernel,
        grid=(n, d3 // tn),
        in_specs=[
            pl.BlockSpec((rows, d), lambda i, j: (0, 0)),
            pl.BlockSpec((None, d, tn), lambda i, j: (i, 0, j)),
            pl.BlockSpec((None, 1, tn), lambda i, j: (i, 0, j)),
        ],
        out_specs=pl.BlockSpec((None, rows, tn), lambda i, j: (i, 0, j)),
        out_shape=jax.ShapeDtypeStruct((n, rows, d3), F32),
        compiler_params=_params("parallel", "parallel"),
        name="adaln",
    )(cp, ada_w.reshape(n, d, d3), ada_b.reshape(n, 1, d3))
    return out[:, :b]


def _nmm_kernel(x_ref, g_ref, sc_ref, sh_ref, w_ref, b_ref, o_ref, hn_ref):
    @pl.when(pl.program_id(1) == 0)
    def _():
        hn_ref[...] = _rms_mod(x_ref[...], g_ref[...], sc_ref[...], sh_ref[...]).astype(BF16)

    o_ref[...] = jnp.dot(hn_ref[...], w_ref[...], preferred_element_type=F32) + b_ref[...]


def norm_mod_matmul(x, g, scale, shift, w, bias, seq):
    m, d = x.shape
    n = w.shape[1]
    tm = min(512, seq)
    tn = _tile(n, 1024, 128)
    per_seq = seq // tm
    return pl.pallas_call(
        _nmm_kernel,
        grid=(m // tm, n // tn),
        in_specs=[
            pl.BlockSpec((tm, d), lambda i, j: (i, 0)),
            pl.BlockSpec((1, d), lambda i, j: (0, 0)),
            pl.BlockSpec((None, 1, d), lambda i, j: (i // per_seq, 0, 0)),
            pl.BlockSpec((None, 1, d), lambda i, j: (i // per_seq, 0, 0)),
            pl.BlockSpec((d, tn), lambda i, j: (0, j)),
            pl.BlockSpec((1, tn), lambda i, j: (0, j)),
        ],
        out_specs=pl.BlockSpec((tm, tn), lambda i, j: (i, j)),
        out_shape=jax.ShapeDtypeStruct((m, n), F32),
        scratch_shapes=[pltpu.VMEM((tm, d), BF16)],
        compiler_params=_params("parallel", "arbitrary"),
        name="norm_mod_matmul",
    )(x, g.reshape(1, d), scale, shift, w, bias.reshape(1, n))


def _mlp_kernel(x_ref, g_ref, sc_ref, sh_ref, gate_ref, w1_ref, w2_ref, fg_ref, o_ref,
                hn_ref, acc_ref, *, final_norm):
    k = pl.program_id(1)

    @pl.when(k == 0)
    def _():
        hn_ref[...] = _rms_mod(x_ref[...], g_ref[...], sc_ref[...], sh_ref[...]).astype(BF16)
        acc_ref[...] = jnp.zeros_like(acc_ref)

    h = jnp.dot(hn_ref[...], w1_ref[...], preferred_element_type=F32)
    h = jnp.square(jnp.maximum(h, 0.0))
    acc_ref[...] += jnp.dot(h.astype(BF16), w2_ref[...], preferred_element_type=F32)

    @pl.when(k == pl.num_programs(1) - 1)
    def _():
        out = x_ref[...] + gate_ref[...] * acc_ref[...]
        if final_norm:
            ms = jnp.mean(out * out, axis=-1, keepdims=True)
            out = out * lax.rsqrt(ms + NORM_EPS) * fg_ref[...]
        o_ref[...] = out


def mlp_block(x, g, scale, shift, gate, w1, w2, final_g, seq, final_norm):
    m, d = x.shape
    dff = w1.shape[1]
    tm = min(512, seq)
    tf = _tile(dff, 512, 128)
    per_seq = seq // tm
    vec = pl.BlockSpec((None, 1, d), lambda i, k: (i // per_seq, 0, 0))
    return pl.pallas_call(
        functools.partial(_mlp_kernel, final_norm=final_norm),
        grid=(m // tm, dff // tf),
        in_specs=[
            pl.BlockSpec((tm, d), lambda i, k: (i, 0)),
            pl.BlockSpec((1, d), lambda i, k: (0, 0)),
            vec, vec, vec,
            pl.BlockSpec((d, tf), lambda i, k: (0, k)),
            pl.BlockSpec((tf, d), lambda i, k: (k, 0)),
            pl.BlockSpec((1, d), lambda i, k: (0, 0)),
        ],
        out_specs=pl.BlockSpec((tm, d), lambda i, k: (i, 0)),
        out_shape=jax.ShapeDtypeStruct((m, d), F32),
        scratch_shapes=[pltpu.VMEM((tm, d), BF16), pltpu.VMEM((tm, d), F32)],
        compiler_params=_params("parallel", "arbitrary"),
        name="mlp",
    )(x, g.reshape(1, d), scale, shift, gate, w1, w2, final_g.reshape(1, d))


def _proj_res_kernel(a_ref, w_ref, x_ref, gate_ref, o_ref):
    o_ref[...] = x_ref[...] + gate_ref[...] * jnp.dot(
        a_ref[...], w_ref[...], preferred_element_type=F32)


def proj_residual(a, w, x, gate, seq):
    m, d = x.shape
    tm = min(512, seq)
    per_seq = seq // tm
    return pl.pallas_call(
        _proj_res_kernel,
        grid=(m // tm,),
        in_specs=[
            pl.BlockSpec((tm, d), lambda i: (i, 0)),
            pl.BlockSpec((d, d), lambda i: (0, 0)),
            pl.BlockSpec((tm, d), lambda i: (i, 0)),
            pl.BlockSpec((None, 1, d), lambda i: (i // per_seq, 0, 0)),
        ],
        out_specs=pl.BlockSpec((tm, d), lambda i: (i, 0)),
        out_shape=jax.ShapeDtypeStruct((m, d), F32),
        compiler_params=_params("parallel"),
        name="proj_residual",
    )(a, w, x, gate)


def _diag_block(q_i, b_i, k_rows, b_rows, v_rows, ones, rev):
    halves = GLA_SUB // SUBLANES
    row = lax.broadcasted_iota(jnp.int32, (SUBLANES, HEAD_DIM), 0)
    pieces, meta = [], []
    for s in range(GLA_SUB):
        for hh in range(halves):
            lo, hi = SUBLANES * hh, SUBLANES * hh + SUBLANES - 1
            needed = (lo <= s) if rev else (hi >= s)
            full = (hi <= s) if rev else (lo >= s)
            if not needed:
                continue
            d = b_i[lo:hi + 1] - b_rows[s]
            if not full:
                d = jnp.minimum(d, 0.0)
            pieces.append((q_i[lo:hi + 1] * k_rows[s] * jnp.exp(d)).astype(BF16))
            meta.append((s, hh, full))
    sums = jnp.dot(jnp.concatenate(pieces, axis=0), ones, preferred_element_type=F32)
    acc = [jnp.zeros((SUBLANES, HEAD_DIM), F32) for _ in range(halves)]
    for idx, (s, hh, full) in enumerate(meta):
        a = sums[SUBLANES * idx:SUBLANES * (idx + 1)]
        if not full:
            t = row + SUBLANES * hh
            a = jnp.where((t <= s) if rev else (t >= s), a, 0.0)
        acc[hh] = acc[hh] + a * v_rows[s]
    return jnp.concatenate(acc, axis=0)


def _gla_kernel(*refs, rev, n_chunks):
    if rev:
        (q_ref, v_ref, z_ref, lb_ref, tri_ref, of_ref, gt_ref, ng_ref, o_ref,
         st_ref, k_s, b_s) = refs
    else:
        q_ref, v_ref, z_ref, lb_ref, tri_ref, o_ref, st_ref, k_s, b_s = refs
    c_len, sub = GLA_CHUNK, GLA_SUB
    n_sub = c_len // sub

    @pl.when(pl.program_id(2) == 0)
    def _():
        st_ref[...] = jnp.zeros_like(st_ref)

    lb = lb_ref[...]
    tri = tri_ref[...]
    ones = jnp.ones((HEAD_DIM, HEAD_DIM), BF16)

    def chunk(ci, carry):
        cc = (n_chunks - 1 - ci) if rev else ci
        off = pl.multiple_of(cc * c_len, c_len)
        rows = pl.ds(off, c_len)
        qz = q_ref[rows, :]
        q = qz * jax.nn.sigmoid(qz)
        z = z_ref[rows, :]
        v = v_ref[rows, :]
        g = jnp.log(lb + (1.0 - lb) * jax.nn.sigmoid(z))
        k = (1.0 - lb) * jax.nn.sigmoid(-z)
        b = jnp.dot(tri, g, preferred_element_type=F32, precision=lax.Precision.HIGHEST)
        k_s[...] = k
        b_s[...] = b
        st = st_ref[...]
        o = lax.dot_general((q * jnp.exp(b)).astype(BF16), st.astype(BF16),
                            (((1,), (1,)), ((), ())), preferred_element_type=F32)
        outs = []
        for i in range(n_sub):
            lo = sub * i
            if rev:
                edge = b[lo + sub:lo + sub + 1] if i < n_sub - 1 else None
                other = slice(lo + sub, c_len)
            else:
                edge = b[lo - 1:lo] if i > 0 else None
                other = slice(0, lo)
            q_i, b_i = q[lo:lo + sub], b[lo:lo + sub]
            k_rows = [k_s[s:s + 1, :] for s in range(lo, lo + sub)]
            b_rows = [b_s[s:s + 1, :] for s in range(lo, lo + sub)]
            v_rows = [v_ref[pl.ds(off + s, 1), :] for s in range(lo, lo + sub)]
            o_i = o[lo:lo + sub] + _diag_block(q_i, b_i, k_rows, b_rows, v_rows, ones, rev)
            if edge is not None:
                qt = (q_i * jnp.exp(b_i - edge)).astype(BF16)
                kt = (k[other] * jnp.exp(edge - b[other])).astype(BF16)
                att = lax.dot_general(qt, kt, (((1,), (1,)), ((), ())),
                                      preferred_element_type=F32)
                o_i = o_i + jnp.dot(att.astype(BF16), v[other].astype(BF16),
                                    preferred_element_type=F32)
            outs.append(o_i)
        o = jnp.concatenate(outs, axis=0)
        b_end = b[0:1] if rev else b[c_len - 1:c_len]
        kh = (k * jnp.exp(b_end - b)).astype(BF16)
        st_ref[...] = st * jnp.exp(b_end) + lax.dot_general(
            v.astype(BF16), kh, (((0,), (0,)), ((), ())), preferred_element_type=F32)
        if rev:
            tot = of_ref[rows, :] + o
            ms = jnp.mean(tot * tot, axis=-1, keepdims=True)
            gt = gt_ref[rows, :]
            y = tot * lax.rsqrt(ms + NORM_EPS) * ng_ref[...] * (gt * jax.nn.sigmoid(gt))
            o_ref[rows, :] = y.astype(o_ref.dtype)
        else:
            o_ref[rows, :] = o
        return carry

    lax.fori_loop(0, n_chunks, chunk, 0)


def gla_direction(proj, lb, tri, norm_g, o_fwd, batch, seq, rev):
    m, d5 = proj.shape
    d = d5 // 5
    heads = d // HEAD_DIM
    t_blk = min(512, seq)
    n_t = seq // t_blk
    n_chunks = t_blk // GLA_CHUNK

    def rowblk(b, h, t):
        return b * n_t + ((n_t - 1 - t) if rev else t)

    def col(group):
        return pl.BlockSpec((t_blk, HEAD_DIM), lambda b, h, t: (rowblk(b, h, t), group * heads + h))

    in_specs = [col(0), col(1), col(3 if rev else 2),
                pl.BlockSpec((1, HEAD_DIM), lambda b, h, t: (0, h)),
                pl.BlockSpec((GLA_CHUNK, GLA_CHUNK), lambda b, h, t: (0, 0))]
    args = [proj, proj, proj, lb.reshape(1, d), tri]
    if rev:
        in_specs += [pl.BlockSpec((t_blk, HEAD_DIM), lambda b, h, t: (rowblk(b, h, t), h)),
                     col(4),
                     pl.BlockSpec((1, HEAD_DIM), lambda b, h, t: (0, 0))]
        args += [o_fwd, proj, norm_g.reshape(1, HEAD_DIM)]
    return pl.pallas_call(
        functools.partial(_gla_kernel, rev=rev, n_chunks=n_chunks),
        grid=(batch, heads, n_t),
        in_specs=in_specs,
        out_specs=pl.BlockSpec((t_blk, HEAD_DIM), lambda b, h, t: (rowblk(b, h, t), h)),
        out_shape=jax.ShapeDtypeStruct((m, d), BF16 if rev else F32),
        scratch_shapes=[pltpu.VMEM((HEAD_DIM, HEAD_DIM), F32),
                        pltpu.VMEM((GLA_CHUNK, HEAD_DIM), F32),
                        pltpu.VMEM((GLA_CHUNK, HEAD_DIM), F32)],
        compiler_params=_params("parallel", "parallel", "arbitrary"),
        name="gla_rev" if rev else "gla_fwd",
    )(*args)


def _short_conv_kernel(cur_ref, prev_ref, next_ref, w_ref, b_ref, x0_ref, z_ref, *, tiles_per_seq):
    i = pl.program_id(0)
    tm, d3 = cur_ref.shape
    d = d3 // 3
    cur = cur_ref[...]
    first = (i % tiles_per_seq) == 0
    last = (i % tiles_per_seq) == tiles_per_seq - 1
    prev_row = jnp.where(first, 0.0, prev_ref[SUBLANES - 1:SUBLANES, :])
    next_row = jnp.where(last, 0.0, next_ref[0:1, :])
    row = lax.broadcasted_iota(jnp.int32, (tm, d3), 0)
    before = jnp.where(row == 0, prev_row, pltpu.roll(cur, 1, 0))
    after = jnp.where(row == tm - 1, next_row, pltpu.roll(cur, tm - 1, 0))
    u = before * w_ref[0:1, :] + cur * w_ref[1:2, :] + after * w_ref[2:3, :] + b_ref[...]
    x0_ref[...] = u[:, :d]
    z_ref[...] = u[:, d:2 * d] * u[:, 2 * d:]


def short_conv_gate(u, conv_w, conv_b, seq):
    m, d3 = u.shape
    d = d3 // 3
    tm = min(256, seq)
    tiles_per_seq = seq // tm
    halo = tm // SUBLANES
    n_halo = m // SUBLANES
    return pl.pallas_call(
        functools.partial(_short_conv_kernel, tiles_per_seq=tiles_per_seq),
        grid=(m // tm,),
        in_specs=[
            pl.BlockSpec((tm, d3), lambda i: (i, 0)),
            pl.BlockSpec((SUBLANES, d3), lambda i: (jnp.maximum(i * halo - 1, 0), 0)),
            pl.BlockSpec((SUBLANES, d3), lambda i: (jnp.minimum((i + 1) * halo, n_halo - 1), 0)),
            pl.BlockSpec((3, d3), lambda i: (0, 0)),
            pl.BlockSpec((1, d3), lambda i: (0, 0)),
        ],
        out_specs=[pl.BlockSpec((tm, d), lambda i: (i, 0)),
                   pl.BlockSpec((tm, d), lambda i: (i, 0))],
        out_shape=[jax.ShapeDtypeStruct((m, d), F32), jax.ShapeDtypeStruct((m, d), F32)],
        compiler_params=_params("parallel"),
        name="short_conv",
    )(u, u, u, conv_w, conv_b.reshape(1, d3))


def _filter_kernel(emb_ref, t_ref, keep_ref, win_ref, wmid_ref, b_ref, freq_ref, wout_ref,
                   delta_ref, o_ref):
    hp = lax.Precision.HIGHEST
    freq = freq_ref[...]
    h = jnp.sin(freq * (jnp.dot(emb_ref[...], win_ref[...], preferred_element_type=F32,
                                precision=hp) + b_ref[0:1, :]))
    for mth in range(wmid_ref.shape[0]):
        h = jnp.sin(freq * (jnp.dot(h, wmid_ref[mth], preferred_element_type=F32, precision=hp)
                            + b_ref[mth + 1:mth + 2, :]))
    hf = jnp.dot(h, wout_ref[...], preferred_element_type=F32, precision=hp)
    o_ref[...] = hf * jnp.exp(-t_ref[...] * delta_ref[...]) * keep_ref[...]


def hyena_filter_circular(seq, d, w_in, w_mid, b, freq, w_out):
    emb_dim, width = w_in.shape
    bands = (emb_dim - 1) // 2
    t = jnp.linspace(0.0, 1.0, seq, dtype=F32)[:, None]
    band = jnp.linspace(1e-4, bands - 1.0, bands, dtype=F32)
    ang = (2.0 * math.pi / seq) * jnp.arange(seq, dtype=F32)[:, None] * band
    emb = jnp.concatenate([t, jnp.cos(ang), -jnp.sin(ang)], axis=-1)
    emb = jnp.pad(emb, ((0, 0), (0, HY_FILTER_PAD - emb_dim)))
    mrow = jnp.arange(2 * seq)
    pos = jnp.where(mrow < seq, mrow, jnp.where(mrow == seq, 0, 2 * seq - mrow))
    emb_c = emb[pos]
    t_c = t[pos]
    keep = (mrow != seq).astype(F32)[:, None]
    w_in_p = jnp.pad(w_in.astype(F32), ((0, HY_FILTER_PAD - emb_dim), (0, 0)))
    deltas = jnp.abs(jnp.linspace(HY_MIN_DECAY, HY_MAX_DECAY, d, dtype=F32)).reshape(1, d)
    tr = min(512, seq)
    half = seq // tr
    n_mid = w_mid.shape[0]
    return pl.pallas_call(
        _filter_kernel,
        grid=(2 * seq // tr,),
        in_specs=[
            pl.BlockSpec((tr, HY_FILTER_PAD), lambda i: (i, 0)),
            pl.BlockSpec((tr, 1), lambda i: (i, 0)),
            pl.BlockSpec((tr, 1), lambda i: (i, 0)),
            pl.BlockSpec((HY_FILTER_PAD, width), lambda i: (0, 0)),
            pl.BlockSpec((n_mid, width, width), lambda i: (0, 0, 0)),
            pl.BlockSpec((n_mid + 1, width), lambda i: (0, 0)),
            pl.BlockSpec((1, width), lambda i: (0, 0)),
            pl.BlockSpec((width, d), lambda i: (0, i // half)),
            pl.BlockSpec((1, d), lambda i: (0, 0)),
        ],
        out_specs=pl.BlockSpec((tr, d), lambda i: (i, 0)),
        out_shape=jax.ShapeDtypeStruct((2 * seq, d), F32),
        compiler_params=_params("parallel"),
        name="hyena_filter",
    )(emb_c, t_c, keep, w_in_p, w_mid.astype(F32), b.astype(F32),
      freq.astype(F32).reshape(1, width), w_out.astype(F32), deltas)


def _left_matmul_kernel(m_ref, x_ref, o_ref):
    o_ref[...] = jnp.dot(m_ref[...], x_ref[...].astype(BF16),
                         preferred_element_type=F32).astype(o_ref.dtype)


def left_matmul(mat, x, out_dtype):
    r, kdim = mat.shape
    cols = x.shape[1]
    tc = min(2048, cols)
    return pl.pallas_call(
        _left_matmul_kernel,
        grid=(cols // tc,),
        in_specs=[pl.BlockSpec((r, kdim), lambda j: (0, 0)),
                  pl.BlockSpec((kdim, tc), lambda j: (0, j))],
        out_specs=pl.BlockSpec((r, tc), lambda j: (0, j)),
        out_shape=jax.ShapeDtypeStruct((r, cols), out_dtype),
        compiler_params=_params("parallel"),
        name="dft_stage1",
    )(mat, x)


def _spectrum_kernel(g_ref, a_ref, o_ref):
    n2 = a_ref.shape[1]
    a = a_ref[...].reshape(2 * n2, a_ref.shape[2])
    o_ref[...] = jnp.dot(g_ref[...], a, preferred_element_type=F32).reshape(o_ref.shape)


def filter_spectrum(g_tab, a):
    _, n1, n2, d = a.shape
    return pl.pallas_call(
        _spectrum_kernel,
        grid=(n1,),
        in_specs=[pl.BlockSpec((None, 2 * n2, 2 * n2), lambda k: (k, 0, 0)),
                  pl.BlockSpec((2, None, n2, d), lambda k: (0, k, 0, 0))],
        out_specs=pl.BlockSpec((None, 2, n2, d), lambda k: (k, 0, 0, 0)),
        out_shape=jax.ShapeDtypeStruct((n1, 2, n2, d), F32),
        compiler_params=_params("parallel"),
        name="filter_spectrum",
    )(g_tab, a)


def _freq_mul_kernel(g_ref, gi_ref, a_ref, h_ref, o_ref):
    n2 = a_ref.shape[1]
    d = a_ref.shape[2]
    a = a_ref[...].reshape(2 * n2, d)
    x = jnp.dot(g_ref[...], a, preferred_element_type=F32)
    xr, xi = x[:n2], x[n2:]
    hr, hi = h_ref[0], h_ref[1]
    p = jnp.concatenate([xr * hr - xi * hi, xr * hi + xi * hr], axis=0).astype(BF16)
    o_ref[...] = jnp.dot(gi_ref[...], p, preferred_element_type=F32).astype(o_ref.dtype).reshape(o_ref.shape)


def freq_multiply(g_tab, gi_tab, a, h):
    _, n1, n2, d = a.shape
    return pl.pallas_call(
        _freq_mul_kernel,
        grid=(n1,),
        in_specs=[pl.BlockSpec((None, 2 * n2, 2 * n2), lambda k: (k, 0, 0)),
                  pl.BlockSpec((None, 2 * n2, 2 * n2), lambda k: (k, 0, 0)),
                  pl.BlockSpec((2, None, n2, d), lambda k: (0, k, 0, 0)),
                  pl.BlockSpec((None, 2, n2, d), lambda k: (k, 0, 0, 0))],
        out_specs=pl.BlockSpec((2, None, n2, d), lambda k: (0, k, 0, 0)),
        out_shape=jax.ShapeDtypeStruct((2, n1, n2, d), BF16),
        compiler_params=_params("parallel"),
        name="freq_multiply",
    )(g_tab, gi_tab, a, h)


def _conv_out_kernel(m_ref, t_ref, x0_ref, z_ref, skip_ref, o_ref):
    y = jnp.dot(m_ref[...], t_ref[...], preferred_element_type=F32)
    o_ref[...] = (x0_ref[...] * (y + skip_ref[...] * z_ref[...])).astype(o_ref.dtype)


def conv_output(mat, t, x0, z, skip_tiled):
    r, kdim = mat.shape
    cols = t.shape[1]
    tc = skip_tiled.shape[1]
    return pl.pallas_call(
        _conv_out_kernel,
        grid=(cols // tc,),
        in_specs=[pl.BlockSpec((r, kdim), lambda j: (0, 0)),
                  pl.BlockSpec((kdim, tc), lambda j: (0, j)),
                  pl.BlockSpec((r, tc), lambda j: (0, j)),
                  pl.BlockSpec((r, tc), lambda j: (0, j)),
                  pl.BlockSpec((1, tc), lambda j: (0, 0))],
        out_specs=pl.BlockSpec((r, tc), lambda j: (0, j)),
        out_shape=jax.ShapeDtypeStruct((r, cols), BF16),
        compiler_params=_params("parallel"),
        name="dft_inverse_out",
    )(mat, t, x0, z, skip_tiled)


def _dft_tables(seq):
    n = 2 * seq
    n2 = FFT_N2
    n1 = n // n2
    lh = seq // n2

    def cs(idx, mod):
        ang = (2.0 * math.pi / mod) * (idx % mod).astype(F32)
        return jnp.cos(ang), jnp.sin(ang)

    k1 = jnp.arange(n1)
    cr, sr = cs(k1[:, None] * jnp.arange(lh)[None, :], n1)
    f_sig = jnp.block([[cr, sr], [-sr, cr]])
    cf, sf = cs(k1[:, None] * k1[None, :], n1)
    f_filt = jnp.concatenate([cf, -sf], axis=0)
    idx = jnp.arange(n2)[None, None, :] * (k1[:, None, None] + n1 * jnp.arange(n2)[None, :, None])
    gr, gs = cs(idx, n)
    g_tab = jnp.concatenate([jnp.concatenate([gr, gs], axis=2),
                             jnp.concatenate([-gs, gr], axis=2)], axis=1)
    gi_tab = jnp.swapaxes(g_tab, 1, 2)
    er, es = cs(jnp.arange(lh)[:, None] * k1[None, :], n1)
    f_inv = jnp.block([[er, -es], [es, er]]) / n
    return (f_sig.astype(BF16), f_filt.astype(BF16), g_tab.astype(BF16), gi_tab.astype(BF16),
            f_inv.astype(BF16))


def hyena_long_conv(x0, z, filt, skip, tables, batch, seq):
    assert batch == 2, "the two sequences ride as real / imaginary parts of one complex signal"
    f_sig, f_filt, g_tab, gi_tab, f_inv = tables
    m, d = z.shape
    n2 = FFT_N2
    n1 = 2 * seq // n2
    lh = seq // n2
    cols = n2 * d
    a_f = left_matmul(f_filt, filt.reshape(n1, cols), BF16)
    h = filter_spectrum(g_tab, a_f.reshape(2, n1, n2, d))
    a = left_matmul(f_sig, z.reshape(batch * lh, cols), BF16)
    t = freq_multiply(g_tab, gi_tab, a.reshape(2, n1, n2, d), h)
    tc = min(2048, cols)
    skip_tiled = jnp.tile(skip.astype(F32).reshape(1, d), (1, tc // d))
    y = conv_output(f_inv, t.reshape(2 * n1, cols), x0.reshape(batch * lh, cols),
                    z.reshape(batch * lh, cols), skip_tiled)
    return y.reshape(m, d)


def kernel(x, c, ada_w, ada_b, norm_g, hg_w_in, hg_lower_bounds, hg_norm_g, hg_w_out,
           hy_w_in, hy_b_in, hy_conv_w, hy_conv_b, hy_filt_w_in, hy_filt_w_mid, hy_filt_b,
           hy_filt_freq, hy_filt_w_out, hy_skip, hy_w_out, mlp_w1, mlp_w2, final_g):
    batch, seq, d = x.shape
    depth = ada_w.shape[0]
    n_mixers = 2
    assert d % HEAD_DIM == 0 and seq % max(GLA_CHUNK, FFT_N2) == 0

    mod = adaln_all(c, ada_w, ada_b)
    mod = mod.reshape(2 * depth, batch, 3, 1, d)

    lbs = jax.nn.softmax(hg_lower_bounds.astype(F32), axis=1)
    lbs = jnp.cumsum(lbs, axis=1) - lbs[:, :1]

    tri = (jnp.arange(GLA_CHUNK)[:, None] >= jnp.arange(GLA_CHUNK)[None, :]).astype(F32)
    tables = _dft_tables(seq)
    zero_bias = jnp.zeros((5 * d,), F32)

    xf = x.reshape(batch * seq, d)
    for i in range(depth):
        j = i // n_mixers
        shift, scale, gate = mod[2 * i, :, 0], mod[2 * i, :, 1], mod[2 * i, :, 2]
        if i % n_mixers == 0:
            proj = norm_mod_matmul(xf, norm_g[i, 0], scale, shift, hg_w_in[j].astype(BF16),
                                   zero_bias, seq)
            o_f = gla_direction(proj, lbs[0, j], tri, hg_norm_g[j], None, batch, seq, rev=False)
            o = gla_direction(proj, lbs[1, j], tri.T, hg_norm_g[j], o_f, batch, seq, rev=True)
            xf = proj_residual(o, hg_w_out[j].astype(BF16), xf, gate, seq)
        else:
            u = norm_mod_matmul(xf, norm_g[i, 0], scale, shift, hy_w_in[j].astype(BF16),
                                hy_b_in[j], seq)
            x0, z = short_conv_gate(u, hy_conv_w[j], hy_conv_b[j], seq)
            filt = hyena_filter_circular(seq, d, hy_filt_w_in[j], hy_filt_w_mid[j], hy_filt_b[j],
                                         hy_filt_freq[j], hy_filt_w_out[j])
            y = hyena_long_conv(x0, z, filt, hy_skip[j], tables, batch, seq)
            xf = proj_residual(y, hy_w_out[j].astype(BF16), xf, gate, seq)
        shift, scale, gate = mod[2 * i + 1, :, 0], mod[2 * i + 1, :, 1], mod[2 * i + 1, :, 2]
        xf = mlp_block(xf, norm_g[i, 1], scale, shift, gate, mlp_w1[i].astype(BF16),
                       mlp_w2[i].astype(BF16), final_g, seq, final_norm=(i == depth - 1))
    return xf.reshape(batch, seq, d)
```

```python
import functools
import math

import jax
import jax.numpy as jnp
from jax import lax
from jax.experimental import pallas as pl
from jax.experimental.pallas import tpu as pltpu

F32 = jnp.float32
BF16 = jnp.bfloat16

NORM_EPS = 1e-6
HEAD_DIM = 128
GLA_CHUNK = 64
GLA_SUB = 16
GLA_UNROLL_GATES = 8
GLA_UNROLL_SCORES = 4
GLA_UNROLL_OUT = 8
SUBLANES = 8
FFT_N2 = 128
HY_FILTER_PAD = 64
HY_MAX_DECAY = math.log(1e-2) / 0.3
HY_MIN_DECAY = math.log(1e-2) / 1.5
VMEM_LIMIT = 56 * 1024 * 1024


def _params(*sem):
    return pltpu.CompilerParams(dimension_semantics=sem, vmem_limit_bytes=VMEM_LIMIT)


def _tile(n, target, unit):
    t = min(n, target) // unit * unit
    while n % t:
        t -= unit
    return t


def _rms_mod(x, g, scale, shift):
    ms = jnp.mean(x * x, axis=-1, keepdims=True)
    y = x * lax.rsqrt(ms + NORM_EPS) * g
    return y * (1.0 + scale) + shift


def _adaln_kernel(s_ref, w_ref, b_ref, o_ref):
    s = s_ref[...]
    s = s * jax.nn.sigmoid(s)
    o_ref[...] = jnp.dot(s, w_ref[...], preferred_element_type=F32,
                         precision=lax.Precision.HIGHEST) + b_ref[...]


def adaln_all(c, ada_w, ada_b):
    depth, two, d, d3 = ada_w.shape
    n = depth * two
    b = c.shape[0]
    rows = -(-b // SUBLANES) * SUBLANES
    cp = jnp.zeros((rows, d), F32).at[:b].set(c)
    tn = _tile(d3, 1536, 128)
    out = pl.pallas_call(
        _adaln_kernel,
        grid=(n, d3 // tn),
        in_specs=[
            pl.BlockSpec((rows, d), lambda i, j: (0, 0)),
            pl.BlockSpec((None, d, tn), lambda i, j: (i, 0, j)),
            pl.BlockSpec((None, 1, tn), lambda i, j: (i, 0, j)),
        ],
        out_specs=pl.BlockSpec((None, rows, tn), lambda i, j: (i, 0, j)),
        out_shape=jax.ShapeDtypeStruct((n, rows, d3), F32),
        compiler_params=_params("parallel", "parallel"),
        name="adaln",
    )(cp, ada_w.reshape(n, d, d3), ada_b.reshape(n, 1, d3))
    return out[:, :b]


def _nmm_kernel(x_ref, g_ref, sc_ref, sh_ref, w_ref, b_ref, o_ref, hn_ref):
    @pl.when(pl.program_id(1) == 0)
    def _():
        hn_ref[...] = _rms_mod(x_ref[...], g_ref[...], sc_ref[...], sh_ref[...]).astype(BF16)

    o_ref[...] = jnp.dot(hn_ref[...], w_ref[...], preferred_element_type=F32) + b_ref[...]


def norm_mod_matmul(x, g, scale, shift, w, bias, seq):
    m, d = x.shape
    n = w.shape[1]
    tm = min(512, seq)
    tn = _tile(n, 1024, 128)
    per_seq = seq // tm
    return pl.pallas_call(
        _nmm_kernel,
        grid=(m // tm, n // tn),
        in_specs=[
            pl.BlockSpec((tm, d), lambda i, j: (i, 0)),
            pl.BlockSpec((1, d), lambda i, j: (0, 0)),
            pl.BlockSpec((None, 1, d), lambda i, j: (i // per_seq, 0, 0)),
            pl.BlockSpec((None, 1, d), lambda i, j: (i // per_seq, 0, 0)),
            pl.BlockSpec((d, tn), lambda i, j: (0, j)),
            pl.BlockSpec((1, tn), lambda i, j: (0, j)),
        ],
        out_specs=pl.BlockSpec((tm, tn), lambda i, j: (i, j)),
        out_shape=jax.ShapeDtypeStruct((m, n), F32),
        scratch_shapes=[pltpu.VMEM((tm, d), BF16)],
        compiler_params=_params("parallel", "arbitrary"),
        name="norm_mod_matmul",
    )(x, g.reshape(1, d), scale, shift, w, bias.reshape(1, n))


def _mlp_kernel(x_ref, g_ref, sc_ref, sh_ref, gate_ref, w1_ref, w2_ref, fg_ref, o_ref,
                hn_ref, acc_ref, *, final_norm):
    k = pl.program_id(1)

    @pl.when(k == 0)
    def _():
        hn_ref[...] = _rms_mod(x_ref[...], g_ref[...], sc_ref[...], sh_ref[...]).astype(BF16)
        acc_ref[...] = jnp.zeros_like(acc_ref)

    h = jnp.dot(hn_ref[...], w1_ref[...], preferred_element_type=F32)
    h = jnp.square(jnp.maximum(h, 0.0))
    acc_ref[...] += jnp.dot(h.astype(BF16), w2_ref[...], preferred_element_type=F32)

    @pl.when(k == pl.num_programs(1) - 1)
    def _():
        out = x_ref[...] + gate_ref[...] * acc_ref[...]
        if final_norm:
            ms = jnp.mean(out * out, axis=-1, keepdims=True)
            out = out * lax.rsqrt(ms + NORM_EPS) * fg_ref[...]
        o_ref[...] = out


def mlp_block(x, g, scale, shift, gate, w1, w2, final_g, seq, final_norm):
    m, d = x.shape
    dff = w1.shape[1]
    tm = min(512, seq)
    tf = _tile(dff, 512, 128)
    per_seq = seq // tm
    vec = pl.BlockSpec((None, 1, d), lambda i, k: (i // per_seq, 0, 0))
    return pl.pallas_call(
        functools.partial(_mlp_kernel, final_norm=final_norm),
        grid=(m // tm, dff // tf),
        in_specs=[
            pl.BlockSpec((tm, d), lambda i, k: (i, 0)),
            pl.BlockSpec((1, d), lambda i, k: (0, 0)),
            vec, vec, vec,
            pl.BlockSpec((d, tf), lambda i, k: (0, k)),
            pl.BlockSpec((tf, d), lambda i, k: (k, 0)),
            pl.BlockSpec((1, d), lambda i, k: (0, 0)),
        ],
        out_specs=pl.BlockSpec((tm, d), lambda i, k: (i, 0)),
        out_shape=jax.ShapeDtypeStruct((m, d), F32),
        scratch_shapes=[pltpu.VMEM((tm, d), BF16), pltpu.VMEM((tm, d), F32)],
        compiler_params=_params("parallel", "arbitrary"),
        name="mlp",
    )(x, g.reshape(1, d), scale, shift, gate, w1, w2, final_g.reshape(1, d))


def _proj_res_kernel(a_ref, w_ref, x_ref, gate_ref, o_ref):
    o_ref[...] = x_ref[...] + gate_ref[...] * jnp.dot(
        a_ref[...], w_ref[...], preferred_element_type=F32)


def proj_residual(a, w, x, gate, seq):
    m, d = x.shape
    tm = min(512, seq)
    per_seq = seq // tm
    return pl.pallas_call(
        _proj_res_kernel,
        grid=(m // tm,),
        in_specs=[
            pl.BlockSpec((tm, d), lambda i: (i, 0)),
            pl.BlockSpec((d, d), lambda i: (0, 0)),
            pl.BlockSpec((tm, d), lambda i: (i, 0)),
            pl.BlockSpec((None, 1, d), lambda i: (i // per_seq, 0, 0)),
        ],
        out_specs=pl.BlockSpec((tm, d), lambda i: (i, 0)),
        out_shape=jax.ShapeDtypeStruct((m, d), F32),
        compiler_params=_params("parallel"),
        name="proj_residual",
    )(a, w, x, gate)


def _gla_kernel(*refs, rev, n_chunks):
    if rev:
        (q_ref, v_ref, z_ref, lb_ref, tri_ref, e_ref, of_ref, gt_ref, ng_ref, o_ref,
         st_s, q_s, b2_s, gam_s, qe_s, dec_s, upd_s, seen_s, att_s, p_s, dg_s) = refs
    else:
        (q_ref, v_ref, z_ref, lb_ref, tri_ref, e_ref, o_ref,
         st_s, q_s, b2_s, gam_s, qe_s, dec_s, upd_s, seen_s, att_s, p_s, dg_s) = refs
    c_len, sub, half = GLA_CHUNK, GLA_SUB, SUBLANES
    n_sub = c_len // sub
    nt = (((1,), (1,)), ((), ()))
    tn = (((0,), (0,)), ((), ()))

    @pl.when(pl.program_id(2) == 0)
    def _():
        st_s[...] = jnp.zeros_like(st_s)

    lb = lb_ref[...]
    one_minus_lb = 1.0 - lb
    log2_oml = jnp.log2(one_minus_lb)
    tri2 = tri_ref[...]

    def gates(ci, carry):
        rows = pl.ds(pl.multiple_of(ci * c_len, c_len), c_len)
        qz = q_ref[rows, :]
        z = z_ref[rows, :]
        q = qz * (1.0 / (1.0 + jnp.exp(-qz)))
        g2 = jnp.log2(lb + one_minus_lb * (1.0 / (1.0 + jnp.exp(-z))))
        lk2 = log2_oml - jnp.log2(1.0 + jnp.exp(z))
        hi = g2.astype(BF16)
        lo = (g2 - hi.astype(F32)).astype(BF16)
        b2 = jnp.dot(tri2, jnp.concatenate([hi, lo], axis=0), preferred_element_type=F32)
        q_s[rows, :] = q
        b2_s[rows, :] = b2
        gam_s[rows, :] = lk2 - b2
        return carry

    lax.fori_loop(0, n_chunks, gates, 0, unroll=GLA_UNROLL_GATES)

    def scores(ci, carry):
        off = pl.multiple_of(ci * c_len, c_len)
        rows = pl.ds(off, c_len)
        q = q_s[rows, :]
        b2 = b2_s[rows, :]
        gam = gam_s[rows, :]
        b2_end = b2[0:1] if rev else b2[c_len - 1:c_len]
        qe_s[rows, :] = (q * jnp.exp2(b2)).astype(BF16)
        upd_s[ci] = lax.dot_general(v_ref[rows, :].astype(BF16),
                                    jnp.exp2(b2_end + gam).astype(BF16), tn,
                                    preferred_element_type=F32)
        dec_s[ci] = jnp.exp2(b2_end)
        atts = []
        for i in range(n_sub):
            r0 = sub * i
            q_i, b_i = q[r0:r0 + sub], b2[r0:r0 + sub]
            if rev:
                others = (r0 + sub, c_len)
                edge = b2[r0 + sub:r0 + sub + 1] if i < n_sub - 1 else None
            else:
                others = (0, r0)
                edge = b2[r0 - 1:r0] if i > 0 else None
            if edge is None:
                atts.append(jnp.zeros((sub, HEAD_DIM), F32))
            else:
                qt = (q_i * jnp.exp2(b_i - edge)).astype(BF16)
                kt = jnp.exp2(edge + gam[others[0]:others[1]]).astype(BF16)
                parts = []
                if others[0]:
                    parts.append(jnp.zeros((others[0], HEAD_DIM), BF16))
                parts.append(kt)
                parts.append(jnp.zeros((HEAD_DIM - others[1], HEAD_DIM), BF16))
                atts.append(lax.dot_general(qt, jnp.concatenate(parts, axis=0), nt,
                                            preferred_element_type=F32))
            for sl in range(sub):
                g_row = gam_s[pl.ds(off + r0 + sl, 1), :]
                halves = []
                for hh in range(sub // half):
                    h0, h1 = half * hh, half * hh + half - 1
                    needed = (h0 <= sl) if rev else (h1 >= sl)
                    full = (h1 <= sl) if rev else (h0 >= sl)
                    if not needed:
                        halves.append(jnp.zeros((half, HEAD_DIM), F32))
                        continue
                    arg = b_i[h0:h1 + 1] + g_row
                    if not full:
                        arg = jnp.minimum(arg, 0.0)
                    halves.append(q_i[h0:h1 + 1] * jnp.exp2(arg))
                p_s[i, pl.ds(pl.multiple_of(ci * sub, sub), sub),
                    sl * HEAD_DIM:(sl + 1) * HEAD_DIM] = jnp.concatenate(halves, axis=0).astype(BF16)
        att_s[rows, :] = jnp.concatenate(atts, axis=0)
        return carry

    lax.fori_loop(0, n_chunks, scores, 0, unroll=GLA_UNROLL_SCORES)

    for i in range(n_sub):
        dg_s[i] = jnp.dot(p_s[i], e_ref[i], preferred_element_type=F32)

    row = lax.broadcasted_iota(jnp.int32, (c_len, HEAD_DIM), 0)
    col = lax.broadcasted_iota(jnp.int32, (c_len, HEAD_DIM), 1)
    keep = (col >= row) if rev else (col <= row)
    v_pad = jnp.zeros((HEAD_DIM - c_len, HEAD_DIM), BF16)

    def state_step(cj, st):
        ci = (n_chunks - 1 - cj) if rev else cj
        seen_s[ci] = st.astype(BF16)
        return st * dec_s[ci] + upd_s[ci]

    st_s[...] = lax.fori_loop(0, n_chunks, state_step, st_s[...], unroll=True)

    def outputs(ci, carry):
        off = pl.multiple_of(ci * c_len, c_len)
        rows = pl.ds(off, c_len)
        blk = pl.ds(pl.multiple_of(ci * sub, sub), sub)
        att = att_s[rows, :] + jnp.concatenate([dg_s[i, blk, :] for i in range(n_sub)], axis=0)
        att = jnp.where(keep, att, 0.0).astype(BF16)
        v = v_ref[rows, :].astype(BF16)
        o = (jnp.dot(att, jnp.concatenate([v, v_pad], axis=0), preferred_element_type=F32)
             + lax.dot_general(qe_s[rows, :], seen_s[ci], nt, preferred_element_type=F32))
        if rev:
            tot = of_ref[rows, :] + o
            ms = jnp.mean(tot * tot, axis=-1, keepdims=True)
            gt = gt_ref[rows, :]
            y = tot * lax.rsqrt(ms + NORM_EPS) * ng_ref[...] * (gt * jax.nn.sigmoid(gt))
            o_ref[rows, :] = y.astype(o_ref.dtype)
        else:
            o_ref[rows, :] = o
        return carry

    lax.fori_loop(0, n_chunks, outputs, 0, unroll=GLA_UNROLL_OUT)


def _gla_tables(rev):
    idx = jnp.arange(GLA_CHUNK)
    tri = (idx[:, None] <= idx[None, :]) if rev else (idx[:, None] >= idx[None, :])
    tri2 = jnp.concatenate([tri, tri], axis=1).astype(BF16)
    s_of_row = jnp.arange(GLA_SUB * HEAD_DIM) // HEAD_DIM
    place = (jnp.arange(HEAD_DIM)[None, None, :]
             == GLA_SUB * jnp.arange(GLA_CHUNK // GLA_SUB)[:, None, None] + s_of_row[None, :, None])
    return tri2, place.astype(BF16)


def gla_direction(proj, lb, norm_g, o_fwd, batch, seq, rev):
    m, d5 = proj.shape
    d = d5 // 5
    heads = d // HEAD_DIM
    t_blk = min(1024, seq)
    n_t = seq // t_blk
    n_chunks = t_blk // GLA_CHUNK
    n_sub = GLA_CHUNK // GLA_SUB
    tri2, place = _gla_tables(rev)

    def rowblk(b, h, t):
        return b * n_t + ((n_t - 1 - t) if rev else t)

    def col(group):
        return pl.BlockSpec((t_blk, HEAD_DIM), lambda b, h, t: (rowblk(b, h, t), group * heads + h))

    in_specs = [col(0), col(1), col(3 if rev else 2),
                pl.BlockSpec((1, HEAD_DIM), lambda b, h, t: (0, h)),
                pl.BlockSpec(tri2.shape, lambda b, h, t: (0, 0)),
                pl.BlockSpec(place.shape, lambda b, h, t: (0, 0, 0))]
    args = [proj, proj, proj, lb.reshape(1, d), tri2, place]
    if rev:
        in_specs += [pl.BlockSpec((t_blk, HEAD_DIM), lambda b, h, t: (rowblk(b, h, t), h)),
                     col(4),
                     pl.BlockSpec((1, HEAD_DIM), lambda b, h, t: (0, 0))]
        args += [o_fwd, proj, norm_g.reshape(1, HEAD_DIM)]
    return pl.pallas_call(
        functools.partial(_gla_kernel, rev=rev, n_chunks=n_chunks),
        grid=(batch, heads, n_t),
        in_specs=in_specs,
        out_specs=pl.BlockSpec((t_blk, HEAD_DIM), lambda b, h, t: (rowblk(b, h, t), h)),
        out_shape=jax.ShapeDtypeStruct((m, d), BF16 if rev else F32),
        scratch_shapes=[pltpu.VMEM((HEAD_DIM, HEAD_DIM), F32),
                        pltpu.VMEM((t_blk, HEAD_DIM), F32),
                        pltpu.VMEM((t_blk, HEAD_DIM), F32),
                        pltpu.VMEM((t_blk, HEAD_DIM), F32),
                        pltpu.VMEM((t_blk, HEAD_DIM), BF16),
                        pltpu.VMEM((n_chunks, 1, HEAD_DIM), F32),
                        pltpu.VMEM((n_chunks, HEAD_DIM, HEAD_DIM), F32),
                        pltpu.VMEM((n_chunks, HEAD_DIM, HEAD_DIM), BF16),
                        pltpu.VMEM((t_blk, HEAD_DIM), F32),
                        pltpu.VMEM((n_sub, t_blk // n_sub, GLA_SUB * HEAD_DIM), BF16),
                        pltpu.VMEM((n_sub, t_blk // n_sub, HEAD_DIM), F32)],
        compiler_params=_params("parallel", "parallel", "arbitrary"),
        name="gla_rev" if rev else "gla_fwd",
    )(*args)


def _short_conv_kernel(cur_ref, prev_ref, next_ref, w_ref, b_ref, x0_ref, z_ref, *, tiles_per_seq):
    i = pl.program_id(0)
    tm, d3 = cur_ref.shape
    d = d3 // 3
    cur = cur_ref[...]
    first = (i % tiles_per_seq) == 0
    last = (i % tiles_per_seq) == tiles_per_seq - 1
    prev_row = jnp.where(first, 0.0, prev_ref[SUBLANES - 1:SUBLANES, :])
    next_row = jnp.where(last, 0.0, next_ref[0:1, :])
    row = lax.broadcasted_iota(jnp.int32, (tm, d3), 0)
    before = jnp.where(row == 0, prev_row, pltpu.roll(cur, 1, 0))
    after = jnp.where(row == tm - 1, next_row, pltpu.roll(cur, tm - 1, 0))
    u = before * w_ref[0:1, :] + cur * w_ref[1:2, :] + after * w_ref[2:3, :] + b_ref[...]
    x0_ref[...] = u[:, :d]
    z_ref[...] = u[:, d:2 * d] * u[:, 2 * d:]


def short_conv_gate(u, conv_w, conv_b, seq):
    m, d3 = u.shape
    d = d3 // 3
    tm = min(256, seq)
    tiles_per_seq = seq // tm
    halo = tm // SUBLANES
    n_halo = m // SUBLANES
    return pl.pallas_call(
        functools.partial(_short_conv_kernel, tiles_per_seq=tiles_per_seq),
        grid=(m // tm,),
        in_specs=[
            pl.BlockSpec((tm, d3), lambda i: (i, 0)),
            pl.BlockSpec((SUBLANES, d3), lambda i: (jnp.maximum(i * halo - 1, 0), 0)),
            pl.BlockSpec((SUBLANES, d3), lambda i: (jnp.minimum((i + 1) * halo, n_halo - 1), 0)),
            pl.BlockSpec((3, d3), lambda i: (0, 0)),
            pl.BlockSpec((1, d3), lambda i: (0, 0)),
        ],
        out_specs=[pl.BlockSpec((tm, d), lambda i: (i, 0)),
                   pl.BlockSpec((tm, d), lambda i: (i, 0))],
        out_shape=[jax.ShapeDtypeStruct((m, d), F32), jax.ShapeDtypeStruct((m, d), F32)],
        compiler_params=_params("parallel"),
        name="short_conv",
    )(u, u, u, conv_w, conv_b.reshape(1, d3))


def _filter_kernel(emb_ref, t_ref, keep_ref, win_ref, wmid_ref, b_ref, freq_ref, wout_ref,
                   delta_ref, o_ref):
    hp = lax.Precision.HIGHEST
    freq = freq_ref[...]
    h = jnp.sin(freq * (jnp.dot(emb_ref[...], win_ref[...], preferred_element_type=F32,
                                precision=hp) + b_ref[0:1, :]))
    for mth in range(wmid_ref.shape[0]):
        h = jnp.sin(freq * (jnp.dot(h, wmid_ref[mth], preferred_element_type=F32, precision=hp)
                            + b_ref[mth + 1:mth + 2, :]))
    hf = jnp.dot(h, wout_ref[...], preferred_element_type=F32, precision=hp)
    o_ref[...] = hf * jnp.exp(-t_ref[...] * delta_ref[...]) * keep_ref[...]


def hyena_filter_circular(seq, d, w_in, w_mid, b, freq, w_out):
    emb_dim, width = w_in.shape
    bands = (emb_dim - 1) // 2
    t = jnp.linspace(0.0, 1.0, seq, dtype=F32)[:, None]
    band = jnp.linspace(1e-4, bands - 1.0, bands, dtype=F32)
    ang = (2.0 * math.pi / seq) * jnp.arange(seq, dtype=F32)[:, None] * band
    emb = jnp.concatenate([t, jnp.cos(ang), -jnp.sin(ang)], axis=-1)
    emb = jnp.pad(emb, ((0, 0), (0, HY_FILTER_PAD - emb_dim)))
    mrow = jnp.arange(2 * seq)
    pos = jnp.where(mrow < seq, mrow, jnp.where(mrow == seq, 0, 2 * seq - mrow))
    emb_c = emb[pos]
    t_c = t[pos]
    keep = (mrow != seq).astype(F32)[:, None]
    w_in_p = jnp.pad(w_in.astype(F32), ((0, HY_FILTER_PAD - emb_dim), (0, 0)))
    deltas = jnp.abs(jnp.linspace(HY_MIN_DECAY, HY_MAX_DECAY, d, dtype=F32)).reshape(1, d)
    tr = min(512, seq)
    half = seq // tr
    n_mid = w_mid.shape[0]
    return pl.pallas_call(
        _filter_kernel,
        grid=(2 * seq // tr,),
        in_specs=[
            pl.BlockSpec((tr, HY_FILTER_PAD), lambda i: (i, 0)),
            pl.BlockSpec((tr, 1), lambda i: (i, 0)),
            pl.BlockSpec((tr, 1), lambda i: (i, 0)),
            pl.BlockSpec((HY_FILTER_PAD, width), lambda i: (0, 0)),
            pl.BlockSpec((n_mid, width, width), lambda i: (0, 0, 0)),
            pl.BlockSpec((n_mid + 1, width), lambda i: (0, 0)),
            pl.BlockSpec((1, width), lambda i: (0, 0)),
            pl.BlockSpec((width, d), lambda i: (0, i // half)),
            pl.BlockSpec((1, d), lambda i: (0, 0)),
        ],
        out_specs=pl.BlockSpec((tr, d), lambda i: (i, 0)),
        out_shape=jax.ShapeDtypeStruct((2 * seq, d), F32),
        compiler_params=_params("parallel"),
        name="hyena_filter",
    )(emb_c, t_c, keep, w_in_p, w_mid.astype(F32), b.astype(F32),
      freq.astype(F32).reshape(1, width), w_out.astype(F32), deltas)


def _left_matmul_kernel(m_ref, x_ref, o_ref):
    o_ref[...] = jnp.dot(m_ref[...], x_ref[...].astype(BF16),
                         preferred_element_type=F32).astype(o_ref.dtype)


def left_matmul(mat, x, out_dtype):
    r, kdim = mat.shape
    cols = x.shape[1]
    tc = min(2048, cols)
    return pl.pallas_call(
        _left_matmul_kernel,
        grid=(cols // tc,),
        in_specs=[pl.BlockSpec((r, kdim), lambda j: (0, 0)),
                  pl.BlockSpec((kdim, tc), lambda j: (0, j))],
        out_specs=pl.BlockSpec((r, tc), lambda j: (0, j)),
        out_shape=jax.ShapeDtypeStruct((r, cols), out_dtype),
        compiler_params=_params("parallel"),
        name="dft_stage1",
    )(mat, x)


def _spectrum_kernel(g_ref, a_ref, o_ref):
    n2 = a_ref.shape[1]
    a = a_ref[...].reshape(2 * n2, a_ref.shape[2])
    o_ref[...] = jnp.dot(g_ref[...], a, preferred_element_type=F32).reshape(o_ref.shape)


def filter_spectrum(g_tab, a):
    _, n1, n2, d = a.shape
    return pl.pallas_call(
        _spectrum_kernel,
        grid=(n1,),
        in_specs=[pl.BlockSpec((None, 2 * n2, 2 * n2), lambda k: (k, 0, 0)),
                  pl.BlockSpec((2, None, n2, d), lambda k: (0, k, 0, 0))],
        out_specs=pl.BlockSpec((None, 2, n2, d), lambda k: (k, 0, 0, 0)),
        out_shape=jax.ShapeDtypeStruct((n1, 2, n2, d), F32),
        compiler_params=_params("parallel"),
        name="filter_spectrum",
    )(g_tab, a)


def _freq_mul_kernel(g_ref, gi_ref, a_ref, h_ref, o_ref):
    n2 = a_ref.shape[1]
    d = a_ref.shape[2]
    a = a_ref[...].reshape(2 * n2, d)
    x = jnp.dot(g_ref[...], a, preferred_element_type=F32)
    xr, xi = x[:n2], x[n2:]
    hr, hi = h_ref[0], h_ref[1]
    p = jnp.concatenate([xr * hr - xi * hi, xr * hi + xi * hr], axis=0).astype(BF16)
    o_ref[...] = jnp.dot(gi_ref[...], p, preferred_element_type=F32).astype(o_ref.dtype).reshape(o_ref.shape)


def freq_multiply(g_tab, gi_tab, a, h):
    _, n1, n2, d = a.shape
    return pl.pallas_call(
        _freq_mul_kernel,
        grid=(n1,),
        in_specs=[pl.BlockSpec((None, 2 * n2, 2 * n2), lambda k: (k, 0, 0)),
                  pl.BlockSpec((None, 2 * n2, 2 * n2), lambda k: (k, 0, 0)),
                  pl.BlockSpec((2, None, n2, d), lambda k: (0, k, 0, 0)),
                  pl.BlockSpec((None, 2, n2, d), lambda k: (k, 0, 0, 0))],
        out_specs=pl.BlockSpec((2, None, n2, d), lambda k: (0, k, 0, 0)),
        out_shape=jax.ShapeDtypeStruct((2, n1, n2, d), BF16),
        compiler_params=_params("parallel"),
        name="freq_multiply",
    )(g_tab, gi_tab, a, h)


def _conv_out_kernel(m_ref, t_ref, x0_ref, z_ref, skip_ref, o_ref):
    y = jnp.dot(m_ref[...], t_ref[...], preferred_element_type=F32)
    o_ref[...] = (x0_ref[...] * (y + skip_ref[...] * z_ref[...])).astype(o_ref.dtype)


def conv_output(mat, t, x0, z, skip_tiled):
    r, kdim = mat.shape
    cols = t.shape[1]
    tc = skip_tiled.shape[1]
    return pl.pallas_call(
        _conv_out_kernel,
        grid=(cols // tc,),
        in_specs=[pl.BlockSpec((r, kdim), lambda j: (0, 0)),
                  pl.BlockSpec((kdim, tc), lambda j: (0, j)),
                  pl.BlockSpec((r, tc), lambda j: (0, j)),
                  pl.BlockSpec((r, tc), lambda j: (0, j)),
                  pl.BlockSpec((1, tc), lambda j: (0, 0))],
        out_specs=pl.BlockSpec((r, tc), lambda j: (0, j)),
        out_shape=jax.ShapeDtypeStruct((r, cols), BF16),
        compiler_params=_params("parallel"),
        name="dft_inverse_out",
    )(mat, t, x0, z, skip_tiled)


def _dft_tables(seq):
    n = 2 * seq
    n2 = FFT_N2
    n1 = n // n2
    lh = seq // n2

    def cs(idx, mod):
        ang = (2.0 * math.pi / mod) * (idx % mod).astype(F32)
        return jnp.cos(ang), jnp.sin(ang)

    k1 = jnp.arange(n1)
    cr, sr = cs(k1[:, None] * jnp.arange(lh)[None, :], n1)
    f_sig = jnp.block([[cr, sr], [-sr, cr]])
    cf, sf = cs(k1[:, None] * k1[None, :], n1)
    f_filt = jnp.concatenate([cf, -sf], axis=0)
    idx = jnp.arange(n2)[None, None, :] * (k1[:, None, None] + n1 * jnp.arange(n2)[None, :, None])
    gr, gs = cs(idx, n)
    g_tab = jnp.concatenate([jnp.concatenate([gr, gs], axis=2),
                             jnp.concatenate([-gs, gr], axis=2)], axis=1)
    gi_tab = jnp.swapaxes(g_tab, 1, 2)
    er, es = cs(jnp.arange(lh)[:, None] * k1[None, :], n1)
    f_inv = jnp.block([[er, -es], [es, er]]) / n
    return (f_sig.astype(BF16), f_filt.astype(BF16), g_tab.astype(BF16), gi_tab.astype(BF16),
            f_inv.astype(BF16))


def hyena_long_conv(x0, z, filt, skip, tables, batch, seq):
    assert batch == 2, "the two sequences ride as real / imaginary parts of one complex signal"
    f_sig, f_filt, g_tab, gi_tab, f_inv = tables
    m, d = z.shape
    n2 = FFT_N2
    n1 = 2 * seq // n2
    lh = seq // n2
    cols = n2 * d
    a_f = left_matmul(f_filt, filt.reshape(n1, cols), BF16)
    h = filter_spectrum(g_tab, a_f.reshape(2, n1, n2, d))
    a = left_matmul(f_sig, z.reshape(batch * lh, cols), BF16)
    t = freq_multiply(g_tab, gi_tab, a.reshape(2, n1, n2, d), h)
    tc = min(2048, cols)
    skip_tiled = jnp.tile(skip.astype(F32).reshape(1, d), (1, tc // d))
    y = conv_output(f_inv, t.reshape(2 * n1, cols), x0.reshape(batch * lh, cols),
                    z.reshape(batch * lh, cols), skip_tiled)
    return y.reshape(m, d)


def kernel(x, c, ada_w, ada_b, norm_g, hg_w_in, hg_lower_bounds, hg_norm_g, hg_w_out,
           hy_w_in, hy_b_in, hy_conv_w, hy_conv_b, hy_filt_w_in, hy_filt_w_mid, hy_filt_b,
           hy_filt_freq, hy_filt_w_out, hy_skip, hy_w_out, mlp_w1, mlp_w2, final_g):
    batch, seq, d = x.shape
    depth = ada_w.shape[0]
    n_mixers = 2
    assert d % HEAD_DIM == 0 and seq % max(GLA_CHUNK, FFT_N2) == 0

    mod = adaln_all(c, ada_w, ada_b)
    mod = mod.reshape(2 * depth, batch, 3, 1, d)

    lbs = jax.nn.softmax(hg_lower_bounds.astype(F32), axis=1)
    lbs = jnp.cumsum(lbs, axis=1) - lbs[:, :1]

    tables = _dft_tables(seq)
    zero_bias = jnp.zeros((5 * d,), F32)

    xf = x.reshape(batch * seq, d)
    for i in range(depth):
        j = i // n_mixers
        shift, scale, gate = mod[2 * i, :, 0], mod[2 * i, :, 1], mod[2 * i, :, 2]
        if i % n_mixers == 0:
            proj = norm_mod_matmul(xf, norm_g[i, 0], scale, shift, hg_w_in[j].astype(BF16),
                                   zero_bias, seq)
            o_f = gla_direction(proj, lbs[0, j], hg_norm_g[j], None, batch, seq, rev=False)
            o = gla_direction(proj, lbs[1, j], hg_norm_g[j], o_f, batch, seq, rev=True)
            xf = proj_residual(o, hg_w_out[j].astype(BF16), xf, gate, seq)
        else:
            u = norm_mod_matmul(xf, norm_g[i, 0], scale, shift, hy_w_in[j].astype(BF16),
                                hy_b_in[j], seq)
            x0, z = short_conv_gate(u, hy_conv_w[j], hy_conv_b[j], seq)
            filt = hyena_filter_circular(seq, d, hy_filt_w_in[j], hy_filt_w_mid[j], hy_filt_b[j],
                                         hy_filt_freq[j], hy_filt_w_out[j])
            y = hyena_long_conv(x0, z, filt, hy_skip[j], tables, batch, seq)
            xf = proj_residual(y, hy_w_out[j].astype(BF16), xf, gate, seq)
        shift, scale, gate = mod[2 * i + 1, :, 0], mod[2 * i + 1, :, 1], mod[2 * i + 1, :, 2]
        xf = mlp_block(xf, norm_g[i, 1], scale, shift, gate, mlp_w1[i].astype(BF16),
                       mlp_w2[i].astype(BF16), final_g, seq, final_norm=(i == depth - 1))
    return xf.reshape(batch, seq, d)
```

```python
import functools
import math

import jax
import jax.numpy as jnp
from jax import lax
from jax.experimental import pallas as pl
from jax.experimental.pallas import tpu as pltpu

F32 = jnp.float32
BF16 = jnp.bfloat16

NORM_EPS = 1e-6
HEAD_DIM = 128
GLA_CHUNK = 64
GLA_SUB = 16
GLA_UNROLL_GATES = 8
GLA_UNROLL_SCORES = 4
GLA_UNROLL_OUT = 8
SUBLANES = 8
FFT_N2 = 128
HY_FILTER_PAD = 64
HY_MAX_DECAY = math.log(1e-2) / 0.3
HY_MIN_DECAY = math.log(1e-2) / 1.5
VMEM_LIMIT = 56 * 1024 * 1024


def _params(*sem):
    return pltpu.CompilerParams(dimension_semantics=sem, vmem_limit_bytes=VMEM_LIMIT)


def _tile(n, target, unit):
    t = min(n, target) // unit * unit
    while n % t:
        t -= unit
    return t


def _rms_mod(x, g, scale, shift):
    ms = jnp.mean(x * x, axis=-1, keepdims=True)
    y = x * lax.rsqrt(ms + NORM_EPS) * g
    return y * (1.0 + scale) + shift


def _adaln_kernel(s_ref, w_ref, b_ref, o_ref):
    s = s_ref[...]
    s = s * jax.nn.sigmoid(s)
    o_ref[...] = jnp.dot(s, w_ref[...], preferred_element_type=F32,
                         precision=lax.Precision.HIGHEST) + b_ref[...]


def adaln_all(c, ada_w, ada_b):
    depth, two, d, d3 = ada_w.shape
    n = depth * two
    b = c.shape[0]
    rows = -(-b // SUBLANES) * SUBLANES
    cp = jnp.zeros((rows, d), F32).at[:b].set(c)
    tn = _tile(d3, 1536, 128)
    out = pl.pallas_call(
        _adaln_kernel,
        grid=(n, d3 // tn),
        in_specs=[
            pl.BlockSpec((rows, d), lambda i, j: (0, 0)),
            pl.BlockSpec((None, d, tn), lambda i, j: (i, 0, j)),
            pl.BlockSpec((None, 1, tn), lambda i, j: (i, 0, j)),
        ],
        out_specs=pl.BlockSpec((None, rows, tn), lambda i, j: (i, 0, j)),
        out_shape=jax.ShapeDtypeStruct((n, rows, d3), F32),
        compiler_params=_params("parallel", "parallel"),
        name="adaln",
    )(cp, ada_w.reshape(n, d, d3), ada_b.reshape(n, 1, d3))
    return out[:, :b]


def _resident(shape):
    return pl.BlockSpec(shape, lambda i: (0,) * len(shape), pipeline_mode=pl.Buffered(1))


def _nmm_kernel(x_ref, g_ref, sc_ref, sh_ref, w_ref, b_ref, o_ref, hn_ref, *, tn):
    hn_ref[...] = _rms_mod(x_ref[...], g_ref[...], sc_ref[...], sh_ref[...]).astype(BF16)
    for j in range(o_ref.shape[1] // tn):
        cols = slice(j * tn, (j + 1) * tn)
        o_ref[:, cols] = jnp.dot(hn_ref[...], w_ref[:, cols],
                                 preferred_element_type=F32) + b_ref[:, cols]


def norm_mod_matmul(x, g, scale, shift, w, bias, seq):
    m, d = x.shape
    n = w.shape[1]
    tm = min(512, seq)
    tn = _tile(n, 1024, 128)
    per_seq = seq // tm
    return pl.pallas_call(
        functools.partial(_nmm_kernel, tn=tn),
        grid=(m // tm,),
        in_specs=[
            pl.BlockSpec((tm, d), lambda i: (i, 0)),
            _resident((1, d)),
            pl.BlockSpec((None, 1, d), lambda i: (i // per_seq, 0, 0)),
            pl.BlockSpec((None, 1, d), lambda i: (i // per_seq, 0, 0)),
            _resident((d, n)),
            _resident((1, n)),
        ],
        out_specs=pl.BlockSpec((tm, n), lambda i: (i, 0)),
        out_shape=jax.ShapeDtypeStruct((m, n), F32),
        scratch_shapes=[pltpu.VMEM((tm, d), BF16)],
        compiler_params=_params("parallel"),
        name="norm_mod_matmul",
    )(x, g.reshape(1, d), scale, shift, w, bias.reshape(1, n))


def _out_mlp_kernel(a_ref, wo_ref, x_ref, gm_ref, g_ref, sc_ref, sh_ref, gate_ref, w1_ref, w2_ref,
                    fg_ref, o_ref, hn_ref, acc_ref, *, final_norm, tf):
    o_ref[...] = x_ref[...] + gm_ref[...] * jnp.dot(a_ref[...], wo_ref[...],
                                                     preferred_element_type=F32)
    hn_ref[...] = _rms_mod(o_ref[...], g_ref[...], sc_ref[...], sh_ref[...]).astype(BF16)
    for k in range(w1_ref.shape[1] // tf):
        cols = slice(k * tf, (k + 1) * tf)
        h = jnp.dot(hn_ref[...], w1_ref[:, cols], preferred_element_type=F32)
        h = jnp.square(jnp.maximum(h, 0.0)).astype(BF16)
        part = jnp.dot(h, w2_ref[cols, :], preferred_element_type=F32)
        if k == 0:
            acc_ref[...] = part
        else:
            acc_ref[...] += part
    out = o_ref[...] + gate_ref[...] * acc_ref[...]
    if final_norm:
        ms = jnp.mean(out * out, axis=-1, keepdims=True)
        out = out * lax.rsqrt(ms + NORM_EPS) * fg_ref[...]
    o_ref[...] = out


def mixer_out_mlp(a, w_out, x, gate_mix, g, scale, shift, gate, w1, w2, final_g, seq, final_norm):
    m, d = x.shape
    dff = w1.shape[1]
    tm = min(512, seq)
    tf = _tile(dff, 1024, 128)
    per_seq = seq // tm
    vec = pl.BlockSpec((None, 1, d), lambda i: (i // per_seq, 0, 0))
    return pl.pallas_call(
        functools.partial(_out_mlp_kernel, final_norm=final_norm, tf=tf),
        grid=(m // tm,),
        in_specs=[
            pl.BlockSpec((tm, d), lambda i: (i, 0)),
            _resident((d, d)),
            pl.BlockSpec((tm, d), lambda i: (i, 0)),
            vec,
            _resident((1, d)),
            vec, vec, vec,
            _resident((d, dff)),
            _resident((dff, d)),
            _resident((1, d)),
        ],
        out_specs=pl.BlockSpec((tm, d), lambda i: (i, 0)),
        out_shape=jax.ShapeDtypeStruct((m, d), F32),
        scratch_shapes=[pltpu.VMEM((tm, d), BF16), pltpu.VMEM((tm, d), F32)],
        compiler_params=_params("parallel"),
        name="mixer_out_mlp",
    )(a, w_out, x, gate_mix, g.reshape(1, d), scale, shift, gate, w1, w2, final_g.reshape(1, d))


def _gla_kernel(*refs, rev, n_chunks):
    if rev:
        (q_ref, v_ref, z_ref, lb_ref, tri_ref, e_ref, of_ref, gt_ref, ng_ref, o_ref,
         st_s, q_s, b2_s, gam_s, qe_s, dec_s, upd_s, seen_s, att_s, p_s, dg_s) = refs
    else:
        (q_ref, v_ref, z_ref, lb_ref, tri_ref, e_ref, o_ref,
         st_s, q_s, b2_s, gam_s, qe_s, dec_s, upd_s, seen_s, att_s, p_s, dg_s) = refs
    c_len, sub, half = GLA_CHUNK, GLA_SUB, SUBLANES
    n_sub = c_len // sub
    nt = (((1,), (1,)), ((), ()))
    tn = (((0,), (0,)), ((), ()))

    @pl.when(pl.program_id(2) == 0)
    def _():
        st_s[...] = jnp.zeros_like(st_s)

    lb = lb_ref[...]
    one_minus_lb = 1.0 - lb
    log2_oml = jnp.log2(one_minus_lb)
    tri2 = tri_ref[...]

    def gates(ci, carry):
        rows = pl.ds(pl.multiple_of(ci * c_len, c_len), c_len)
        qz = q_ref[rows, :]
        z = z_ref[rows, :]
        q = qz * (1.0 / (1.0 + jnp.exp(-qz)))
        g2 = jnp.log2(lb + one_minus_lb * (1.0 / (1.0 + jnp.exp(-z))))
        lk2 = log2_oml - jnp.log2(1.0 + jnp.exp(z))
        hi = g2.astype(BF16)
        lo = (g2 - hi.astype(F32)).astype(BF16)
        b2 = jnp.dot(tri2, jnp.concatenate([hi, lo], axis=0), preferred_element_type=F32)
        q_s[rows, :] = q
        b2_s[rows, :] = b2
        gam_s[rows, :] = lk2 - b2
        return carry

    lax.fori_loop(0, n_chunks, gates, 0, unroll=GLA_UNROLL_GATES)

    def scores(ci, carry):
        off = pl.multiple_of(ci * c_len, c_len)
        rows = pl.ds(off, c_len)
        q = q_s[rows, :]
        b2 = b2_s[rows, :]
        gam = gam_s[rows, :]
        b2_end = b2[0:1] if rev else b2[c_len - 1:c_len]
        qe_s[rows, :] = (q * jnp.exp2(b2)).astype(BF16)
        upd_s[ci] = lax.dot_general(v_ref[rows, :].astype(BF16),
                                    jnp.exp2(b2_end + gam).astype(BF16), tn,
                                    preferred_element_type=F32)
        dec_s[ci] = jnp.exp2(b2_end)
        atts = []
        for i in range(n_sub):
            r0 = sub * i
            q_i, b_i = q[r0:r0 + sub], b2[r0:r0 + sub]
            if rev:
                others = (r0 + sub, c_len)
                edge = b2[r0 + sub:r0 + sub + 1] if i < n_sub - 1 else None
            else:
                others = (0, r0)
                edge = b2[r0 - 1:r0] if i > 0 else None
            if edge is None:
                atts.append(jnp.zeros((sub, HEAD_DIM), F32))
            else:
                qt = (q_i * jnp.exp2(b_i - edge)).astype(BF16)
                kt = jnp.exp2(edge + gam[others[0]:others[1]]).astype(BF16)
                parts = []
                if others[0]:
                    parts.append(jnp.zeros((others[0], HEAD_DIM), BF16))
                parts.append(kt)
                parts.append(jnp.zeros((HEAD_DIM - others[1], HEAD_DIM), BF16))
                atts.append(lax.dot_general(qt, jnp.concatenate(parts, axis=0), nt,
                                            preferred_element_type=F32))
            for sl in range(sub):
                g_row = gam_s[pl.ds(off + r0 + sl, 1), :]
                halves = []
                for hh in range(sub // half):
                    h0, h1 = half * hh, half * hh + half - 1
                    needed = (h0 <= sl) if rev else (h1 >= sl)
                    full = (h1 <= sl) if rev else (h0 >= sl)
                    if not needed:
                        halves.append(jnp.zeros((half, HEAD_DIM), F32))
                        continue
                    arg = b_i[h0:h1 + 1] + g_row
                    if not full:
                        arg = jnp.minimum(arg, 0.0)
                    halves.append(q_i[h0:h1 + 1] * jnp.exp2(arg))
                p_s[i, pl.ds(pl.multiple_of(ci * sub, sub), sub),
                    sl * HEAD_DIM:(sl + 1) * HEAD_DIM] = jnp.concatenate(halves, axis=0).astype(BF16)
        att_s[rows, :] = jnp.concatenate(atts, axis=0)
        return carry

    lax.fori_loop(0, n_chunks, scores, 0, unroll=GLA_UNROLL_SCORES)

    for i in range(n_sub):
        dg_s[i] = jnp.dot(p_s[i], e_ref[i], preferred_element_type=F32)

    row = lax.broadcasted_iota(jnp.int32, (c_len, HEAD_DIM), 0)
    col = lax.broadcasted_iota(jnp.int32, (c_len, HEAD_DIM), 1)
    keep = (col >= row) if rev else (col <= row)
    v_pad = jnp.zeros((HEAD_DIM - c_len, HEAD_DIM), BF16)

    def state_step(cj, st):
        ci = (n_chunks - 1 - cj) if rev else cj
        seen_s[ci] = st.astype(BF16)
        return st * dec_s[ci] + upd_s[ci]

    st_s[...] = lax.fori_loop(0, n_chunks, state_step, st_s[...], unroll=True)

    def outputs(ci, carry):
        off = pl.multiple_of(ci * c_len, c_len)
        rows = pl.ds(off, c_len)
        blk = pl.ds(pl.multiple_of(ci * sub, sub), sub)
        att = att_s[rows, :] + jnp.concatenate([dg_s[i, blk, :] for i in range(n_sub)], axis=0)
        att = jnp.where(keep, att, 0.0).astype(BF16)
        v = v_ref[rows, :].astype(BF16)
        o = (jnp.dot(att, jnp.concatenate([v, v_pad], axis=0), preferred_element_type=F32)
             + lax.dot_general(qe_s[rows, :], seen_s[ci], nt, preferred_element_type=F32))
        if rev:
            tot = of_ref[rows, :] + o
            ms = jnp.mean(tot * tot, axis=-1, keepdims=True)
            gt = gt_ref[rows, :]
            y = tot * lax.rsqrt(ms + NORM_EPS) * ng_ref[...] * (gt * jax.nn.sigmoid(gt))
            o_ref[rows, :] = y.astype(o_ref.dtype)
        else:
            o_ref[rows, :] = o
        return carry

    lax.fori_loop(0, n_chunks, outputs, 0, unroll=GLA_UNROLL_OUT)


def _gla_tables(rev):
    idx = jnp.arange(GLA_CHUNK)
    tri = (idx[:, None] <= idx[None, :]) if rev else (idx[:, None] >= idx[None, :])
    tri2 = jnp.concatenate([tri, tri], axis=1).astype(BF16)
    s_of_row = jnp.arange(GLA_SUB * HEAD_DIM) // HEAD_DIM
    place = (jnp.arange(HEAD_DIM)[None, None, :]
             == GLA_SUB * jnp.arange(GLA_CHUNK // GLA_SUB)[:, None, None] + s_of_row[None, :, None])
    return tri2, place.astype(BF16)


def gla_direction(proj, lb, norm_g, o_fwd, batch, seq, rev):
    m, d5 = proj.shape
    d = d5 // 5
    heads = d // HEAD_DIM
    t_blk = min(1024, seq)
    n_t = seq // t_blk
    n_chunks = t_blk // GLA_CHUNK
    n_sub = GLA_CHUNK // GLA_SUB
    tri2, place = _gla_tables(rev)

    def rowblk(b, h, t):
        return b * n_t + ((n_t - 1 - t) if rev else t)

    def col(group):
        return pl.BlockSpec((t_blk, HEAD_DIM), lambda b, h, t: (rowblk(b, h, t), group * heads + h))

    in_specs = [col(0), col(1), col(3 if rev else 2),
                pl.BlockSpec((1, HEAD_DIM), lambda b, h, t: (0, h)),
                pl.BlockSpec(tri2.shape, lambda b, h, t: (0, 0)),
                pl.BlockSpec(place.shape, lambda b, h, t: (0, 0, 0))]
    args = [proj, proj, proj, lb.reshape(1, d), tri2, place]
    if rev:
        in_specs += [pl.BlockSpec((t_blk, HEAD_DIM), lambda b, h, t: (rowblk(b, h, t), h)),
                     col(4),
                     pl.BlockSpec((1, HEAD_DIM), lambda b, h, t: (0, 0))]
        args += [o_fwd, proj, norm_g.reshape(1, HEAD_DIM)]
    return pl.pallas_call(
        functools.partial(_gla_kernel, rev=rev, n_chunks=n_chunks),
        grid=(batch, heads, n_t),
        in_specs=in_specs,
        out_specs=pl.BlockSpec((t_blk, HEAD_DIM), lambda b, h, t: (rowblk(b, h, t), h)),
        out_shape=jax.ShapeDtypeStruct((m, d), BF16 if rev else F32),
        scratch_shapes=[pltpu.VMEM((HEAD_DIM, HEAD_DIM), F32),
                        pltpu.VMEM((t_blk, HEAD_DIM), F32),
                        pltpu.VMEM((t_blk, HEAD_DIM), F32),
                        pltpu.VMEM((t_blk, HEAD_DIM), F32),
                        pltpu.VMEM((t_blk, HEAD_DIM), BF16),
                        pltpu.VMEM((n_chunks, 1, HEAD_DIM), F32),
                        pltpu.VMEM((n_chunks, HEAD_DIM, HEAD_DIM), F32),
                        pltpu.VMEM((n_chunks, HEAD_DIM, HEAD_DIM), BF16),
                        pltpu.VMEM((t_blk, HEAD_DIM), F32),
                        pltpu.VMEM((n_sub, t_blk // n_sub, GLA_SUB * HEAD_DIM), BF16),
                        pltpu.VMEM((n_sub, t_blk // n_sub, HEAD_DIM), F32)],
        compiler_params=_params("parallel", "parallel", "arbitrary"),
        name="gla_rev" if rev else "gla_fwd",
    )(*args)


def _short_conv_kernel(cur_ref, prev_ref, next_ref, w_ref, b_ref, x0_ref, z_ref, *, tiles_per_seq):
    i = pl.program_id(0)
    tm, d3 = cur_ref.shape
    d = d3 // 3
    cur = cur_ref[...]
    first = (i % tiles_per_seq) == 0
    last = (i % tiles_per_seq) == tiles_per_seq - 1
    prev_row = jnp.where(first, 0.0, prev_ref[SUBLANES - 1:SUBLANES, :])
    next_row = jnp.where(last, 0.0, next_ref[0:1, :])
    row = lax.broadcasted_iota(jnp.int32, (tm, d3), 0)
    before = jnp.where(row == 0, prev_row, pltpu.roll(cur, 1, 0))
    after = jnp.where(row == tm - 1, next_row, pltpu.roll(cur, tm - 1, 0))
    u = before * w_ref[0:1, :] + cur * w_ref[1:2, :] + after * w_ref[2:3, :] + b_ref[...]
    x0_ref[...] = u[:, :d]
    z_ref[...] = u[:, d:2 * d] * u[:, 2 * d:]


def short_conv_gate(u, conv_w, conv_b, seq):
    m, d3 = u.shape
    d = d3 // 3
    tm = min(256, seq)
    tiles_per_seq = seq // tm
    halo = tm // SUBLANES
    n_halo = m // SUBLANES
    return pl.pallas_call(
        functools.partial(_short_conv_kernel, tiles_per_seq=tiles_per_seq),
        grid=(m // tm,),
        in_specs=[
            pl.BlockSpec((tm, d3), lambda i: (i, 0)),
            pl.BlockSpec((SUBLANES, d3), lambda i: (jnp.maximum(i * halo - 1, 0), 0)),
            pl.BlockSpec((SUBLANES, d3), lambda i: (jnp.minimum((i + 1) * halo, n_halo - 1), 0)),
            pl.BlockSpec((3, d3), lambda i: (0, 0)),
            pl.BlockSpec((1, d3), lambda i: (0, 0)),
        ],
        out_specs=[pl.BlockSpec((tm, d), lambda i: (i, 0)),
                   pl.BlockSpec((tm, d), lambda i: (i, 0))],
        out_shape=[jax.ShapeDtypeStruct((m, d), F32), jax.ShapeDtypeStruct((m, d), F32)],
        compiler_params=_params("parallel"),
        name="short_conv",
    )(u, u, u, conv_w, conv_b.reshape(1, d3))


def _filter_kernel(emb_ref, t_ref, keep_ref, win_ref, wmid_ref, b_ref, freq_ref, wout_ref,
                   delta_ref, o_ref):
    def split(a):
        hi = a.astype(BF16)
        return hi, (a - hi.astype(F32)).astype(BF16)

    def dot3(a, w):
        a_hi, a_lo = split(a)
        w_hi, w_lo = split(w)
        mm = functools.partial(jnp.dot, preferred_element_type=F32)
        return mm(a_hi, w_hi) + mm(a_hi, w_lo) + mm(a_lo, w_hi)

    freq = freq_ref[...]
    h = jnp.sin(freq * (dot3(emb_ref[...], win_ref[...]) + b_ref[0:1, :]))
    for mth in range(wmid_ref.shape[0]):
        h = jnp.sin(freq * (dot3(h, wmid_ref[mth]) + b_ref[mth + 1:mth + 2, :]))
    hf = dot3(h, wout_ref[...])
    o_ref[...] = hf * jnp.exp(-t_ref[...] * delta_ref[...]) * keep_ref[...]


def hyena_filter_circular(seq, d, w_in, w_mid, b, freq, w_out):
    emb_dim, width = w_in.shape
    bands = (emb_dim - 1) // 2
    mrow = jnp.arange(2 * seq)
    pos = jnp.where(mrow < seq, mrow, jnp.where(mrow == seq, 0, 2 * seq - mrow)).astype(F32)[:, None]
    t_c = pos / (seq - 1.0)
    band = jnp.linspace(1e-4, bands - 1.0, bands, dtype=F32)
    ang = (2.0 * math.pi / seq) * pos * band
    emb_c = jnp.concatenate([t_c, jnp.cos(ang), -jnp.sin(ang)], axis=-1)
    emb_c = jnp.pad(emb_c, ((0, 0), (0, HY_FILTER_PAD - emb_dim)))
    keep = (mrow != seq).astype(F32)[:, None]
    w_in_p = jnp.pad(w_in.astype(F32), ((0, HY_FILTER_PAD - emb_dim), (0, 0)))
    deltas = jnp.abs(jnp.linspace(HY_MIN_DECAY, HY_MAX_DECAY, d, dtype=F32)).reshape(1, d)
    tr = min(512, seq)
    half = seq // tr
    n_mid = w_mid.shape[0]
    return pl.pallas_call(
        _filter_kernel,
        grid=(2 * seq // tr,),
        in_specs=[
            pl.BlockSpec((tr, HY_FILTER_PAD), lambda i: (i, 0)),
            pl.BlockSpec((tr, 1), lambda i: (i, 0)),
            pl.BlockSpec((tr, 1), lambda i: (i, 0)),
            pl.BlockSpec((HY_FILTER_PAD, width), lambda i: (0, 0)),
            pl.BlockSpec((n_mid, width, width), lambda i: (0, 0, 0)),
            pl.BlockSpec((n_mid + 1, width), lambda i: (0, 0)),
            pl.BlockSpec((1, width), lambda i: (0, 0)),
            pl.BlockSpec((width, d), lambda i: (0, i // half)),
            pl.BlockSpec((1, d), lambda i: (0, 0)),
        ],
        out_specs=pl.BlockSpec((tr, d), lambda i: (i, 0)),
        out_shape=jax.ShapeDtypeStruct((2 * seq, d), F32),
        compiler_params=_params("parallel"),
        name="hyena_filter",
    )(emb_c, t_c, keep, w_in_p, w_mid.astype(F32), b.astype(F32),
      freq.astype(F32).reshape(1, width), w_out.astype(F32), deltas)


def _left_matmul_kernel(m_ref, x_ref, o_ref):
    o_ref[...] = jnp.dot(m_ref[...], x_ref[...].astype(BF16),
                         preferred_element_type=F32).astype(o_ref.dtype)


def left_matmul(mat, x, out_dtype):
    r, kdim = mat.shape
    cols = x.shape[1]
    tc = min(2048, cols)
    return pl.pallas_call(
        _left_matmul_kernel,
        grid=(cols // tc,),
        in_specs=[pl.BlockSpec((r, kdim), lambda j: (0, 0)),
                  pl.BlockSpec((kdim, tc), lambda j: (0, j))],
        out_specs=pl.BlockSpec((r, tc), lambda j: (0, j)),
        out_shape=jax.ShapeDtypeStruct((r, cols), out_dtype),
        compiler_params=_params("parallel"),
        name="dft_stage1",
    )(mat, x)


def _spectrum_kernel(g_ref, a_ref, o_ref):
    n2 = a_ref.shape[1]
    a = a_ref[...].reshape(2 * n2, a_ref.shape[2])
    o_ref[...] = jnp.dot(g_ref[...], a, preferred_element_type=F32).reshape(o_ref.shape)


def filter_spectrum(g_tab, a):
    _, n1, n2, d = a.shape
    return pl.pallas_call(
        _spectrum_kernel,
        grid=(n1,),
        in_specs=[pl.BlockSpec((None, 2 * n2, 2 * n2), lambda k: (k, 0, 0)),
                  pl.BlockSpec((2, None, n2, d), lambda k: (0, k, 0, 0))],
        out_specs=pl.BlockSpec((None, 2, n2, d), lambda k: (k, 0, 0, 0)),
        out_shape=jax.ShapeDtypeStruct((n1, 2, n2, d), F32),
        compiler_params=_params("parallel"),
        name="filter_spectrum",
    )(g_tab, a)


def _freq_mul_kernel(g_ref, gi_ref, a_ref, h_ref, o_ref):
    n2 = a_ref.shape[1]
    d = a_ref.shape[2]
    a = a_ref[...].reshape(2 * n2, d)
    x = jnp.dot(g_ref[...], a, preferred_element_type=F32)
    xr, xi = x[:n2], x[n2:]
    hr, hi = h_ref[0], h_ref[1]
    p = jnp.concatenate([xr * hr - xi * hi, xr * hi + xi * hr], axis=0).astype(BF16)
    o_ref[...] = jnp.dot(gi_ref[...], p, preferred_element_type=F32).astype(o_ref.dtype).reshape(o_ref.shape)


def freq_multiply(g_tab, gi_tab, a, h):
    _, n1, n2, d = a.shape
    return pl.pallas_call(
        _freq_mul_kernel,
        grid=(n1,),
        in_specs=[pl.BlockSpec((None, 2 * n2, 2 * n2), lambda k: (k, 0, 0)),
                  pl.BlockSpec((None, 2 * n2, 2 * n2), lambda k: (k, 0, 0)),
                  pl.BlockSpec((2, None, n2, d), lambda k: (0, k, 0, 0)),
                  pl.BlockSpec((None, 2, n2, d), lambda k: (k, 0, 0, 0))],
        out_specs=pl.BlockSpec((2, None, n2, d), lambda k: (0, k, 0, 0)),
        out_shape=jax.ShapeDtypeStruct((2, n1, n2, d), BF16),
        compiler_params=_params("parallel"),
        name="freq_multiply",
    )(g_tab, gi_tab, a, h)


def _conv_out_kernel(m_ref, t_ref, x0_ref, z_ref, skip_ref, o_ref):
    y = jnp.dot(m_ref[...], t_ref[...], preferred_element_type=F32)
    o_ref[...] = (x0_ref[...] * (y + skip_ref[...] * z_ref[...])).astype(o_ref.dtype)


def conv_output(mat, t, x0, z, skip_tiled):
    r, kdim = mat.shape
    cols = t.shape[1]
    tc = skip_tiled.shape[1]
    return pl.pallas_call(
        _conv_out_kernel,
        grid=(cols // tc,),
        in_specs=[pl.BlockSpec((r, kdim), lambda j: (0, 0)),
                  pl.BlockSpec((kdim, tc), lambda j: (0, j)),
                  pl.BlockSpec((r, tc), lambda j: (0, j)),
                  pl.BlockSpec((r, tc), lambda j: (0, j)),
                  pl.BlockSpec((1, tc), lambda j: (0, 0))],
        out_specs=pl.BlockSpec((r, tc), lambda j: (0, j)),
        out_shape=jax.ShapeDtypeStruct((r, cols), BF16),
        compiler_params=_params("parallel"),
        name="dft_inverse_out",
    )(mat, t, x0, z, skip_tiled)


def _dft_tables(seq):
    n = 2 * seq
    n2 = FFT_N2
    n1 = n // n2
    lh = seq // n2

    def cs(idx, mod):
        ang = (2.0 * math.pi / mod) * (idx % mod).astype(F32)
        return jnp.cos(ang), jnp.sin(ang)

    k1 = jnp.arange(n1)
    cr, sr = cs(k1[:, None] * jnp.arange(lh)[None, :], n1)
    f_sig = jnp.block([[cr, sr], [-sr, cr]])
    cf, sf = cs(k1[:, None] * k1[None, :], n1)
    f_filt = jnp.concatenate([cf, -sf], axis=0)
    idx = jnp.arange(n2)[None, None, :] * (k1[:, None, None] + n1 * jnp.arange(n2)[None, :, None])
    gr, gs = cs(idx, n)
    g_tab = jnp.concatenate([jnp.concatenate([gr, gs], axis=2),
                             jnp.concatenate([-gs, gr], axis=2)], axis=1)
    gi_tab = jnp.swapaxes(g_tab, 1, 2)
    er, es = cs(jnp.arange(lh)[:, None] * k1[None, :], n1)
    f_inv = jnp.block([[er, -es], [es, er]]) / n
    return (f_sig.astype(BF16), f_filt.astype(BF16), g_tab.astype(BF16), gi_tab.astype(BF16),
            f_inv.astype(BF16))


def hyena_long_conv(x0, z, filt, skip, tables, batch, seq):
    assert batch == 2, "the two sequences ride as real / imaginary parts of one complex signal"
    f_sig, f_filt, g_tab, gi_tab, f_inv = tables
    m, d = z.shape
    n2 = FFT_N2
    n1 = 2 * seq // n2
    lh = seq // n2
    cols = n2 * d
    a_f = left_matmul(f_filt, filt.reshape(n1, cols), BF16)
    h = filter_spectrum(g_tab, a_f.reshape(2, n1, n2, d))
    a = left_matmul(f_sig, z.reshape(batch * lh, cols), BF16)
    t = freq_multiply(g_tab, gi_tab, a.reshape(2, n1, n2, d), h)
    tc = min(2048, cols)
    skip_tiled = jnp.tile(skip.astype(F32).reshape(1, d), (1, tc // d))
    y = conv_output(f_inv, t.reshape(2 * n1, cols), x0.reshape(batch * lh, cols),
                    z.reshape(batch * lh, cols), skip_tiled)
    return y.reshape(m, d)


def kernel(x, c, ada_w, ada_b, norm_g, hg_w_in, hg_lower_bounds, hg_norm_g, hg_w_out,
           hy_w_in, hy_b_in, hy_conv_w, hy_conv_b, hy_filt_w_in, hy_filt_w_mid, hy_filt_b,
           hy_filt_freq, hy_filt_w_out, hy_skip, hy_w_out, mlp_w1, mlp_w2, final_g):
    batch, seq, d = x.shape
    depth = ada_w.shape[0]
    n_mixers = 2
    assert d % HEAD_DIM == 0 and seq % max(GLA_CHUNK, FFT_N2) == 0

    mod = adaln_all(c, ada_w, ada_b)
    mod = mod.reshape(2 * depth, batch, 3, 1, d)

    lbs = jax.nn.softmax(hg_lower_bounds.astype(F32), axis=1)
    lbs = jnp.cumsum(lbs, axis=1) - lbs[:, :1]

    tables = _dft_tables(seq)
    zero_bias = jnp.zeros((5 * d,), F32)

    xf = x.reshape(batch * seq, d)
    for i in range(depth):
        j = i // n_mixers
        shift, scale, gate = mod[2 * i, :, 0], mod[2 * i, :, 1], mod[2 * i, :, 2]
        if i % n_mixers == 0:
            proj = norm_mod_matmul(xf, norm_g[i, 0], scale, shift, hg_w_in[j].astype(BF16),
                                   zero_bias, seq)
            o_f = gla_direction(proj, lbs[0, j], hg_norm_g[j], None, batch, seq, rev=False)
            mix = gla_direction(proj, lbs[1, j], hg_norm_g[j], o_f, batch, seq, rev=True)
            w_out = hg_w_out[j]
        else:
            u = norm_mod_matmul(xf, norm_g[i, 0], scale, shift, hy_w_in[j].astype(BF16),
                                hy_b_in[j], seq)
            x0, z = short_conv_gate(u, hy_conv_w[j], hy_conv_b[j], seq)
            filt = hyena_filter_circular(seq, d, hy_filt_w_in[j], hy_filt_w_mid[j], hy_filt_b[j],
                                         hy_filt_freq[j], hy_filt_w_out[j])
            mix = hyena_long_conv(x0, z, filt, hy_skip[j], tables, batch, seq)
            w_out = hy_w_out[j]
        gate_mix = gate
        shift, scale, gate = mod[2 * i + 1, :, 0], mod[2 * i + 1, :, 1], mod[2 * i + 1, :, 2]
        xf = mixer_out_mlp(mix, w_out.astype(BF16), xf, gate_mix, norm_g[i, 1], scale, shift, gate,
                           mlp_w1[i].astype(BF16), mlp_w2[i].astype(BF16), final_g, seq,
                           final_norm=(i == depth - 1))
    return xf.reshape(batch, seq, d)
```

```python
import functools
import math

import jax
import jax.numpy as jnp
from jax import lax
from jax.experimental import pallas as pl
from jax.experimental.pallas import tpu as pltpu

F32 = jnp.float32
BF16 = jnp.bfloat16

NORM_EPS = 1e-6
HEAD_DIM = 128
GLA_CHUNK = 64
GLA_SUB = 8
GLA_UNROLL_SUMS = 16
GLA_UNROLL_SCORES = 16
GLA_UNROLL_OUT = 16
SUBLANES = 8
BF16_ROWS = 16
HGRN_PROJ_SLAB = 256
FFT_N2 = 128
HY_FILTER_PAD = 64
HY_MAX_DECAY = math.log(1e-2) / 0.3
HY_MIN_DECAY = math.log(1e-2) / 1.5
VMEM_LIMIT = 56 * 1024 * 1024


def _params(*sem):
    return pltpu.CompilerParams(dimension_semantics=sem, vmem_limit_bytes=VMEM_LIMIT)


def _tile(n, target, unit):
    t = min(n, target) // unit * unit
    while n % t:
        t -= unit
    return t


def _rms_mod(x, g, scale, shift):
    ms = jnp.mean(x * x, axis=-1, keepdims=True)
    y = x * lax.rsqrt(ms + NORM_EPS) * g
    return y * (1.0 + scale) + shift


def _adaln_kernel(s_ref, w_ref, b_ref, o_ref):
    s = s_ref[...]
    s = s * jax.nn.sigmoid(s)
    o_ref[...] = jnp.dot(s, w_ref[...], preferred_element_type=F32,
                         precision=lax.Precision.HIGHEST) + b_ref[...]


def adaln_all(c, ada_w, ada_b):
    depth, two, d, d3 = ada_w.shape
    n = depth * two
    b = c.shape[0]
    rows = -(-b // SUBLANES) * SUBLANES
    cp = jnp.zeros((rows, d), F32).at[:b].set(c)
    tn = _tile(d3, 1536, 128)
    out = pl.pallas_call(
        _adaln_kernel,
        grid=(n, d3 // tn),
        in_specs=[
            pl.BlockSpec((rows, d), lambda i, j: (0, 0)),
            pl.BlockSpec((None, d, tn), lambda i, j: (i, 0, j)),
            pl.BlockSpec((None, 1, tn), lambda i, j: (i, 0, j)),
        ],
        out_specs=pl.BlockSpec((None, rows, tn), lambda i, j: (i, 0, j)),
        out_shape=jax.ShapeDtypeStruct((n, rows, d3), F32),
        compiler_params=_params("parallel", "parallel"),
        name="adaln",
    )(cp, ada_w.reshape(n, d, d3), ada_b.reshape(n, 1, d3))
    return out[:, :b]


def _resident(shape):
    return pl.BlockSpec(shape, lambda i: (0,) * len(shape), pipeline_mode=pl.Buffered(1))


def _nmm_kernel(x_ref, g_ref, sc_ref, sh_ref, w_ref, b_ref, o_ref, hn_ref, *, tn):
    hn_ref[...] = _rms_mod(x_ref[...], g_ref[...], sc_ref[...], sh_ref[...]).astype(BF16)
    for j in range(o_ref.shape[1] // tn):
        cols = slice(j * tn, (j + 1) * tn)
        o_ref[:, cols] = jnp.dot(hn_ref[...], w_ref[:, cols],
                                 preferred_element_type=F32) + b_ref[:, cols]


def norm_mod_matmul(x, g, scale, shift, w, bias, seq):
    m, d = x.shape
    n = w.shape[1]
    tm = min(512, seq)
    tn = _tile(n, 1024, 128)
    per_seq = seq // tm
    return pl.pallas_call(
        functools.partial(_nmm_kernel, tn=tn),
        grid=(m // tm,),
        in_specs=[
            pl.BlockSpec((tm, d), lambda i: (i, 0)),
            _resident((1, d)),
            pl.BlockSpec((None, 1, d), lambda i: (i // per_seq, 0, 0)),
            pl.BlockSpec((None, 1, d), lambda i: (i // per_seq, 0, 0)),
            _resident((d, n)),
            _resident((1, n)),
        ],
        out_specs=pl.BlockSpec((tm, n), lambda i: (i, 0)),
        out_shape=jax.ShapeDtypeStruct((m, n), F32),
        scratch_shapes=[pltpu.VMEM((tm, d), BF16)],
        compiler_params=_params("parallel"),
        name="norm_mod_matmul",
    )(x, g.reshape(1, d), scale, shift, w, bias.reshape(1, n))


def _out_mlp_kernel(a_ref, wo_ref, x_ref, gm_ref, g_ref, sc_ref, sh_ref, gate_ref, w1_ref, w2_ref,
                    fg_ref, o_ref, hn_ref, acc_ref, *, final_norm, tf):
    o_ref[...] = x_ref[...] + gm_ref[...] * jnp.dot(a_ref[...], wo_ref[...],
                                                     preferred_element_type=F32)
    hn_ref[...] = _rms_mod(o_ref[...], g_ref[...], sc_ref[...], sh_ref[...]).astype(BF16)
    for k in range(w1_ref.shape[1] // tf):
        cols = slice(k * tf, (k + 1) * tf)
        h = jnp.dot(hn_ref[...], w1_ref[:, cols], preferred_element_type=F32)
        h = jnp.square(jnp.maximum(h, 0.0)).astype(BF16)
        part = jnp.dot(h, w2_ref[cols, :], preferred_element_type=F32)
        if k == 0:
            acc_ref[...] = part
        else:
            acc_ref[...] += part
    out = o_ref[...] + gate_ref[...] * acc_ref[...]
    if final_norm:
        ms = jnp.mean(out * out, axis=-1, keepdims=True)
        out = out * lax.rsqrt(ms + NORM_EPS) * fg_ref[...]
    o_ref[...] = out


def mixer_out_mlp(a, w_out, x, gate_mix, g, scale, shift, gate, w1, w2, final_g, seq, final_norm):
    m, d = x.shape
    dff = w1.shape[1]
    tm = min(512, seq)
    tf = _tile(dff, 1024, 128)
    per_seq = seq // tm
    vec = pl.BlockSpec((None, 1, d), lambda i: (i // per_seq, 0, 0))
    return pl.pallas_call(
        functools.partial(_out_mlp_kernel, final_norm=final_norm, tf=tf),
        grid=(m // tm,),
        in_specs=[
            pl.BlockSpec((tm, d), lambda i: (i, 0)),
            _resident((d, d)),
            pl.BlockSpec((tm, d), lambda i: (i, 0)),
            vec,
            _resident((1, d)),
            vec, vec, vec,
            _resident((d, dff)),
            _resident((dff, d)),
            _resident((1, d)),
        ],
        out_specs=pl.BlockSpec((tm, d), lambda i: (i, 0)),
        out_shape=jax.ShapeDtypeStruct((m, d), F32),
        scratch_shapes=[pltpu.VMEM((tm, d), BF16), pltpu.VMEM((tm, d), F32)],
        compiler_params=_params("parallel"),
        name="mixer_out_mlp",
    )(a, w_out, x, gate_mix, g.reshape(1, d), scale, shift, gate, w1, w2, final_g.reshape(1, d))


def _hgrn_proj_kernel(x_ref, g_ref, sc_ref, sh_ref, w_ref, lb_ref, q_ref, v_ref, gt_ref, dec_ref,
                      hn_ref):
    d = q_ref.shape[1]
    hn_ref[...] = _rms_mod(x_ref[...], g_ref[...], sc_ref[...], sh_ref[...]).astype(BF16)

    slab = min(d, HGRN_PROJ_SLAB)
    for c0 in range(0, d, slab):
        cols = slice(c0, c0 + slab)

        def proj(group):
            return jnp.dot(hn_ref[...], w_ref[:, group * d + c0:group * d + c0 + slab],
                           preferred_element_type=F32)

        qz = proj(0)
        q_ref[:, cols] = (qz * (1.0 / (1.0 + jnp.exp(-qz)))).astype(BF16)
        v_ref[:, cols] = proj(1).astype(BF16)
        for dirn in range(2):
            z = proj(2 + dirn)
            lb = lb_ref[dirn:dirn + 1, cols]
            one_minus_lb = 1.0 - lb
            dec_ref[:, 2 * dirn * d + c0:2 * dirn * d + c0 + slab] = jnp.log2(
                lb + one_minus_lb * (1.0 / (1.0 + jnp.exp(-z))))
            dec_ref[:, (2 * dirn + 1) * d + c0:(2 * dirn + 1) * d + c0 + slab] = (
                jnp.log2(one_minus_lb) - jnp.log2(1.0 + jnp.exp(z)))
        gz = proj(4)
        gt_ref[:, cols] = (gz * (1.0 / (1.0 + jnp.exp(-gz)))).astype(BF16)


def hgrn_projection(x, g, scale, shift, w, lbs, seq):
    m, d = x.shape
    tm = min(512, seq)
    per_seq = seq // tm
    row = lambda width: pl.BlockSpec((tm, width), lambda i: (i, 0))
    return pl.pallas_call(
        _hgrn_proj_kernel,
        grid=(m // tm,),
        in_specs=[
            row(d),
            _resident((1, d)),
            pl.BlockSpec((None, 1, d), lambda i: (i // per_seq, 0, 0)),
            pl.BlockSpec((None, 1, d), lambda i: (i // per_seq, 0, 0)),
            _resident((d, 5 * d)),
            _resident((2, d)),
        ],
        out_specs=[row(d), row(d), row(d), row(4 * d)],
        out_shape=[jax.ShapeDtypeStruct((m, d), BF16), jax.ShapeDtypeStruct((m, d), BF16),
                   jax.ShapeDtypeStruct((m, d), BF16), jax.ShapeDtypeStruct((m, 4 * d), F32)],
        scratch_shapes=[pltpu.VMEM((tm, d), BF16)],
        compiler_params=_params("parallel"),
        name="hgrn_projection",
    )(x, g.reshape(1, d), scale, shift, w, lbs)


def _gla_kernel(*refs, rev, n_chunks):
    if rev:
        (q_ref, v_ref, g2_ref, lk2_ref, tri_ref, e_ref, of_ref, gt_ref, ng_ref, o_ref,
         st_s, b2_s, gam_s, qe_s, dec_s, upd_s, seen_s, att_s, p_s, dg_s) = refs
    else:
        (q_ref, v_ref, g2_ref, lk2_ref, tri_ref, e_ref, o_ref,
         st_s, b2_s, gam_s, qe_s, dec_s, upd_s, seen_s, att_s, p_s, dg_s) = refs
    c_len, sub, pack = GLA_CHUNK, GLA_SUB, BF16_ROWS
    n_sub = c_len // sub
    nt = (((1,), (1,)), ((), ()))
    tn = (((0,), (0,)), ((), ()))

    @pl.when(pl.program_id(2) == 0)
    def _():
        st_s[...] = jnp.zeros_like(st_s)

    tri2 = tri_ref[...]

    def sums(ci, carry):
        rows = pl.ds(pl.multiple_of(ci * c_len, c_len), c_len)
        g2 = g2_ref[rows, :]
        hi = g2.astype(BF16)
        lo = (g2 - hi.astype(F32)).astype(BF16)
        b2 = jnp.dot(tri2, jnp.concatenate([hi, lo], axis=0), preferred_element_type=F32)
        b2_s[rows, :] = b2
        gam_s[rows, :] = lk2_ref[rows, :] - b2
        return carry

    lax.fori_loop(0, n_chunks, sums, 0, unroll=GLA_UNROLL_SUMS)

    def padded_rows(x, lo, hi, total):
        lo_t, hi_t = lo // pack * pack, -(-hi // pack) * pack
        parts = ([jnp.zeros((lo - lo_t, HEAD_DIM), F32)] if lo > lo_t else []) + [x]
        if hi_t > hi:
            parts.append(jnp.zeros((hi_t - hi, HEAD_DIM), F32))
        tiles = [jnp.concatenate(parts, axis=0).astype(BF16)]
        if lo_t:
            tiles.insert(0, jnp.zeros((lo_t, HEAD_DIM), BF16))
        if total > hi_t:
            tiles.append(jnp.zeros((total - hi_t, HEAD_DIM), BF16))
        return jnp.concatenate(tiles, axis=0)

    def scores(ci, carry):
        off = pl.multiple_of(ci * c_len, c_len)
        rows = pl.ds(off, c_len)
        q = q_ref[rows, :].astype(F32)
        b2 = b2_s[rows, :]
        gam = gam_s[rows, :]
        b2_end = b2[0:1] if rev else b2[c_len - 1:c_len]
        qe_s[rows, :] = (q * jnp.exp2(b2)).astype(BF16)
        upd_s[ci] = lax.dot_general(v_ref[rows, :], jnp.exp2(b2_end + gam).astype(BF16), tn,
                                    preferred_element_type=F32)
        dec_s[ci] = jnp.exp2(b2_end)
        atts, pieces = [], []
        for i in range(n_sub):
            r0 = sub * i
            q_i, b_i = q[r0:r0 + sub], b2[r0:r0 + sub]
            if rev:
                others = (r0 + sub, c_len)
                edge = b2[r0 + sub:r0 + sub + 1] if i < n_sub - 1 else None
            else:
                others = (0, r0)
                edge = b2[r0 - 1:r0] if i > 0 else None
            if edge is None:
                atts.append(jnp.zeros((sub, HEAD_DIM), F32))
            else:
                qt = padded_rows(q_i * jnp.exp2(b_i - edge), 0, sub, pack)
                kt = padded_rows(jnp.exp2(edge + gam[others[0]:others[1]]), others[0], others[1],
                                 HEAD_DIM)
                atts.append(lax.dot_general(qt, kt, nt, preferred_element_type=F32)[:sub])
            row_pieces = []
            for sl in range(sub):
                g_row = gam_s[pl.ds(off + r0 + sl, 1), :]
                row_pieces.append(q_i * jnp.exp2(jnp.minimum(b_i + g_row, 0.0)))
            pieces.append(row_pieces)
        for i in range(0, n_sub, pack // sub):
            for sl in range(sub):
                tile = jnp.concatenate([pieces[i + u][sl] for u in range(pack // sub)], axis=0)
                p_s[pl.ds(off + sub * i, pack), sl * HEAD_DIM:(sl + 1) * HEAD_DIM] = tile.astype(BF16)
        att_s[rows, :] = jnp.concatenate(atts, axis=0)
        return carry

    lax.fori_loop(0, n_chunks, scores, 0, unroll=GLA_UNROLL_SCORES)

    dg_s[...] = jnp.dot(p_s[...], e_ref[...], preferred_element_type=F32)

    row = lax.broadcasted_iota(jnp.int32, (c_len, HEAD_DIM), 0)
    col = lax.broadcasted_iota(jnp.int32, (c_len, HEAD_DIM), 1)
    keep = (col >= row) if rev else (col <= row)
    v_pad = jnp.zeros((HEAD_DIM - c_len, HEAD_DIM), BF16)

    def state_step(cj, st):
        ci = (n_chunks - 1 - cj) if rev else cj
        seen_s[ci] = st.astype(BF16)
        return st * dec_s[ci] + upd_s[ci]

    st_s[...] = lax.fori_loop(0, n_chunks, state_step, st_s[...], unroll=True)

    def outputs(ci, carry):
        off = pl.multiple_of(ci * c_len, c_len)
        rows = pl.ds(off, c_len)
        dg = dg_s[rows, :]
        diag = [dg[0:sub]] + [pltpu.roll(dg[sub * i:sub * (i + 1)], sub * i, 1)
                              for i in range(1, n_sub)]
        att = att_s[rows, :] + jnp.concatenate(diag, axis=0)
        att = jnp.where(keep, att, 0.0).astype(BF16)
        o = (jnp.dot(att, jnp.concatenate([v_ref[rows, :], v_pad], axis=0),
                     preferred_element_type=F32)
             + lax.dot_general(qe_s[rows, :], seen_s[ci], nt, preferred_element_type=F32))
        if rev:
            tot = of_ref[rows, :] + o
            ms = jnp.mean(tot * tot, axis=-1, keepdims=True)
            y = tot * lax.rsqrt(ms + NORM_EPS) * ng_ref[...] * gt_ref[rows, :].astype(F32)
            o_ref[rows, :] = y.astype(o_ref.dtype)
        else:
            o_ref[rows, :] = o
        return carry

    lax.fori_loop(0, n_chunks, outputs, 0, unroll=GLA_UNROLL_OUT)


def _gla_tables(rev):
    idx = jnp.arange(GLA_CHUNK)
    tri = (idx[:, None] <= idx[None, :]) if rev else (idx[:, None] >= idx[None, :])
    tri2 = jnp.concatenate([tri, tri], axis=1).astype(BF16)
    s_of_row = jnp.arange(GLA_SUB * HEAD_DIM) // HEAD_DIM
    place = jnp.arange(HEAD_DIM)[None, :] == s_of_row[:, None]
    return tri2, place.astype(BF16)


def gla_direction(q, v, gate, dec, norm_g, o_fwd, batch, seq, rev):
    m, d = q.shape
    heads = d // HEAD_DIM
    t_blk = min(1024, seq)
    n_t = seq // t_blk
    n_chunks = t_blk // GLA_CHUNK
    tri2, place = _gla_tables(rev)

    def rowblk(b, h, t):
        return b * n_t + ((n_t - 1 - t) if rev else t)

    def col(group):
        return pl.BlockSpec((t_blk, HEAD_DIM), lambda b, h, t: (rowblk(b, h, t), group * heads + h))

    in_specs = [col(0), col(0), col(2 if rev else 0), col(3 if rev else 1),
                pl.BlockSpec(tri2.shape, lambda b, h, t: (0, 0)),
                pl.BlockSpec(place.shape, lambda b, h, t: (0, 0))]
    args = [q, v, dec, dec, tri2, place]
    if rev:
        in_specs += [col(0), col(0), pl.BlockSpec((1, HEAD_DIM), lambda b, h, t: (0, 0))]
        args += [o_fwd, gate, norm_g.reshape(1, HEAD_DIM)]
    return pl.pallas_call(
        functools.partial(_gla_kernel, rev=rev, n_chunks=n_chunks),
        grid=(batch, heads, n_t),
        in_specs=in_specs,
        out_specs=col(0),
        out_shape=jax.ShapeDtypeStruct((m, d), BF16 if rev else F32),
        scratch_shapes=[pltpu.VMEM((HEAD_DIM, HEAD_DIM), F32),
                        pltpu.VMEM((t_blk, HEAD_DIM), F32),
                        pltpu.VMEM((t_blk, HEAD_DIM), F32),
                        pltpu.VMEM((t_blk, HEAD_DIM), BF16),
                        pltpu.VMEM((n_chunks, 1, HEAD_DIM), F32),
                        pltpu.VMEM((n_chunks, HEAD_DIM, HEAD_DIM), F32),
                        pltpu.VMEM((n_chunks, HEAD_DIM, HEAD_DIM), BF16),
                        pltpu.VMEM((t_blk, HEAD_DIM), F32),
                        pltpu.VMEM((t_blk, GLA_SUB * HEAD_DIM), BF16),
                        pltpu.VMEM((t_blk, HEAD_DIM), F32)],
        compiler_params=_params("parallel", "parallel", "arbitrary"),
        name="gla_rev" if rev else "gla_fwd",
    )(*args)


def _short_conv_kernel(cur_ref, prev_ref, next_ref, w_ref, b_ref, x0_ref, z_ref, *, tiles_per_seq):
    i = pl.program_id(0)
    tm, d3 = cur_ref.shape
    d = d3 // 3
    cur = cur_ref[...]
    first = (i % tiles_per_seq) == 0
    last = (i % tiles_per_seq) == tiles_per_seq - 1
    prev_row = jnp.where(first, 0.0, prev_ref[SUBLANES - 1:SUBLANES, :])
    next_row = jnp.where(last, 0.0, next_ref[0:1, :])
    row = lax.broadcasted_iota(jnp.int32, (tm, d3), 0)
    before = jnp.where(row == 0, prev_row, pltpu.roll(cur, 1, 0))
    after = jnp.where(row == tm - 1, next_row, pltpu.roll(cur, tm - 1, 0))
    u = before * w_ref[0:1, :] + cur * w_ref[1:2, :] + after * w_ref[2:3, :] + b_ref[...]
    x0_ref[...] = u[:, :d]
    z_ref[...] = u[:, d:2 * d] * u[:, 2 * d:]


def short_conv_gate(u, conv_w, conv_b, seq):
    m, d3 = u.shape
    d = d3 // 3
    tm = min(256, seq)
    tiles_per_seq = seq // tm
    halo = tm // SUBLANES
    n_halo = m // SUBLANES
    return pl.pallas_call(
        functools.partial(_short_conv_kernel, tiles_per_seq=tiles_per_seq),
        grid=(m // tm,),
        in_specs=[
            pl.BlockSpec((tm, d3), lambda i: (i, 0)),
            pl.BlockSpec((SUBLANES, d3), lambda i: (jnp.maximum(i * halo - 1, 0), 0)),
            pl.BlockSpec((SUBLANES, d3), lambda i: (jnp.minimum((i + 1) * halo, n_halo - 1), 0)),
            pl.BlockSpec((3, d3), lambda i: (0, 0)),
            pl.BlockSpec((1, d3), lambda i: (0, 0)),
        ],
        out_specs=[pl.BlockSpec((tm, d), lambda i: (i, 0)),
                   pl.BlockSpec((tm, d), lambda i: (i, 0))],
        out_shape=[jax.ShapeDtypeStruct((m, d), F32), jax.ShapeDtypeStruct((m, d), F32)],
        compiler_params=_params("parallel"),
        name="short_conv",
    )(u, u, u, conv_w, conv_b.reshape(1, d3))


def _filter_kernel(emb_ref, t_ref, keep_ref, win_ref, wmid_ref, b_ref, freq_ref, wout_ref,
                   delta_ref, o_ref):
    def split(a):
        hi = a.astype(BF16)
        return hi, (a - hi.astype(F32)).astype(BF16)

    def dot3(a, w):
        a_hi, a_lo = split(a)
        w_hi, w_lo = split(w)
        mm = functools.partial(jnp.dot, preferred_element_type=F32)
        return mm(a_hi, w_hi) + mm(a_hi, w_lo) + mm(a_lo, w_hi)

    freq = freq_ref[...]
    h = jnp.sin(freq * (dot3(emb_ref[...], win_ref[...]) + b_ref[0:1, :]))
    for mth in range(wmid_ref.shape[0]):
        h = jnp.sin(freq * (dot3(h, wmid_ref[mth]) + b_ref[mth + 1:mth + 2, :]))
    hf = dot3(h, wout_ref[...])
    o_ref[...] = hf * jnp.exp(-t_ref[...] * delta_ref[...]) * keep_ref[...]


def hyena_filter_circular(seq, d, w_in, w_mid, b, freq, w_out):
    emb_dim, width = w_in.shape
    bands = (emb_dim - 1) // 2
    mrow = jnp.arange(2 * seq)
    pos = jnp.where(mrow < seq, mrow, jnp.where(mrow == seq, 0, 2 * seq - mrow)).astype(F32)[:, None]
    t_c = pos / (seq - 1.0)
    band = jnp.linspace(1e-4, bands - 1.0, bands, dtype=F32)
    ang = (2.0 * math.pi / seq) * pos * band
    emb_c = jnp.concatenate([t_c, jnp.cos(ang), -jnp.sin(ang)], axis=-1)
    emb_c = jnp.pad(emb_c, ((0, 0), (0, HY_FILTER_PAD - emb_dim)))
    keep = (mrow != seq).astype(F32)[:, None]
    w_in_p = jnp.pad(w_in.astype(F32), ((0, HY_FILTER_PAD - emb_dim), (0, 0)))
    deltas = jnp.abs(jnp.linspace(HY_MIN_DECAY, HY_MAX_DECAY, d, dtype=F32)).reshape(1, d)
    tr = min(512, seq)
    half = seq // tr
    n_mid = w_mid.shape[0]
    return pl.pallas_call(
        _filter_kernel,
        grid=(2 * seq // tr,),
        in_specs=[
            pl.BlockSpec((tr, HY_FILTER_PAD), lambda i: (i, 0)),
            pl.BlockSpec((tr, 1), lambda i: (i, 0)),
            pl.BlockSpec((tr, 1), lambda i: (i, 0)),
            pl.BlockSpec((HY_FILTER_PAD, width), lambda i: (0, 0)),
            pl.BlockSpec((n_mid, width, width), lambda i: (0, 0, 0)),
            pl.BlockSpec((n_mid + 1, width), lambda i: (0, 0)),
            pl.BlockSpec((1, width), lambda i: (0, 0)),
            pl.BlockSpec((width, d), lambda i: (0, i // half)),
            pl.BlockSpec((1, d), lambda i: (0, 0)),
        ],
        out_specs=pl.BlockSpec((tr, d), lambda i: (i, 0)),
        out_shape=jax.ShapeDtypeStruct((2 * seq, d), F32),
        compiler_params=_params("parallel"),
        name="hyena_filter",
    )(emb_c, t_c, keep, w_in_p, w_mid.astype(F32), b.astype(F32),
      freq.astype(F32).reshape(1, width), w_out.astype(F32), deltas)


def _left_matmul_kernel(m_ref, x_ref, o_ref):
    o_ref[...] = jnp.dot(m_ref[...], x_ref[...].astype(BF16),
                         preferred_element_type=F32).astype(o_ref.dtype)


def left_matmul(mat, x, out_dtype):
    r, kdim = mat.shape
    cols = x.shape[1]
    tc = min(2048, cols)
    return pl.pallas_call(
        _left_matmul_kernel,
        grid=(cols // tc,),
        in_specs=[pl.BlockSpec((r, kdim), lambda j: (0, 0)),
                  pl.BlockSpec((kdim, tc), lambda j: (0, j))],
        out_specs=pl.BlockSpec((r, tc), lambda j: (0, j)),
        out_shape=jax.ShapeDtypeStruct((r, cols), out_dtype),
        compiler_params=_params("parallel"),
        name="dft_stage1",
    )(mat, x)


def _spectrum_kernel(g_ref, a_ref, o_ref):
    n2 = a_ref.shape[1]
    a = a_ref[...].reshape(2 * n2, a_ref.shape[2])
    o_ref[...] = jnp.dot(g_ref[...], a, preferred_element_type=F32).reshape(o_ref.shape)


def filter_spectrum(g_tab, a):
    _, n1, n2, d = a.shape
    return pl.pallas_call(
        _spectrum_kernel,
        grid=(n1,),
        in_specs=[pl.BlockSpec((None, 2 * n2, 2 * n2), lambda k: (k, 0, 0)),
                  pl.BlockSpec((2, None, n2, d), lambda k: (0, k, 0, 0))],
        out_specs=pl.BlockSpec((None, 2, n2, d), lambda k: (k, 0, 0, 0)),
        out_shape=jax.ShapeDtypeStruct((n1, 2, n2, d), F32),
        compiler_params=_params("parallel"),
        name="filter_spectrum",
    )(g_tab, a)


def _freq_mul_kernel(g_ref, gi_ref, a_ref, h_ref, o_ref):
    n2 = a_ref.shape[1]
    d = a_ref.shape[2]
    a = a_ref[...].reshape(2 * n2, d)
    x = jnp.dot(g_ref[...], a, preferred_element_type=F32)
    xr, xi = x[:n2], x[n2:]
    hr, hi = h_ref[0], h_ref[1]
    p = jnp.concatenate([xr * hr - xi * hi, xr * hi + xi * hr], axis=0).astype(BF16)
    o_ref[...] = jnp.dot(gi_ref[...], p, preferred_element_type=F32).astype(o_ref.dtype).reshape(o_ref.shape)


def freq_multiply(g_tab, gi_tab, a, h):
    _, n1, n2, d = a.shape
    return pl.pallas_call(
        _freq_mul_kernel,
        grid=(n1,),
        in_specs=[pl.BlockSpec((None, 2 * n2, 2 * n2), lambda k: (k, 0, 0)),
                  pl.BlockSpec((None, 2 * n2, 2 * n2), lambda k: (k, 0, 0)),
                  pl.BlockSpec((2, None, n2, d), lambda k: (0, k, 0, 0)),
                  pl.BlockSpec((None, 2, n2, d), lambda k: (k, 0, 0, 0))],
        out_specs=pl.BlockSpec((2, None, n2, d), lambda k: (0, k, 0, 0)),
        out_shape=jax.ShapeDtypeStruct((2, n1, n2, d), BF16),
        compiler_params=_params("parallel"),
        name="freq_multiply",
    )(g_tab, gi_tab, a, h)


def _conv_out_kernel(m_ref, t_ref, x0_ref, z_ref, skip_ref, o_ref):
    y = jnp.dot(m_ref[...], t_ref[...], preferred_element_type=F32)
    o_ref[...] = (x0_ref[...] * (y + skip_ref[...] * z_ref[...])).astype(o_ref.dtype)


def conv_output(mat, t, x0, z, skip_tiled):
    r, kdim = mat.shape
    cols = t.shape[1]
    tc = skip_tiled.shape[1]
    return pl.pallas_call(
        _conv_out_kernel,
        grid=(cols // tc,),
        in_specs=[pl.BlockSpec((r, kdim), lambda j: (0, 0)),
                  pl.BlockSpec((kdim, tc), lambda j: (0, j)),
                  pl.BlockSpec((r, tc), lambda j: (0, j)),
                  pl.BlockSpec((r, tc), lambda j: (0, j)),
                  pl.BlockSpec((1, tc), lambda j: (0, 0))],
        out_specs=pl.BlockSpec((r, tc), lambda j: (0, j)),
        out_shape=jax.ShapeDtypeStruct((r, cols), BF16),
        compiler_params=_params("parallel"),
        name="dft_inverse_out",
    )(mat, t, x0, z, skip_tiled)


def _dft_tables(seq):
    n = 2 * seq
    n2 = FFT_N2
    n1 = n // n2
    lh = seq // n2

    def cs(idx, mod):
        ang = (2.0 * math.pi / mod) * (idx % mod).astype(F32)
        return jnp.cos(ang), jnp.sin(ang)

    k1 = jnp.arange(n1)
    cr, sr = cs(k1[:, None] * jnp.arange(lh)[None, :], n1)
    f_sig = jnp.block([[cr, sr], [-sr, cr]])
    cf, sf = cs(k1[:, None] * k1[None, :], n1)
    f_filt = jnp.concatenate([cf, -sf], axis=0)
    idx = jnp.arange(n2)[None, None, :] * (k1[:, None, None] + n1 * jnp.arange(n2)[None, :, None])
    gr, gs = cs(idx, n)
    g_tab = jnp.concatenate([jnp.concatenate([gr, gs], axis=2),
                             jnp.concatenate([-gs, gr], axis=2)], axis=1)
    gi_tab = jnp.swapaxes(g_tab, 1, 2)
    er, es = cs(jnp.arange(lh)[:, None] * k1[None, :], n1)
    f_inv = jnp.block([[er, -es], [es, er]]) / n
    return (f_sig.astype(BF16), f_filt.astype(BF16), g_tab.astype(BF16), gi_tab.astype(BF16),
            f_inv.astype(BF16))


def hyena_long_conv(x0, z, filt, skip, tables, batch, seq):
    assert batch == 2, "the two sequences ride as real / imaginary parts of one complex signal"
    f_sig, f_filt, g_tab, gi_tab, f_inv = tables
    m, d = z.shape
    n2 = FFT_N2
    n1 = 2 * seq // n2
    lh = seq // n2
    cols = n2 * d
    a_f = left_matmul(f_filt, filt.reshape(n1, cols), BF16)
    h = filter_spectrum(g_tab, a_f.reshape(2, n1, n2, d))
    a = left_matmul(f_sig, z.reshape(batch * lh, cols), BF16)
    t = freq_multiply(g_tab, gi_tab, a.reshape(2, n1, n2, d), h)
    tc = min(2048, cols)
    skip_tiled = jnp.tile(skip.astype(F32).reshape(1, d), (1, tc // d))
    y = conv_output(f_inv, t.reshape(2 * n1, cols), x0.reshape(batch * lh, cols),
                    z.reshape(batch * lh, cols), skip_tiled)
    return y.reshape(m, d)


def kernel(x, c, ada_w, ada_b, norm_g, hg_w_in, hg_lower_bounds, hg_norm_g, hg_w_out,
           hy_w_in, hy_b_in, hy_conv_w, hy_conv_b, hy_filt_w_in, hy_filt_w_mid, hy_filt_b,
           hy_filt_freq, hy_filt_w_out, hy_skip, hy_w_out, mlp_w1, mlp_w2, final_g):
    batch, seq, d = x.shape
    depth = ada_w.shape[0]
    n_mixers = 2
    assert d % HEAD_DIM == 0 and seq % max(GLA_CHUNK, FFT_N2) == 0

    mod = adaln_all(c, ada_w, ada_b)
    mod = mod.reshape(2 * depth, batch, 3, 1, d)

    lbs = jax.nn.softmax(hg_lower_bounds.astype(F32), axis=1)
    lbs = jnp.cumsum(lbs, axis=1) - lbs[:, :1]

    tables = _dft_tables(seq)

    xf = x.reshape(batch * seq, d)
    for i in range(depth):
        j = i // n_mixers
        shift, scale, gate = mod[2 * i, :, 0], mod[2 * i, :, 1], mod[2 * i, :, 2]
        if i % n_mixers == 0:
            q, v, gt, dec = hgrn_projection(xf, norm_g[i, 0], scale, shift, hg_w_in[j].astype(BF16),
                                            lbs[:, j], seq)
            o_f = gla_direction(q, v, gt, dec, hg_norm_g[j], None, batch, seq, rev=False)
            mix = gla_direction(q, v, gt, dec, hg_norm_g[j], o_f, batch, seq, rev=True)
            w_out = hg_w_out[j]
        else:
            u = norm_mod_matmul(xf, norm_g[i, 0], scale, shift, hy_w_in[j].astype(BF16),
                                hy_b_in[j], seq)
            x0, z = short_conv_gate(u, hy_conv_w[j], hy_conv_b[j], seq)
            filt = hyena_filter_circular(seq, d, hy_filt_w_in[j], hy_filt_w_mid[j], hy_filt_b[j],
                                         hy_filt_freq[j], hy_filt_w_out[j])
            mix = hyena_long_conv(x0, z, filt, hy_skip[j], tables, batch, seq)
            w_out = hy_w_out[j]
        gate_mix = gate
        shift, scale, gate = mod[2 * i + 1, :, 0], mod[2 * i + 1, :, 1], mod[2 * i + 1, :, 2]
        xf = mixer_out_mlp(mix, w_out.astype(BF16), xf, gate_mix, norm_g[i, 1], scale, shift, gate,
                           mlp_w1[i].astype(BF16), mlp_w2[i].astype(BF16), final_g, seq,
                           final_norm=(i == depth - 1))
    return xf.reshape(batch, seq, d)
```

```python
import functools
import math

import jax
import jax.numpy as jnp
from jax import lax
from jax.experimental import pallas as pl
from jax.experimental.pallas import tpu as pltpu

F32 = jnp.float32
BF16 = jnp.bfloat16

NORM_EPS = 1e-6
HEAD_DIM = 128
GLA_CHUNK = 64
GLA_SUB = 8
GLA_UNROLL_SUMS = 16
GLA_UNROLL_SCORES = 16
GLA_UNROLL_OUT = 16
SUBLANES = 8
BF16_ROWS = 16
HGRN_PROJ_SLAB = 256
FFT_N2 = 128
HY_FILTER_PAD = 64
HY_MAX_DECAY = math.log(1e-2) / 0.3
HY_MIN_DECAY = math.log(1e-2) / 1.5
VMEM_LIMIT = 56 * 1024 * 1024


def _params(*sem):
    return pltpu.CompilerParams(dimension_semantics=sem, vmem_limit_bytes=VMEM_LIMIT)


def _tile(n, target, unit):
    t = min(n, target) // unit * unit
    while n % t:
        t -= unit
    return t


def _rms_mod(x, g, scale, shift):
    ms = jnp.mean(x * x, axis=-1, keepdims=True)
    y = x * lax.rsqrt(ms + NORM_EPS) * g
    return y * (1.0 + scale) + shift


def _adaln_kernel(s_ref, w_ref, b_ref, o_ref):
    s = s_ref[...]
    s = s * jax.nn.sigmoid(s)
    o_ref[...] = jnp.dot(s, w_ref[...], preferred_element_type=F32,
                         precision=lax.Precision.HIGHEST) + b_ref[...]


def adaln_all(c, ada_w, ada_b):
    depth, two, d, d3 = ada_w.shape
    n = depth * two
    b = c.shape[0]
    rows = -(-b // SUBLANES) * SUBLANES
    cp = jnp.zeros((rows, d), F32).at[:b].set(c)
    tn = _tile(d3, 1536, 128)
    out = pl.pallas_call(
        _adaln_kernel,
        grid=(n, d3 // tn),
        in_specs=[
            pl.BlockSpec((rows, d), lambda i, j: (0, 0)),
            pl.BlockSpec((None, d, tn), lambda i, j: (i, 0, j)),
            pl.BlockSpec((None, 1, tn), lambda i, j: (i, 0, j)),
        ],
        out_specs=pl.BlockSpec((None, rows, tn), lambda i, j: (i, 0, j)),
        out_shape=jax.ShapeDtypeStruct((n, rows, d3), F32),
        compiler_params=_params("parallel", "parallel"),
        name="adaln",
    )(cp, ada_w.reshape(n, d, d3), ada_b.reshape(n, 1, d3))
    return out[:, :b]


def _resident(shape):
    return pl.BlockSpec(shape, lambda i: (0,) * len(shape), pipeline_mode=pl.Buffered(1))


def _nmm_kernel(x_ref, g_ref, sc_ref, sh_ref, w_ref, b_ref, o_ref, hn_ref, *, tn):
    hn_ref[...] = _rms_mod(x_ref[...], g_ref[...], sc_ref[...], sh_ref[...]).astype(BF16)
    for j in range(o_ref.shape[1] // tn):
        cols = slice(j * tn, (j + 1) * tn)
        o_ref[:, cols] = jnp.dot(hn_ref[...], w_ref[:, cols],
                                 preferred_element_type=F32) + b_ref[:, cols]


def norm_mod_matmul(x, g, scale, shift, w, bias, seq):
    m, d = x.shape
    n = w.shape[1]
    tm = min(512, seq)
    tn = _tile(n, 1024, 128)
    per_seq = seq // tm
    return pl.pallas_call(
        functools.partial(_nmm_kernel, tn=tn),
        grid=(m // tm,),
        in_specs=[
            pl.BlockSpec((tm, d), lambda i: (i, 0)),
            _resident((1, d)),
            pl.BlockSpec((None, 1, d), lambda i: (i // per_seq, 0, 0)),
            pl.BlockSpec((None, 1, d), lambda i: (i // per_seq, 0, 0)),
            _resident((d, n)),
            _resident((1, n)),
        ],
        out_specs=pl.BlockSpec((tm, n), lambda i: (i, 0)),
        out_shape=jax.ShapeDtypeStruct((m, n), F32),
        scratch_shapes=[pltpu.VMEM((tm, d), BF16)],
        compiler_params=_params("parallel"),
        name="norm_mod_matmul",
    )(x, g.reshape(1, d), scale, shift, w, bias.reshape(1, n))


def _out_mlp_kernel(a_ref, wo_ref, x_ref, gm_ref, g_ref, sc_ref, sh_ref, gate_ref, w1_ref, w2_ref,
                    fg_ref, o_ref, hn_ref, acc_ref, *, final_norm, tf):
    o_ref[...] = x_ref[...] + gm_ref[...] * jnp.dot(a_ref[...], wo_ref[...],
                                                     preferred_element_type=F32)
    hn_ref[...] = _rms_mod(o_ref[...], g_ref[...], sc_ref[...], sh_ref[...]).astype(BF16)
    for k in range(w1_ref.shape[1] // tf):
        cols = slice(k * tf, (k + 1) * tf)
        h = jnp.dot(hn_ref[...], w1_ref[:, cols], preferred_element_type=F32)
        h = jnp.square(jnp.maximum(h, 0.0)).astype(BF16)
        part = jnp.dot(h, w2_ref[cols, :], preferred_element_type=F32)
        if k == 0:
            acc_ref[...] = part
        else:
            acc_ref[...] += part
    out = o_ref[...] + gate_ref[...] * acc_ref[...]
    if final_norm:
        ms = jnp.mean(out * out, axis=-1, keepdims=True)
        out = out * lax.rsqrt(ms + NORM_EPS) * fg_ref[...]
    o_ref[...] = out


def mixer_out_mlp(a, w_out, x, gate_mix, g, scale, shift, gate, w1, w2, final_g, seq, final_norm):
    m, d = x.shape
    dff = w1.shape[1]
    tm = min(512, seq)
    tf = _tile(dff, 1024, 128)
    per_seq = seq // tm
    vec = pl.BlockSpec((None, 1, d), lambda i: (i // per_seq, 0, 0))
    return pl.pallas_call(
        functools.partial(_out_mlp_kernel, final_norm=final_norm, tf=tf),
        grid=(m // tm,),
        in_specs=[
            pl.BlockSpec((tm, d), lambda i: (i, 0)),
            _resident((d, d)),
            pl.BlockSpec((tm, d), lambda i: (i, 0)),
            vec,
            _resident((1, d)),
            vec, vec, vec,
            _resident((d, dff)),
            _resident((dff, d)),
            _resident((1, d)),
        ],
        out_specs=pl.BlockSpec((tm, d), lambda i: (i, 0)),
        out_shape=jax.ShapeDtypeStruct((m, d), F32),
        scratch_shapes=[pltpu.VMEM((tm, d), BF16), pltpu.VMEM((tm, d), F32)],
        compiler_params=_params("parallel"),
        name="mixer_out_mlp",
    )(a, w_out, x, gate_mix, g.reshape(1, d), scale, shift, gate, w1, w2, final_g.reshape(1, d))


def _hgrn_proj_kernel(x_ref, g_ref, sc_ref, sh_ref, w_ref, lb_ref, q_ref, v_ref, gt_ref, dec_ref,
                      hn_ref):
    d = q_ref.shape[1]
    hn_ref[...] = _rms_mod(x_ref[...], g_ref[...], sc_ref[...], sh_ref[...]).astype(BF16)

    slab = min(d, HGRN_PROJ_SLAB)
    for c0 in range(0, d, slab):
        cols = slice(c0, c0 + slab)

        def proj(group):
            return jnp.dot(hn_ref[...], w_ref[:, group * d + c0:group * d + c0 + slab],
                           preferred_element_type=F32)

        qz = proj(0)
        q_ref[:, cols] = (qz * (1.0 / (1.0 + jnp.exp(-qz)))).astype(BF16)
        v_ref[:, cols] = proj(1).astype(BF16)
        for dirn in range(2):
            z = proj(2 + dirn)
            lb = lb_ref[dirn:dirn + 1, cols]
            one_minus_lb = 1.0 - lb
            dec_ref[:, 2 * dirn * d + c0:2 * dirn * d + c0 + slab] = jnp.log2(
                lb + one_minus_lb * (1.0 / (1.0 + jnp.exp(-z))))
            dec_ref[:, (2 * dirn + 1) * d + c0:(2 * dirn + 1) * d + c0 + slab] = (
                jnp.log2(one_minus_lb) - jnp.log2(1.0 + jnp.exp(z)))
        gz = proj(4)
        gt_ref[:, cols] = (gz * (1.0 / (1.0 + jnp.exp(-gz)))).astype(BF16)


def hgrn_projection(x, g, scale, shift, w, lbs, seq):
    m, d = x.shape
    tm = min(512, seq)
    per_seq = seq // tm
    row = lambda width: pl.BlockSpec((tm, width), lambda i: (i, 0))
    return pl.pallas_call(
        _hgrn_proj_kernel,
        grid=(m // tm,),
        in_specs=[
            row(d),
            _resident((1, d)),
            pl.BlockSpec((None, 1, d), lambda i: (i // per_seq, 0, 0)),
            pl.BlockSpec((None, 1, d), lambda i: (i // per_seq, 0, 0)),
            _resident((d, 5 * d)),
            _resident((2, d)),
        ],
        out_specs=[row(d), row(d), row(d), row(4 * d)],
        out_shape=[jax.ShapeDtypeStruct((m, d), BF16), jax.ShapeDtypeStruct((m, d), BF16),
                   jax.ShapeDtypeStruct((m, d), BF16), jax.ShapeDtypeStruct((m, 4 * d), F32)],
        scratch_shapes=[pltpu.VMEM((tm, d), BF16)],
        compiler_params=_params("parallel"),
        name="hgrn_projection",
    )(x, g.reshape(1, d), scale, shift, w, lbs)


def _gla_kernel(*refs, rev, n_chunks):
    if rev:
        (q_ref, v_ref, g2_ref, lk2_ref, tri_ref, e_ref, of_ref, gt_ref, ng_ref, o_ref,
         st_s, b2_s, gam_s, qe_s, dec_s, upd_s, seen_s, att_s, p_s, dg_s) = refs
    else:
        (q_ref, v_ref, g2_ref, lk2_ref, tri_ref, e_ref, o_ref,
         st_s, b2_s, gam_s, qe_s, dec_s, upd_s, seen_s, att_s, p_s, dg_s) = refs
    c_len, sub, pack = GLA_CHUNK, GLA_SUB, BF16_ROWS
    n_sub = c_len // sub
    nt = (((1,), (1,)), ((), ()))
    tn = (((0,), (0,)), ((), ()))

    @pl.when(pl.program_id(2) == 0)
    def _():
        st_s[...] = jnp.zeros_like(st_s)

    tri2 = tri_ref[...]

    def sums(ci, carry):
        rows = pl.ds(pl.multiple_of(ci * c_len, c_len), c_len)
        g2 = g2_ref[rows, :]
        hi = g2.astype(BF16)
        lo = (g2 - hi.astype(F32)).astype(BF16)
        b2 = jnp.dot(tri2, jnp.concatenate([hi, lo], axis=0), preferred_element_type=F32)
        b2_s[rows, :] = b2
        gam_s[rows, :] = lk2_ref[rows, :] - b2
        return carry

    lax.fori_loop(0, n_chunks, sums, 0, unroll=GLA_UNROLL_SUMS)

    def padded_rows(x, lo, hi, total):
        lo_t, hi_t = lo // pack * pack, -(-hi // pack) * pack
        parts = ([jnp.zeros((lo - lo_t, HEAD_DIM), F32)] if lo > lo_t else []) + [x]
        if hi_t > hi:
            parts.append(jnp.zeros((hi_t - hi, HEAD_DIM), F32))
        tiles = [jnp.concatenate(parts, axis=0).astype(BF16)]
        if lo_t:
            tiles.insert(0, jnp.zeros((lo_t, HEAD_DIM), BF16))
        if total > hi_t:
            tiles.append(jnp.zeros((total - hi_t, HEAD_DIM), BF16))
        return jnp.concatenate(tiles, axis=0)

    def scores(ci, carry):
        off = pl.multiple_of(ci * c_len, c_len)
        rows = pl.ds(off, c_len)
        q = q_ref[rows, :].astype(F32)
        b2 = b2_s[rows, :]
        gam = gam_s[rows, :]
        b2_end = b2[0:1] if rev else b2[c_len - 1:c_len]
        qe_s[rows, :] = (q * jnp.exp2(b2)).astype(BF16)
        upd_s[ci] = lax.dot_general(v_ref[rows, :], jnp.exp2(b2_end + gam).astype(BF16), tn,
                                    preferred_element_type=F32)
        dec_s[ci] = jnp.exp2(b2_end)
        atts, pieces = [], []
        for i in range(n_sub):
            r0 = sub * i
            q_i, b_i = q[r0:r0 + sub], b2[r0:r0 + sub]
            if rev:
                others = (r0 + sub, c_len)
                edge = b2[r0 + sub:r0 + sub + 1] if i < n_sub - 1 else None
            else:
                others = (0, r0)
                edge = b2[r0 - 1:r0] if i > 0 else None
            if edge is None:
                atts.append(jnp.zeros((sub, HEAD_DIM), F32))
            else:
                qt = padded_rows(q_i * jnp.exp2(b_i - edge), 0, sub, pack)
                kt = padded_rows(jnp.exp2(edge + gam[others[0]:others[1]]), others[0], others[1],
                                 HEAD_DIM)
                atts.append(lax.dot_general(qt, kt, nt, preferred_element_type=F32)[:sub])
            row_pieces = []
            for sl in range(sub):
                g_row = gam_s[pl.ds(off + r0 + sl, 1), :]
                row_pieces.append(q_i * jnp.exp2(jnp.minimum(b_i + g_row, 0.0)))
            pieces.append(row_pieces)
        for i in range(0, n_sub, pack // sub):
            for sl in range(sub):
                tile = jnp.concatenate([pieces[i + u][sl] for u in range(pack // sub)], axis=0)
                p_s[pl.ds(off + sub * i, pack), sl * HEAD_DIM:(sl + 1) * HEAD_DIM] = tile.astype(BF16)
        att_s[rows, :] = jnp.concatenate(atts, axis=0)
        return carry

    lax.fori_loop(0, n_chunks, scores, 0, unroll=GLA_UNROLL_SCORES)

    dg_s[...] = jnp.dot(p_s[...], e_ref[...], preferred_element_type=F32)

    row = lax.broadcasted_iota(jnp.int32, (c_len, HEAD_DIM), 0)
    col = lax.broadcasted_iota(jnp.int32, (c_len, HEAD_DIM), 1)
    keep = (col >= row) if rev else (col <= row)
    v_pad = jnp.zeros((HEAD_DIM - c_len, HEAD_DIM), BF16)

    def state_step(cj, st):
        ci = (n_chunks - 1 - cj) if rev else cj
        seen_s[ci] = st.astype(BF16)
        return st * dec_s[ci] + upd_s[ci]

    st_s[...] = lax.fori_loop(0, n_chunks, state_step, st_s[...], unroll=True)

    def outputs(ci, carry):
        off = pl.multiple_of(ci * c_len, c_len)
        rows = pl.ds(off, c_len)
        dg = dg_s[rows, :]
        diag = [dg[0:sub]] + [pltpu.roll(dg[sub * i:sub * (i + 1)], sub * i, 1)
                              for i in range(1, n_sub)]
        att = att_s[rows, :] + jnp.concatenate(diag, axis=0)
        att = jnp.where(keep, att, 0.0).astype(BF16)
        o = (jnp.dot(att, jnp.concatenate([v_ref[rows, :], v_pad], axis=0),
                     preferred_element_type=F32)
             + lax.dot_general(qe_s[rows, :], seen_s[ci], nt, preferred_element_type=F32))
        if rev:
            tot = of_ref[rows, :] + o
            ms = jnp.mean(tot * tot, axis=-1, keepdims=True)
            y = tot * lax.rsqrt(ms + NORM_EPS) * ng_ref[...] * gt_ref[rows, :].astype(F32)
            o_ref[rows, :] = y.astype(o_ref.dtype)
        else:
            o_ref[rows, :] = o
        return carry

    lax.fori_loop(0, n_chunks, outputs, 0, unroll=GLA_UNROLL_OUT)


def _gla_tables(rev):
    idx = jnp.arange(GLA_CHUNK)
    tri = (idx[:, None] <= idx[None, :]) if rev else (idx[:, None] >= idx[None, :])
    tri2 = jnp.concatenate([tri, tri], axis=1).astype(BF16)
    s_of_row = jnp.arange(GLA_SUB * HEAD_DIM) // HEAD_DIM
    place = jnp.arange(HEAD_DIM)[None, :] == s_of_row[:, None]
    return tri2, place.astype(BF16)


def gla_direction(q, v, gate, dec, norm_g, o_fwd, batch, seq, rev):
    m, d = q.shape
    heads = d // HEAD_DIM
    t_blk = min(1024, seq)
    n_t = seq // t_blk
    n_chunks = t_blk // GLA_CHUNK
    tri2, place = _gla_tables(rev)

    def rowblk(b, h, t):
        return b * n_t + ((n_t - 1 - t) if rev else t)

    def col(group):
        return pl.BlockSpec((t_blk, HEAD_DIM), lambda b, h, t: (rowblk(b, h, t), group * heads + h))

    in_specs = [col(0), col(0), col(2 if rev else 0), col(3 if rev else 1),
                pl.BlockSpec(tri2.shape, lambda b, h, t: (0, 0)),
                pl.BlockSpec(place.shape, lambda b, h, t: (0, 0))]
    args = [q, v, dec, dec, tri2, place]
    if rev:
        in_specs += [col(0), col(0), pl.BlockSpec((1, HEAD_DIM), lambda b, h, t: (0, 0))]
        args += [o_fwd, gate, norm_g.reshape(1, HEAD_DIM)]
    return pl.pallas_call(
        functools.partial(_gla_kernel, rev=rev, n_chunks=n_chunks),
        grid=(batch, heads, n_t),
        in_specs=in_specs,
        out_specs=col(0),
        out_shape=jax.ShapeDtypeStruct((m, d), BF16 if rev else F32),
        scratch_shapes=[pltpu.VMEM((HEAD_DIM, HEAD_DIM), F32),
                        pltpu.VMEM((t_blk, HEAD_DIM), F32),
                        pltpu.VMEM((t_blk, HEAD_DIM), F32),
                        pltpu.VMEM((t_blk, HEAD_DIM), BF16),
                        pltpu.VMEM((n_chunks, 1, HEAD_DIM), F32),
                        pltpu.VMEM((n_chunks, HEAD_DIM, HEAD_DIM), F32),
                        pltpu.VMEM((n_chunks, HEAD_DIM, HEAD_DIM), BF16),
                        pltpu.VMEM((t_blk, HEAD_DIM), F32),
                        pltpu.VMEM((t_blk, GLA_SUB * HEAD_DIM), BF16),
                        pltpu.VMEM((t_blk, HEAD_DIM), F32)],
        compiler_params=_params("parallel", "parallel", "arbitrary"),
        name="gla_rev" if rev else "gla_fwd",
    )(*args)


def _short_conv_kernel(cur_ref, prev_ref, next_ref, w_ref, b_ref, x0_ref, z_ref, *, tiles_per_seq):
    i = pl.program_id(0)
    tm, d3 = cur_ref.shape
    d = d3 // 3
    cur = cur_ref[...]
    first = (i % tiles_per_seq) == 0
    last = (i % tiles_per_seq) == tiles_per_seq - 1
    prev_row = jnp.where(first, 0.0, prev_ref[SUBLANES - 1:SUBLANES, :])
    next_row = jnp.where(last, 0.0, next_ref[0:1, :])
    row = lax.broadcasted_iota(jnp.int32, (tm, d3), 0)
    before = jnp.where(row == 0, prev_row, pltpu.roll(cur, 1, 0))
    after = jnp.where(row == tm - 1, next_row, pltpu.roll(cur, tm - 1, 0))
    u = before * w_ref[0:1, :] + cur * w_ref[1:2, :] + after * w_ref[2:3, :] + b_ref[...]
    x0_ref[...] = u[:, :d]
    z_ref[...] = u[:, d:2 * d] * u[:, 2 * d:]


def short_conv_gate(u, conv_w, conv_b, seq):
    m, d3 = u.shape
    d = d3 // 3
    tm = min(256, seq)
    tiles_per_seq = seq // tm
    halo = tm // SUBLANES
    n_halo = m // SUBLANES
    return pl.pallas_call(
        functools.partial(_short_conv_kernel, tiles_per_seq=tiles_per_seq),
        grid=(m // tm,),
        in_specs=[
            pl.BlockSpec((tm, d3), lambda i: (i, 0)),
            pl.BlockSpec((SUBLANES, d3), lambda i: (jnp.maximum(i * halo - 1, 0), 0)),
            pl.BlockSpec((SUBLANES, d3), lambda i: (jnp.minimum((i + 1) * halo, n_halo - 1), 0)),
            pl.BlockSpec((3, d3), lambda i: (0, 0)),
            pl.BlockSpec((1, d3), lambda i: (0, 0)),
        ],
        out_specs=[pl.BlockSpec((tm, d), lambda i: (i, 0)),
                   pl.BlockSpec((tm, d), lambda i: (i, 0))],
        out_shape=[jax.ShapeDtypeStruct((m, d), F32), jax.ShapeDtypeStruct((m, d), F32)],
        compiler_params=_params("parallel"),
        name="short_conv",
    )(u, u, u, conv_w, conv_b.reshape(1, d3))


def _filter_kernel(emb_ref, t_ref, keep_ref, win_ref, wmid_ref, b_ref, freq_ref, wout_ref,
                   delta_ref, o_ref):
    def split(a):
        hi = a.astype(BF16)
        return hi, (a - hi.astype(F32)).astype(BF16)

    def dot3(a, w):
        a_hi, a_lo = split(a)
        w_hi, w_lo = split(w)
        mm = functools.partial(jnp.dot, preferred_element_type=F32)
        return mm(a_hi, w_hi) + mm(a_hi, w_lo) + mm(a_lo, w_hi)

    freq = freq_ref[...]
    h = jnp.sin(freq * (dot3(emb_ref[...], win_ref[...]) + b_ref[0:1, :]))
    for mth in range(wmid_ref.shape[0]):
        h = jnp.sin(freq * (dot3(h, wmid_ref[mth]) + b_ref[mth + 1:mth + 2, :]))
    hf = dot3(h, wout_ref[...])
    o_ref[...] = hf * jnp.exp(-t_ref[...] * delta_ref[...]) * keep_ref[...]


def hyena_filter_circular(seq, d, w_in, w_mid, b, freq, w_out):
    emb_dim, width = w_in.shape
    bands = (emb_dim - 1) // 2
    mrow = jnp.arange(2 * seq)
    pos = jnp.where(mrow < seq, mrow, jnp.where(mrow == seq, 0, 2 * seq - mrow)).astype(F32)[:, None]
    t_c = pos / (seq - 1.0)
    band = jnp.linspace(1e-4, bands - 1.0, bands, dtype=F32)
    ang = (2.0 * math.pi / seq) * pos * band
    emb_c = jnp.concatenate([t_c, jnp.cos(ang), -jnp.sin(ang)], axis=-1)
    emb_c = jnp.pad(emb_c, ((0, 0), (0, HY_FILTER_PAD - emb_dim)))
    keep = (mrow != seq).astype(F32)[:, None]
    w_in_p = jnp.pad(w_in.astype(F32), ((0, HY_FILTER_PAD - emb_dim), (0, 0)))
    deltas = jnp.abs(jnp.linspace(HY_MIN_DECAY, HY_MAX_DECAY, d, dtype=F32)).reshape(1, d)
    tr = min(512, seq)
    half = seq // tr
    n_mid = w_mid.shape[0]
    return pl.pallas_call(
        _filter_kernel,
        grid=(2 * seq // tr,),
        in_specs=[
            pl.BlockSpec((tr, HY_FILTER_PAD), lambda i: (i, 0)),
            pl.BlockSpec((tr, 1), lambda i: (i, 0)),
            pl.BlockSpec((tr, 1), lambda i: (i, 0)),
            pl.BlockSpec((HY_FILTER_PAD, width), lambda i: (0, 0)),
            pl.BlockSpec((n_mid, width, width), lambda i: (0, 0, 0)),
            pl.BlockSpec((n_mid + 1, width), lambda i: (0, 0)),
            pl.BlockSpec((1, width), lambda i: (0, 0)),
            pl.BlockSpec((width, d), lambda i: (0, i // half)),
            pl.BlockSpec((1, d), lambda i: (0, 0)),
        ],
        out_specs=pl.BlockSpec((tr, d), lambda i: (i, 0)),
        out_shape=jax.ShapeDtypeStruct((2 * seq, d), F32),
        compiler_params=_params("parallel"),
        name="hyena_filter",
    )(emb_c, t_c, keep, w_in_p, w_mid.astype(F32), b.astype(F32),
      freq.astype(F32).reshape(1, width), w_out.astype(F32), deltas)


def _swap_major_sublane(x):
    return jnp.swapaxes(x, 0, 1)


def _dft_stage1_kernel(m_ref, x_ref, o_ref):
    outs = []
    for g in range(x_ref.shape[1] // SUBLANES):
        xt = _swap_major_sublane(x_ref[:, g * SUBLANES:(g + 1) * SUBLANES, :]).astype(BF16)
        r = jnp.stack([jnp.dot(m_ref[...], xt[j], preferred_element_type=F32)
                       for j in range(SUBLANES)], axis=0)
        outs.append(_swap_major_sublane(r))
    o_ref[...] = jnp.concatenate(outs, axis=1).astype(o_ref.dtype)


def dft_stage1(mat, x):
    r, kdim = mat.shape
    _, n2, d = x.shape
    cw = _tile(d, 512, 128)
    return pl.pallas_call(
        _dft_stage1_kernel,
        grid=(n2 // BF16_ROWS, d // cw),
        in_specs=[pl.BlockSpec((r, kdim), lambda j, c: (0, 0)),
                  pl.BlockSpec((kdim, BF16_ROWS, cw), lambda j, c: (0, j, c))],
        out_specs=pl.BlockSpec((r, BF16_ROWS, cw), lambda j, c: (0, j, c)),
        out_shape=jax.ShapeDtypeStruct((r, n2, d), BF16),
        compiler_params=_params("parallel", "parallel"),
        name="dft_stage1",
    )(mat, x)


def _spectrum_kernel(g_ref, a_ref, o_ref):
    n2 = a_ref.shape[1]
    a = a_ref[...].reshape(2 * n2, a_ref.shape[2])
    o_ref[...] = jnp.dot(g_ref[...], a, preferred_element_type=F32).astype(o_ref.dtype).reshape(
        o_ref.shape)


def filter_spectrum(g_tab, a):
    _, n1, n2, d = a.shape
    return pl.pallas_call(
        _spectrum_kernel,
        grid=(n1,),
        in_specs=[pl.BlockSpec((None, 2 * n2, 2 * n2), lambda k: (k, 0, 0)),
                  pl.BlockSpec((2, None, n2, d), lambda k: (0, k, 0, 0))],
        out_specs=pl.BlockSpec((None, 2, n2, d), lambda k: (k, 0, 0, 0)),
        out_shape=jax.ShapeDtypeStruct((n1, 2, n2, d), BF16),
        compiler_params=_params("parallel"),
        name="filter_spectrum",
    )(g_tab, a)


def _freq_mul_kernel(g_ref, gi_ref, a_ref, h_ref, o_ref):
    n2 = a_ref.shape[1]
    d = a_ref.shape[2]
    a = a_ref[...].reshape(2 * n2, d)
    x = jnp.dot(g_ref[...], a, preferred_element_type=F32)
    xr, xi = x[:n2], x[n2:]
    hr, hi = h_ref[0].astype(F32), h_ref[1].astype(F32)
    p = jnp.concatenate([xr * hr - xi * hi, xr * hi + xi * hr], axis=0).astype(BF16)
    o_ref[...] = jnp.dot(gi_ref[...], p, preferred_element_type=F32).astype(o_ref.dtype).reshape(o_ref.shape)


def freq_multiply(g_tab, gi_tab, a, h):
    _, n1, n2, d = a.shape
    return pl.pallas_call(
        _freq_mul_kernel,
        grid=(n1,),
        in_specs=[pl.BlockSpec((None, 2 * n2, 2 * n2), lambda k: (k, 0, 0)),
                  pl.BlockSpec((None, 2 * n2, 2 * n2), lambda k: (k, 0, 0)),
                  pl.BlockSpec((2, None, n2, d), lambda k: (0, k, 0, 0)),
                  pl.BlockSpec((None, 2, n2, d), lambda k: (k, 0, 0, 0))],
        out_specs=pl.BlockSpec((2, None, n2, d), lambda k: (0, k, 0, 0)),
        out_shape=jax.ShapeDtypeStruct((2, n1, n2, d), BF16),
        compiler_params=_params("parallel"),
        name="freq_multiply",
    )(g_tab, gi_tab, a, h)


def _conv_out_kernel(m_ref, t_ref, x0_ref, z_ref, skip_ref, o_ref):
    t32 = t_ref[...].astype(F32)
    ys = []
    for g in range(t32.shape[1] // SUBLANES):
        tt = _swap_major_sublane(t32[:, g * SUBLANES:(g + 1) * SUBLANES, :]).astype(BF16)
        y = jnp.stack([jnp.dot(m_ref[...], tt[j], preferred_element_type=F32)
                       for j in range(SUBLANES)], axis=0)
        ys.append(_swap_major_sublane(y))
    y = jnp.concatenate(ys, axis=1)
    o_ref[...] = (x0_ref[...] * (y + skip_ref[...] * z_ref[...])).astype(o_ref.dtype)


def conv_output(mat, t, x0, z, skip):
    r, kdim = mat.shape
    _, n2, d = t.shape
    cw = _tile(d, 256, 128)
    blk = lambda rows: pl.BlockSpec((rows, BF16_ROWS, cw), lambda j, c: (0, j, c))
    return pl.pallas_call(
        _conv_out_kernel,
        grid=(n2 // BF16_ROWS, d // cw),
        in_specs=[pl.BlockSpec((r, kdim), lambda j, c: (0, 0)),
                  blk(kdim), blk(r), blk(r),
                  pl.BlockSpec((1, cw), lambda j, c: (0, c))],
        out_specs=blk(r),
        out_shape=jax.ShapeDtypeStruct((r, n2, d), BF16),
        compiler_params=_params("parallel", "parallel"),
        name="dft_inverse_out",
    )(mat, t, x0, z, skip)


def _dft_tables(seq):
    n = 2 * seq
    n2 = FFT_N2
    n1 = n // n2
    lh = seq // n2

    def cs(idx, mod):
        ang = (2.0 * math.pi / mod) * (idx % mod).astype(F32)
        return jnp.cos(ang), jnp.sin(ang)

    k1 = jnp.arange(n1)
    cr, sr = cs(k1[:, None] * jnp.arange(lh)[None, :], n1)
    f_sig = jnp.block([[cr, sr], [-sr, cr]])
    cf, sf = cs(k1[:, None] * k1[None, :], n1)
    f_filt = jnp.concatenate([cf, -sf], axis=0)
    idx = jnp.arange(n2)[None, None, :] * (k1[:, None, None] + n1 * jnp.arange(n2)[None, :, None])
    gr, gs = cs(idx, n)
    g_tab = jnp.concatenate([jnp.concatenate([gr, gs], axis=2),
                             jnp.concatenate([-gs, gr], axis=2)], axis=1)
    gi_tab = jnp.swapaxes(g_tab, 1, 2)
    er, es = cs(jnp.arange(lh)[:, None] * k1[None, :], n1)
    f_inv = jnp.block([[er, -es], [es, er]]) / n
    return (f_sig.astype(BF16), f_filt.astype(BF16), g_tab.astype(BF16), gi_tab.astype(BF16),
            f_inv.astype(BF16))


def hyena_long_conv(x0, z, filt, skip, tables, batch, seq):
    assert batch == 2, "the two sequences ride as real / imaginary parts of one complex signal"
    f_sig, f_filt, g_tab, gi_tab, f_inv = tables
    m, d = z.shape
    n2 = FFT_N2
    n1 = 2 * seq // n2
    lh = seq // n2
    a_f = dft_stage1(f_filt, filt.reshape(n1, n2, d))
    h = filter_spectrum(g_tab, a_f.reshape(2, n1, n2, d))
    a = dft_stage1(f_sig, z.reshape(batch * lh, n2, d))
    t = freq_multiply(g_tab, gi_tab, a.reshape(2, n1, n2, d), h)
    y = conv_output(f_inv, t.reshape(2 * n1, n2, d), x0.reshape(batch * lh, n2, d),
                    z.reshape(batch * lh, n2, d), skip.astype(F32).reshape(1, d))
    return y.reshape(m, d)


def kernel(x, c, ada_w, ada_b, norm_g, hg_w_in, hg_lower_bounds, hg_norm_g, hg_w_out,
           hy_w_in, hy_b_in, hy_conv_w, hy_conv_b, hy_filt_w_in, hy_filt_w_mid, hy_filt_b,
           hy_filt_freq, hy_filt_w_out, hy_skip, hy_w_out, mlp_w1, mlp_w2, final_g):
    batch, seq, d = x.shape
    depth = ada_w.shape[0]
    n_mixers = 2
    assert d % HEAD_DIM == 0 and seq % max(GLA_CHUNK, FFT_N2) == 0

    mod = adaln_all(c, ada_w, ada_b)
    mod = mod.reshape(2 * depth, batch, 3, 1, d)

    lbs = jax.nn.softmax(hg_lower_bounds.astype(F32), axis=1)
    lbs = jnp.cumsum(lbs, axis=1) - lbs[:, :1]

    tables = _dft_tables(seq)

    xf = x.reshape(batch * seq, d)
    for i in range(depth):
        j = i // n_mixers
        shift, scale, gate = mod[2 * i, :, 0], mod[2 * i, :, 1], mod[2 * i, :, 2]
        if i % n_mixers == 0:
            q, v, gt, dec = hgrn_projection(xf, norm_g[i, 0], scale, shift, hg_w_in[j].astype(BF16),
                                            lbs[:, j], seq)
            o_f = gla_direction(q, v, gt, dec, hg_norm_g[j], None, batch, seq, rev=False)
            mix = gla_direction(q, v, gt, dec, hg_norm_g[j], o_f, batch, seq, rev=True)
            w_out = hg_w_out[j]
        else:
            u = norm_mod_matmul(xf, norm_g[i, 0], scale, shift, hy_w_in[j].astype(BF16),
                                hy_b_in[j], seq)
            x0, z = short_conv_gate(u, hy_conv_w[j], hy_conv_b[j], seq)
            filt = hyena_filter_circular(seq, d, hy_filt_w_in[j], hy_filt_w_mid[j], hy_filt_b[j],
                                         hy_filt_freq[j], hy_filt_w_out[j])
            mix = hyena_long_conv(x0, z, filt, hy_skip[j], tables, batch, seq)
            w_out = hy_w_out[j]
        gate_mix = gate
        shift, scale, gate = mod[2 * i + 1, :, 0], mod[2 * i + 1, :, 1], mod[2 * i + 1, :, 2]
        xf = mixer_out_mlp(mix, w_out.astype(BF16), xf, gate_mix, norm_g[i, 1], scale, shift, gate,
                           mlp_w1[i].astype(BF16), mlp_w2[i].astype(BF16), final_g, seq,
                           final_norm=(i == depth - 1))
    return xf.reshape(batch, seq, d)
```

```python
import functools
import math

import jax
import jax.numpy as jnp
from jax import lax
from jax.experimental import pallas as pl
from jax.experimental.pallas import tpu as pltpu

F32 = jnp.float32
BF16 = jnp.bfloat16

NORM_EPS = 1e-6
HEAD_DIM = 128
GLA_CHUNK = 64
GLA_SUB = 8
GLA_UNROLL_SUMS = 16
GLA_UNROLL_SCORES = 16
GLA_UNROLL_OUT = 16
SUBLANES = 8
BF16_ROWS = 16
HGRN_PROJ_SLAB = 256
FFT_N2 = 128
HY_FILTER_PAD = 64
HY_MAX_DECAY = math.log(1e-2) / 0.3
HY_MIN_DECAY = math.log(1e-2) / 1.5
VMEM_LIMIT = 56 * 1024 * 1024


def _params(*sem):
    return pltpu.CompilerParams(dimension_semantics=sem, vmem_limit_bytes=VMEM_LIMIT)


def _tile(n, target, unit):
    t = min(n, target) // unit * unit
    while n % t:
        t -= unit
    return t


def _rms_mod(x, g, scale, shift):
    ms = jnp.mean(x * x, axis=-1, keepdims=True)
    y = x * lax.rsqrt(ms + NORM_EPS) * g
    return y * (1.0 + scale) + shift


def _adaln_kernel(s_ref, w_ref, b_ref, o_ref):
    s = s_ref[...]
    s = s * jax.nn.sigmoid(s)
    o_ref[...] = jnp.dot(s, w_ref[...], preferred_element_type=F32,
                         precision=lax.Precision.HIGHEST) + b_ref[...]


def adaln_all(c, ada_w, ada_b):
    depth, two, d, d3 = ada_w.shape
    n = depth * two
    b = c.shape[0]
    rows = -(-b // SUBLANES) * SUBLANES
    cp = jnp.zeros((rows, d), F32).at[:b].set(c)
    tn = _tile(d3, 1536, 128)
    out = pl.pallas_call(
        _adaln_kernel,
        grid=(n, d3 // tn),
        in_specs=[
            pl.BlockSpec((rows, d), lambda i, j: (0, 0)),
            pl.BlockSpec((None, d, tn), lambda i, j: (i, 0, j)),
            pl.BlockSpec((None, 1, tn), lambda i, j: (i, 0, j)),
        ],
        out_specs=pl.BlockSpec((None, rows, tn), lambda i, j: (i, 0, j)),
        out_shape=jax.ShapeDtypeStruct((n, rows, d3), F32),
        compiler_params=_params("parallel", "parallel"),
        name="adaln",
    )(cp, ada_w.reshape(n, d, d3), ada_b.reshape(n, 1, d3))
    return out[:, :b]


def _resident(shape):
    return pl.BlockSpec(shape, lambda i: (0,) * len(shape), pipeline_mode=pl.Buffered(1))


def _out_mlp_kernel(a_ref, wo_ref, x_ref, gm_ref, g_ref, sc_ref, sh_ref, gate_ref, w1_ref, w2_ref,
                    fg_ref, o_ref, hn_ref, acc_ref, *, final_norm, tf):
    o_ref[...] = x_ref[...] + gm_ref[...] * jnp.dot(a_ref[...], wo_ref[...],
                                                     preferred_element_type=F32)
    hn_ref[...] = _rms_mod(o_ref[...], g_ref[...], sc_ref[...], sh_ref[...]).astype(BF16)
    for k in range(w1_ref.shape[1] // tf):
        cols = slice(k * tf, (k + 1) * tf)
        h = jnp.dot(hn_ref[...], w1_ref[:, cols], preferred_element_type=F32)
        h = jnp.square(jnp.maximum(h, 0.0)).astype(BF16)
        part = jnp.dot(h, w2_ref[cols, :], preferred_element_type=F32)
        if k == 0:
            acc_ref[...] = part
        else:
            acc_ref[...] += part
    out = o_ref[...] + gate_ref[...] * acc_ref[...]
    if final_norm:
        ms = jnp.mean(out * out, axis=-1, keepdims=True)
        out = out * lax.rsqrt(ms + NORM_EPS) * fg_ref[...]
    o_ref[...] = out


def mixer_out_mlp(a, w_out, x, gate_mix, g, scale, shift, gate, w1, w2, final_g, seq, final_norm):
    m, d = x.shape
    dff = w1.shape[1]
    tm = min(512, seq)
    tf = _tile(dff, 1024, 128)
    per_seq = seq // tm
    vec = pl.BlockSpec((None, 1, d), lambda i: (i // per_seq, 0, 0))
    return pl.pallas_call(
        functools.partial(_out_mlp_kernel, final_norm=final_norm, tf=tf),
        grid=(m // tm,),
        in_specs=[
            pl.BlockSpec((tm, d), lambda i: (i, 0)),
            _resident((d, d)),
            pl.BlockSpec((tm, d), lambda i: (i, 0)),
            vec,
            _resident((1, d)),
            vec, vec, vec,
            _resident((d, dff)),
            _resident((dff, d)),
            _resident((1, d)),
        ],
        out_specs=pl.BlockSpec((tm, d), lambda i: (i, 0)),
        out_shape=jax.ShapeDtypeStruct((m, d), F32),
        scratch_shapes=[pltpu.VMEM((tm, d), BF16), pltpu.VMEM((tm, d), F32)],
        compiler_params=_params("parallel"),
        name="mixer_out_mlp",
    )(a, w_out, x, gate_mix, g.reshape(1, d), scale, shift, gate, w1, w2, final_g.reshape(1, d))


def _hgrn_proj_kernel(x_ref, g_ref, sc_ref, sh_ref, w_ref, lb_ref, q_ref, v_ref, gt_ref, dec_ref,
                      hn_ref):
    d = q_ref.shape[1]
    hn_ref[...] = _rms_mod(x_ref[...], g_ref[...], sc_ref[...], sh_ref[...]).astype(BF16)

    slab = min(d, HGRN_PROJ_SLAB)
    for c0 in range(0, d, slab):
        cols = slice(c0, c0 + slab)

        def proj(group):
            return jnp.dot(hn_ref[...], w_ref[:, group * d + c0:group * d + c0 + slab],
                           preferred_element_type=F32)

        qz = proj(0)
        q_ref[:, cols] = (qz * (1.0 / (1.0 + jnp.exp(-qz)))).astype(BF16)
        v_ref[:, cols] = proj(1).astype(BF16)
        for dirn in range(2):
            z = proj(2 + dirn)
            lb = lb_ref[dirn:dirn + 1, cols]
            one_minus_lb = 1.0 - lb
            dec_ref[:, 2 * dirn * d + c0:2 * dirn * d + c0 + slab] = jnp.log2(
                lb + one_minus_lb * (1.0 / (1.0 + jnp.exp(-z))))
            dec_ref[:, (2 * dirn + 1) * d + c0:(2 * dirn + 1) * d + c0 + slab] = (
                jnp.log2(one_minus_lb) - jnp.log2(1.0 + jnp.exp(z)))
        gz = proj(4)
        gt_ref[:, cols] = (gz * (1.0 / (1.0 + jnp.exp(-gz)))).astype(BF16)


def hgrn_projection(x, g, scale, shift, w, lbs, seq):
    m, d = x.shape
    tm = min(512, seq)
    per_seq = seq // tm
    row = lambda width: pl.BlockSpec((tm, width), lambda i: (i, 0))
    return pl.pallas_call(
        _hgrn_proj_kernel,
        grid=(m // tm,),
        in_specs=[
            row(d),
            _resident((1, d)),
            pl.BlockSpec((None, 1, d), lambda i: (i // per_seq, 0, 0)),
            pl.BlockSpec((None, 1, d), lambda i: (i // per_seq, 0, 0)),
            _resident((d, 5 * d)),
            _resident((2, d)),
        ],
        out_specs=[row(d), row(d), row(d), row(4 * d)],
        out_shape=[jax.ShapeDtypeStruct((m, d), BF16), jax.ShapeDtypeStruct((m, d), BF16),
                   jax.ShapeDtypeStruct((m, d), BF16), jax.ShapeDtypeStruct((m, 4 * d), F32)],
        scratch_shapes=[pltpu.VMEM((tm, d), BF16)],
        compiler_params=_params("parallel"),
        name="hgrn_projection",
    )(x, g.reshape(1, d), scale, shift, w, lbs)


def _gla_kernel(*refs, rev, n_chunks):
    if rev:
        (q_ref, v_ref, g2_ref, lk2_ref, tri_ref, e_ref, of_ref, gt_ref, ng_ref, o_ref,
         st_s, b2_s, gam_s, qe_s, dec_s, upd_s, seen_s, att_s, p_s, dg_s) = refs
    else:
        (q_ref, v_ref, g2_ref, lk2_ref, tri_ref, e_ref, o_ref,
         st_s, b2_s, gam_s, qe_s, dec_s, upd_s, seen_s, att_s, p_s, dg_s) = refs
    c_len, sub, pack = GLA_CHUNK, GLA_SUB, BF16_ROWS
    n_sub = c_len // sub
    nt = (((1,), (1,)), ((), ()))
    tn = (((0,), (0,)), ((), ()))

    @pl.when(pl.program_id(2) == 0)
    def _():
        st_s[...] = jnp.zeros_like(st_s)

    tri2 = tri_ref[...]

    def sums(ci, carry):
        rows = pl.ds(pl.multiple_of(ci * c_len, c_len), c_len)
        g2 = g2_ref[rows, :]
        hi = g2.astype(BF16)
        lo = (g2 - hi.astype(F32)).astype(BF16)
        b2 = jnp.dot(tri2, jnp.concatenate([hi, lo], axis=0), preferred_element_type=F32)
        b2_s[rows, :] = b2
        gam_s[rows, :] = lk2_ref[rows, :] - b2
        return carry

    lax.fori_loop(0, n_chunks, sums, 0, unroll=GLA_UNROLL_SUMS)

    def padded_rows(x, lo, hi, total):
        lo_t, hi_t = lo // pack * pack, -(-hi // pack) * pack
        parts = ([jnp.zeros((lo - lo_t, HEAD_DIM), F32)] if lo > lo_t else []) + [x]
        if hi_t > hi:
            parts.append(jnp.zeros((hi_t - hi, HEAD_DIM), F32))
        tiles = [jnp.concatenate(parts, axis=0).astype(BF16)]
        if lo_t:
            tiles.insert(0, jnp.zeros((lo_t, HEAD_DIM), BF16))
        if total > hi_t:
            tiles.append(jnp.zeros((total - hi_t, HEAD_DIM), BF16))
        return jnp.concatenate(tiles, axis=0)

    def scores(ci, carry):
        off = pl.multiple_of(ci * c_len, c_len)
        rows = pl.ds(off, c_len)
        q = q_ref[rows, :].astype(F32)
        b2 = b2_s[rows, :]
        gam = gam_s[rows, :]
        b2_end = b2[0:1] if rev else b2[c_len - 1:c_len]
        qe_s[rows, :] = (q * jnp.exp2(b2)).astype(BF16)
        upd_s[ci] = lax.dot_general(v_ref[rows, :], jnp.exp2(b2_end + gam).astype(BF16), tn,
                                    preferred_element_type=F32)
        dec_s[ci] = jnp.exp2(b2_end)
        atts = []
        for i in range(n_sub):
            r0 = sub * i
            q_i, b_i = q[r0:r0 + sub], b2[r0:r0 + sub]
            if rev:
                others = (r0 + sub, c_len)
                edge = b2[r0 + sub:r0 + sub + 1] if i < n_sub - 1 else None
            else:
                others = (0, r0)
                edge = b2[r0 - 1:r0] if i > 0 else None
            if edge is None:
                atts.append(jnp.zeros((sub, HEAD_DIM), F32))
            else:
                qt = padded_rows(q_i * jnp.exp2(b_i - edge), 0, sub, pack)
                kt = padded_rows(jnp.exp2(edge + gam[others[0]:others[1]]), others[0], others[1],
                                 HEAD_DIM)
                atts.append(lax.dot_general(qt, kt, nt, preferred_element_type=F32)[:sub])
        att_s[rows, :] = jnp.concatenate(atts, axis=0)
        for r0 in range(0, c_len, pack):
            q_t, b_t = q[r0:r0 + pack], b2[r0:r0 + pack]
            for sl in range(sub):
                g_rows = jnp.concatenate(
                    [jnp.broadcast_to(gam_s[pl.ds(off + r0 + u * sub + sl, 1), :], (sub, HEAD_DIM))
                     for u in range(pack // sub)], axis=0)
                tile = q_t * jnp.exp2(jnp.minimum(b_t + g_rows, 0.0))
                p_s[pl.ds(off + r0, pack), sl * HEAD_DIM:(sl + 1) * HEAD_DIM] = tile.astype(BF16)
        return carry

    lax.fori_loop(0, n_chunks, scores, 0, unroll=GLA_UNROLL_SCORES)

    dg_s[...] = jnp.dot(p_s[...], e_ref[...], preferred_element_type=F32)

    row = lax.broadcasted_iota(jnp.int32, (c_len, HEAD_DIM), 0)
    col = lax.broadcasted_iota(jnp.int32, (c_len, HEAD_DIM), 1)
    keep = (col >= row) if rev else (col <= row)
    v_pad = jnp.zeros((HEAD_DIM - c_len, HEAD_DIM), BF16)

    def state_step(cj, st):
        ci = (n_chunks - 1 - cj) if rev else cj
        seen_s[ci] = st.astype(BF16)
        return st * dec_s[ci] + upd_s[ci]

    st_s[...] = lax.fori_loop(0, n_chunks, state_step, st_s[...], unroll=True)

    def outputs(ci, carry):
        off = pl.multiple_of(ci * c_len, c_len)
        rows = pl.ds(off, c_len)
        dg = dg_s[rows, :]
        diag = [dg[0:sub]] + [pltpu.roll(dg[sub * i:sub * (i + 1)], sub * i, 1)
                              for i in range(1, n_sub)]
        att = att_s[rows, :] + jnp.concatenate(diag, axis=0)
        att = jnp.where(keep, att, 0.0).astype(BF16)
        o = (jnp.dot(att, jnp.concatenate([v_ref[rows, :], v_pad], axis=0),
                     preferred_element_type=F32)
             + lax.dot_general(qe_s[rows, :], seen_s[ci], nt, preferred_element_type=F32))
        if rev:
            tot = of_ref[rows, :] + o
            ms = jnp.mean(tot * tot, axis=-1, keepdims=True)
            y = tot * lax.rsqrt(ms + NORM_EPS) * ng_ref[...] * gt_ref[rows, :].astype(F32)
            o_ref[rows, :] = y.astype(o_ref.dtype)
        else:
            o_ref[rows, :] = o
        return carry

    lax.fori_loop(0, n_chunks, outputs, 0, unroll=GLA_UNROLL_OUT)


def _gla_tables(rev):
    idx = jnp.arange(GLA_CHUNK)
    tri = (idx[:, None] <= idx[None, :]) if rev else (idx[:, None] >= idx[None, :])
    tri2 = jnp.concatenate([tri, tri], axis=1).astype(BF16)
    s_of_row = jnp.arange(GLA_SUB * HEAD_DIM) // HEAD_DIM
    place = jnp.arange(HEAD_DIM)[None, :] == s_of_row[:, None]
    return tri2, place.astype(BF16)


def gla_direction(q, v, gate, dec, norm_g, o_fwd, batch, seq, rev):
    m, d = q.shape
    heads = d // HEAD_DIM
    t_blk = min(1024, seq)
    n_t = seq // t_blk
    n_chunks = t_blk // GLA_CHUNK
    tri2, place = _gla_tables(rev)

    def rowblk(b, h, t):
        return b * n_t + ((n_t - 1 - t) if rev else t)

    def col(group):
        return pl.BlockSpec((t_blk, HEAD_DIM), lambda b, h, t: (rowblk(b, h, t), group * heads + h))

    in_specs = [col(0), col(0), col(2 if rev else 0), col(3 if rev else 1),
                pl.BlockSpec(tri2.shape, lambda b, h, t: (0, 0)),
                pl.BlockSpec(place.shape, lambda b, h, t: (0, 0))]
    args = [q, v, dec, dec, tri2, place]
    if rev:
        in_specs += [col(0), col(0), pl.BlockSpec((1, HEAD_DIM), lambda b, h, t: (0, 0))]
        args += [o_fwd, gate, norm_g.reshape(1, HEAD_DIM)]
    return pl.pallas_call(
        functools.partial(_gla_kernel, rev=rev, n_chunks=n_chunks),
        grid=(batch, heads, n_t),
        in_specs=in_specs,
        out_specs=col(0),
        out_shape=jax.ShapeDtypeStruct((m, d), BF16 if rev else F32),
        scratch_shapes=[pltpu.VMEM((HEAD_DIM, HEAD_DIM), F32),
                        pltpu.VMEM((t_blk, HEAD_DIM), F32),
                        pltpu.VMEM((t_blk, HEAD_DIM), F32),
                        pltpu.VMEM((t_blk, HEAD_DIM), BF16),
                        pltpu.VMEM((n_chunks, 1, HEAD_DIM), F32),
                        pltpu.VMEM((n_chunks, HEAD_DIM, HEAD_DIM), F32),
                        pltpu.VMEM((n_chunks, HEAD_DIM, HEAD_DIM), BF16),
                        pltpu.VMEM((t_blk, HEAD_DIM), F32),
                        pltpu.VMEM((t_blk, GLA_SUB * HEAD_DIM), BF16),
                        pltpu.VMEM((t_blk, HEAD_DIM), F32)],
        compiler_params=_params("parallel", "parallel", "arbitrary"),
        name="gla_rev" if rev else "gla_fwd",
    )(*args)


def _hyena_proj_kernel(x_ref, xp_ref, xn_ref, g_ref, sc_ref, sh_ref, w_ref, b_ref, cw_ref, cb_ref,
                       x0_ref, z_ref, hn_ref, u_ref, *, per_seq, cs):
    i = pl.program_id(0)
    tm, d = x_ref.shape
    halo = xp_ref.shape[0]
    keep_prev = jnp.where(i % per_seq == 0, 0.0, 1.0)
    keep_next = jnp.where(i % per_seq == per_seq - 1, 0.0, 1.0)

    def norm(ref):
        return _rms_mod(ref[...], g_ref[...], sc_ref[...], sh_ref[...]).astype(BF16)

    hn_ref[0:halo, :] = norm(xp_ref)
    hn_ref[halo:halo + tm, :] = norm(x_ref)
    hn_ref[halo + tm:, :] = norm(xn_ref)
    row = lax.broadcasted_iota(jnp.int32, (tm + 2 * halo, cs), 0)
    keep = jnp.where(row < halo, keep_prev, jnp.where(row >= halo + tm, keep_next, 1.0))

    for c0 in range(0, d, cs):
        def conv(group):
            cols = slice(group * d + c0, group * d + c0 + cs)
            u = jnp.dot(hn_ref[...], w_ref[:, cols], preferred_element_type=F32) + b_ref[:, cols]
            u_ref[group] = u * keep
            return (u_ref[group, halo - 1:halo - 1 + tm, :] * cw_ref[0:1, cols]
                    + u_ref[group, halo:halo + tm, :] * cw_ref[1:2, cols]
                    + u_ref[group, halo + 1:halo + 1 + tm, :] * cw_ref[2:3, cols]
                    + cb_ref[:, cols])

        x0_ref[:, c0:c0 + cs] = conv(0)
        z_ref[:, c0:c0 + cs] = conv(1) * conv(2)


def hyena_projection(x, g, scale, shift, w, bias, conv_w, conv_b, seq):
    m, d = x.shape
    tm = min(512, seq)
    per_seq = seq // tm
    halo = BF16_ROWS
    per_tile = tm // halo
    n_halo = m // halo
    cs = _tile(d, 512, 128)
    vec = pl.BlockSpec((None, 1, d), lambda i: (i // per_seq, 0, 0))
    return pl.pallas_call(
        functools.partial(_hyena_proj_kernel, per_seq=per_seq, cs=cs),
        grid=(m // tm,),
        in_specs=[
            pl.BlockSpec((tm, d), lambda i: (i, 0)),
            pl.BlockSpec((halo, d), lambda i: (jnp.maximum(i * per_tile - 1, 0), 0)),
            pl.BlockSpec((halo, d), lambda i: (jnp.minimum((i + 1) * per_tile, n_halo - 1), 0)),
            _resident((1, d)),
            vec, vec,
            _resident((d, 3 * d)),
            _resident((1, 3 * d)),
            _resident((3, 3 * d)),
            _resident((1, 3 * d)),
        ],
        out_specs=[pl.BlockSpec((tm, d), lambda i: (i, 0)),
                   pl.BlockSpec((tm, d), lambda i: (i, 0))],
        out_shape=[jax.ShapeDtypeStruct((m, d), F32), jax.ShapeDtypeStruct((m, d), F32)],
        scratch_shapes=[pltpu.VMEM((tm + 2 * halo, d), BF16),
                        pltpu.VMEM((3, tm + 2 * halo, cs), F32)],
        compiler_params=_params("parallel"),
        name="hyena_projection",
    )(x, x, x, g.reshape(1, d), scale, shift, w, bias.reshape(1, 3 * d), conv_w,
      conv_b.reshape(1, 3 * d))


def _filter_kernel(emb_ref, t_ref, keep_ref, win_ref, wmid_ref, b_ref, freq_ref, wout_ref,
                   delta_ref, o_ref):
    def split(a):
        hi = a.astype(BF16)
        return hi, (a - hi.astype(F32)).astype(BF16)

    def dot3(a, w):
        a_hi, a_lo = split(a)
        w_hi, w_lo = split(w)
        mm = functools.partial(jnp.dot, preferred_element_type=F32)
        return mm(a_hi, w_hi) + mm(a_hi, w_lo) + mm(a_lo, w_hi)

    freq = freq_ref[...]
    h = jnp.sin(freq * (dot3(emb_ref[...], win_ref[...]) + b_ref[0:1, :]))
    for mth in range(wmid_ref.shape[0]):
        h = jnp.sin(freq * (dot3(h, wmid_ref[mth]) + b_ref[mth + 1:mth + 2, :]))
    hf = dot3(h, wout_ref[...])
    o_ref[...] = hf * jnp.exp(-t_ref[...] * delta_ref[...]) * keep_ref[...]


def hyena_filter_circular(seq, d, w_in, w_mid, b, freq, w_out):
    emb_dim, width = w_in.shape
    bands = (emb_dim - 1) // 2
    mrow = jnp.arange(2 * seq)
    pos = jnp.where(mrow < seq, mrow, jnp.where(mrow == seq, 0, 2 * seq - mrow)).astype(F32)[:, None]
    t_c = pos / (seq - 1.0)
    band = jnp.linspace(1e-4, bands - 1.0, bands, dtype=F32)
    ang = (2.0 * math.pi / seq) * pos * band
    emb_c = jnp.concatenate([t_c, jnp.cos(ang), -jnp.sin(ang)], axis=-1)
    emb_c = jnp.pad(emb_c, ((0, 0), (0, HY_FILTER_PAD - emb_dim)))
    keep = (mrow != seq).astype(F32)[:, None]
    w_in_p = jnp.pad(w_in.astype(F32), ((0, HY_FILTER_PAD - emb_dim), (0, 0)))
    deltas = jnp.abs(jnp.linspace(HY_MIN_DECAY, HY_MAX_DECAY, d, dtype=F32)).reshape(1, d)
    tr = min(512, seq)
    half = seq // tr
    n_mid = w_mid.shape[0]
    return pl.pallas_call(
        _filter_kernel,
        grid=(2 * seq // tr,),
        in_specs=[
            pl.BlockSpec((tr, HY_FILTER_PAD), lambda i: (i, 0)),
            pl.BlockSpec((tr, 1), lambda i: (i, 0)),
            pl.BlockSpec((tr, 1), lambda i: (i, 0)),
            pl.BlockSpec((HY_FILTER_PAD, width), lambda i: (0, 0)),
            pl.BlockSpec((n_mid, width, width), lambda i: (0, 0, 0)),
            pl.BlockSpec((n_mid + 1, width), lambda i: (0, 0)),
            pl.BlockSpec((1, width), lambda i: (0, 0)),
            pl.BlockSpec((width, d), lambda i: (0, i // half)),
            pl.BlockSpec((1, d), lambda i: (0, 0)),
        ],
        out_specs=pl.BlockSpec((tr, d), lambda i: (i, 0)),
        out_shape=jax.ShapeDtypeStruct((2 * seq, d), F32),
        compiler_params=_params("parallel"),
        name="hyena_filter",
    )(emb_c, t_c, keep, w_in_p, w_mid.astype(F32), b.astype(F32),
      freq.astype(F32).reshape(1, width), w_out.astype(F32), deltas)


def _swap_major_sublane(x):
    return jnp.swapaxes(x, 0, 1)


def _dft_stage1_kernel(m_ref, x_ref, o_ref):
    outs = []
    for g in range(x_ref.shape[1] // SUBLANES):
        xt = _swap_major_sublane(x_ref[:, g * SUBLANES:(g + 1) * SUBLANES, :]).astype(BF16)
        r = jnp.stack([jnp.dot(m_ref[...], xt[j], preferred_element_type=F32)
                       for j in range(SUBLANES)], axis=0)
        outs.append(_swap_major_sublane(r))
    o_ref[...] = jnp.concatenate(outs, axis=1).astype(o_ref.dtype)


def dft_stage1(mat, x):
    r, kdim = mat.shape
    _, n2, d = x.shape
    cw = _tile(d, 512, 128)
    return pl.pallas_call(
        _dft_stage1_kernel,
        grid=(n2 // BF16_ROWS, d // cw),
        in_specs=[pl.BlockSpec((r, kdim), lambda j, c: (0, 0)),
                  pl.BlockSpec((kdim, BF16_ROWS, cw), lambda j, c: (0, j, c))],
        out_specs=pl.BlockSpec((r, BF16_ROWS, cw), lambda j, c: (0, j, c)),
        out_shape=jax.ShapeDtypeStruct((r, n2, d), BF16),
        compiler_params=_params("parallel", "parallel"),
        name="dft_stage1",
    )(mat, x)


def _spectrum_kernel(g_ref, a_ref, o_ref):
    n2 = a_ref.shape[1]
    a = a_ref[...].reshape(2 * n2, a_ref.shape[2])
    o_ref[...] = jnp.dot(g_ref[...], a, preferred_element_type=F32).astype(o_ref.dtype).reshape(
        o_ref.shape)


def filter_spectrum(g_tab, a):
    _, n1, n2, d = a.shape
    return pl.pallas_call(
        _spectrum_kernel,
        grid=(n1,),
        in_specs=[pl.BlockSpec((None, 2 * n2, 2 * n2), lambda k: (k, 0, 0)),
                  pl.BlockSpec((2, None, n2, d), lambda k: (0, k, 0, 0))],
        out_specs=pl.BlockSpec((None, 2, n2, d), lambda k: (k, 0, 0, 0)),
        out_shape=jax.ShapeDtypeStruct((n1, 2, n2, d), BF16),
        compiler_params=_params("parallel"),
        name="filter_spectrum",
    )(g_tab, a)


def _freq_mul_kernel(g_ref, gi_ref, a_ref, h_ref, o_ref):
    n2 = a_ref.shape[1]
    d = a_ref.shape[2]
    a = a_ref[...].reshape(2 * n2, d)
    x = jnp.dot(g_ref[...], a, preferred_element_type=F32)
    xr, xi = x[:n2], x[n2:]
    hr, hi = h_ref[0].astype(F32), h_ref[1].astype(F32)
    p = jnp.concatenate([xr * hr - xi * hi, xr * hi + xi * hr], axis=0).astype(BF16)
    o_ref[...] = jnp.dot(gi_ref[...], p, preferred_element_type=F32).astype(o_ref.dtype).reshape(o_ref.shape)


def freq_multiply(g_tab, gi_tab, a, h):
    _, n1, n2, d = a.shape
    return pl.pallas_call(
        _freq_mul_kernel,
        grid=(n1,),
        in_specs=[pl.BlockSpec((None, 2 * n2, 2 * n2), lambda k: (k, 0, 0)),
                  pl.BlockSpec((None, 2 * n2, 2 * n2), lambda k: (k, 0, 0)),
                  pl.BlockSpec((2, None, n2, d), lambda k: (0, k, 0, 0)),
                  pl.BlockSpec((None, 2, n2, d), lambda k: (k, 0, 0, 0))],
        out_specs=pl.BlockSpec((2, None, n2, d), lambda k: (0, k, 0, 0)),
        out_shape=jax.ShapeDtypeStruct((2, n1, n2, d), BF16),
        compiler_params=_params("parallel"),
        name="freq_multiply",
    )(g_tab, gi_tab, a, h)


def _conv_out_kernel(m_ref, t_ref, x0_ref, z_ref, skip_ref, o_ref):
    t32 = t_ref[...].astype(F32)
    ys = []
    for g in range(t32.shape[1] // SUBLANES):
        tt = _swap_major_sublane(t32[:, g * SUBLANES:(g + 1) * SUBLANES, :]).astype(BF16)
        y = jnp.stack([jnp.dot(m_ref[...], tt[j], preferred_element_type=F32)
                       for j in range(SUBLANES)], axis=0)
        ys.append(_swap_major_sublane(y))
    y = jnp.concatenate(ys, axis=1)
    o_ref[...] = (x0_ref[...] * (y + skip_ref[...] * z_ref[...])).astype(o_ref.dtype)


def conv_output(mat, t, x0, z, skip):
    r, kdim = mat.shape
    _, n2, d = t.shape
    cw = _tile(d, 256, 128)
    blk = lambda rows: pl.BlockSpec((rows, BF16_ROWS, cw), lambda j, c: (0, j, c))
    return pl.pallas_call(
        _conv_out_kernel,
        grid=(n2 // BF16_ROWS, d // cw),
        in_specs=[pl.BlockSpec((r, kdim), lambda j, c: (0, 0)),
                  blk(kdim), blk(r), blk(r),
                  pl.BlockSpec((1, cw), lambda j, c: (0, c))],
        out_specs=blk(r),
        out_shape=jax.ShapeDtypeStruct((r, n2, d), BF16),
        compiler_params=_params("parallel", "parallel"),
        name="dft_inverse_out",
    )(mat, t, x0, z, skip)


def _dft_tables(seq):
    n = 2 * seq
    n2 = FFT_N2
    n1 = n // n2
    lh = seq // n2

    def cs(idx, mod):
        ang = (2.0 * math.pi / mod) * (idx % mod).astype(F32)
        return jnp.cos(ang), jnp.sin(ang)

    k1 = jnp.arange(n1)
    cr, sr = cs(k1[:, None] * jnp.arange(lh)[None, :], n1)
    f_sig = jnp.block([[cr, sr], [-sr, cr]])
    cf, sf = cs(k1[:, None] * k1[None, :], n1)
    f_filt = jnp.concatenate([cf, -sf], axis=0)
    idx = jnp.arange(n2)[None, None, :] * (k1[:, None, None] + n1 * jnp.arange(n2)[None, :, None])
    gr, gs = cs(idx, n)
    g_tab = jnp.concatenate([jnp.concatenate([gr, gs], axis=2),
                             jnp.concatenate([-gs, gr], axis=2)], axis=1)
    gi_tab = jnp.swapaxes(g_tab, 1, 2)
    er, es = cs(jnp.arange(lh)[:, None] * k1[None, :], n1)
    f_inv = jnp.block([[er, -es], [es, er]]) / n
    return (f_sig.astype(BF16), f_filt.astype(BF16), g_tab.astype(BF16), gi_tab.astype(BF16),
            f_inv.astype(BF16))


def hyena_long_conv(x0, z, filt, skip, tables, batch, seq):
    assert batch == 2, "the two sequences ride as real / imaginary parts of one complex signal"
    f_sig, f_filt, g_tab, gi_tab, f_inv = tables
    m, d = z.shape
    n2 = FFT_N2
    n1 = 2 * seq // n2
    lh = seq // n2
    a_f = dft_stage1(f_filt, filt.reshape(n1, n2, d))
    h = filter_spectrum(g_tab, a_f.reshape(2, n1, n2, d))
    a = dft_stage1(f_sig, z.reshape(batch * lh, n2, d))
    t = freq_multiply(g_tab, gi_tab, a.reshape(2, n1, n2, d), h)
    y = conv_output(f_inv, t.reshape(2 * n1, n2, d), x0.reshape(batch * lh, n2, d),
                    z.reshape(batch * lh, n2, d), skip.astype(F32).reshape(1, d))
    return y.reshape(m, d)


def kernel(x, c, ada_w, ada_b, norm_g, hg_w_in, hg_lower_bounds, hg_norm_g, hg_w_out,
           hy_w_in, hy_b_in, hy_conv_w, hy_conv_b, hy_filt_w_in, hy_filt_w_mid, hy_filt_b,
           hy_filt_freq, hy_filt_w_out, hy_skip, hy_w_out, mlp_w1, mlp_w2, final_g):
    batch, seq, d = x.shape
    depth = ada_w.shape[0]
    n_mixers = 2
    assert d % HEAD_DIM == 0 and seq % max(GLA_CHUNK, FFT_N2) == 0

    mod = adaln_all(c, ada_w, ada_b)
    mod = mod.reshape(2 * depth, batch, 3, 1, d)

    lbs = jax.nn.softmax(hg_lower_bounds.astype(F32), axis=1)
    lbs = jnp.cumsum(lbs, axis=1) - lbs[:, :1]

    tables = _dft_tables(seq)

    xf = x.reshape(batch * seq, d)
    for i in range(depth):
        j = i // n_mixers
        shift, scale, gate = mod[2 * i, :, 0], mod[2 * i, :, 1], mod[2 * i, :, 2]
        if i % n_mixers == 0:
            q, v, gt, dec = hgrn_projection(xf, norm_g[i, 0], scale, shift, hg_w_in[j].astype(BF16),
                                            lbs[:, j], seq)
            o_f = gla_direction(q, v, gt, dec, hg_norm_g[j], None, batch, seq, rev=False)
            mix = gla_direction(q, v, gt, dec, hg_norm_g[j], o_f, batch, seq, rev=True)
            w_out = hg_w_out[j]
        else:
            x0, z = hyena_projection(xf, norm_g[i, 0], scale, shift, hy_w_in[j].astype(BF16),
                                     hy_b_in[j], hy_conv_w[j], hy_conv_b[j], seq)
            filt = hyena_filter_circular(seq, d, hy_filt_w_in[j], hy_filt_w_mid[j], hy_filt_b[j],
                                         hy_filt_freq[j], hy_filt_w_out[j])
            mix = hyena_long_conv(x0, z, filt, hy_skip[j], tables, batch, seq)
            w_out = hy_w_out[j]
        gate_mix = gate
        shift, scale, gate = mod[2 * i + 1, :, 0], mod[2 * i + 1, :, 1], mod[2 * i + 1, :, 2]
        xf = mixer_out_mlp(mix, w_out.astype(BF16), xf, gate_mix, norm_g[i, 1], scale, shift, gate,
                           mlp_w1[i].astype(BF16), mlp_w2[i].astype(BF16), final_g, seq,
                           final_norm=(i == depth - 1))
    return xf.reshape(batch, seq, d)
```

```python
import functools
import math

import jax
import jax.numpy as jnp
from jax import lax
from jax.experimental import pallas as pl
from jax.experimental.pallas import tpu as pltpu

F32 = jnp.float32
BF16 = jnp.bfloat16

NORM_EPS = 1e-6
HEAD_DIM = 128
GLA_CHUNK = 64
GLA_SUB = 8
SUBLANES = 8
BF16_ROWS = 16
HGRN_PROJ_SLAB = 256
FFT_N2 = 128
HY_FILTER_PAD = 64
HY_MAX_DECAY = math.log(1e-2) / 0.3
HY_MIN_DECAY = math.log(1e-2) / 1.5
VMEM_LIMIT = 56 * 1024 * 1024


def _params(*sem):
    return pltpu.CompilerParams(dimension_semantics=sem, vmem_limit_bytes=VMEM_LIMIT)


def _tile(n, target, unit):
    t = min(n, target) // unit * unit
    while n % t:
        t -= unit
    return t


def _rms_mod(x, g, scale, shift):
    ms = jnp.mean(x * x, axis=-1, keepdims=True)
    y = x * lax.rsqrt(ms + NORM_EPS) * g
    return y * (1.0 + scale) + shift


def _adaln_kernel(s_ref, w_ref, b_ref, o_ref):
    s = s_ref[...]
    s = s * jax.nn.sigmoid(s)
    o_ref[...] = jnp.dot(s, w_ref[...], preferred_element_type=F32,
                         precision=lax.Precision.HIGHEST) + b_ref[...]


def adaln_all(c, ada_w, ada_b):
    depth, two, d, d3 = ada_w.shape
    n = depth * two
    b = c.shape[0]
    rows = -(-b // SUBLANES) * SUBLANES
    cp = jnp.zeros((rows, d), F32).at[:b].set(c)
    tn = _tile(d3, 1536, 128)
    out = pl.pallas_call(
        _adaln_kernel,
        grid=(n, d3 // tn),
        in_specs=[
            pl.BlockSpec((rows, d), lambda i, j: (0, 0)),
            pl.BlockSpec((None, d, tn), lambda i, j: (i, 0, j)),
            pl.BlockSpec((None, 1, tn), lambda i, j: (i, 0, j)),
        ],
        out_specs=pl.BlockSpec((None, rows, tn), lambda i, j: (i, 0, j)),
        out_shape=jax.ShapeDtypeStruct((n, rows, d3), F32),
        compiler_params=_params("parallel", "parallel"),
        name="adaln",
    )(cp, ada_w.reshape(n, d, d3), ada_b.reshape(n, 1, d3))
    return out[:, :b]


def _resident(shape):
    return pl.BlockSpec(shape, lambda i: (0,) * len(shape), pipeline_mode=pl.Buffered(1))


def _out_mlp_kernel(a_ref, wo_ref, x_ref, gm_ref, g_ref, sc_ref, sh_ref, gate_ref, w1_ref, w2_ref,
                    fg_ref, o_ref, hn_ref, acc_ref, *, final_norm, tf):
    o_ref[...] = x_ref[...] + gm_ref[...] * jnp.dot(a_ref[...], wo_ref[...],
                                                     preferred_element_type=F32)
    hn_ref[...] = _rms_mod(o_ref[...], g_ref[...], sc_ref[...], sh_ref[...]).astype(BF16)
    for k in range(w1_ref.shape[1] // tf):
        cols = slice(k * tf, (k + 1) * tf)
        h = jnp.dot(hn_ref[...], w1_ref[:, cols], preferred_element_type=F32)
        h = jnp.square(jnp.maximum(h, 0.0)).astype(BF16)
        part = jnp.dot(h, w2_ref[cols, :], preferred_element_type=F32)
        if k == 0:
            acc_ref[...] = part
        else:
            acc_ref[...] += part
    out = o_ref[...] + gate_ref[...] * acc_ref[...]
    if final_norm:
        ms = jnp.mean(out * out, axis=-1, keepdims=True)
        out = out * lax.rsqrt(ms + NORM_EPS) * fg_ref[...]
    o_ref[...] = out


def mixer_out_mlp(a, w_out, x, gate_mix, g, scale, shift, gate, w1, w2, final_g, seq, final_norm):
    m, d = x.shape
    dff = w1.shape[1]
    tm = min(512, seq)
    tf = _tile(dff, 1024, 128)
    per_seq = seq // tm
    vec = pl.BlockSpec((None, 1, d), lambda i: (i // per_seq, 0, 0))
    return pl.pallas_call(
        functools.partial(_out_mlp_kernel, final_norm=final_norm, tf=tf),
        grid=(m // tm,),
        in_specs=[
            pl.BlockSpec((tm, d), lambda i: (i, 0)),
            _resident((d, d)),
            pl.BlockSpec((tm, d), lambda i: (i, 0)),
            vec,
            _resident((1, d)),
            vec, vec, vec,
            _resident((d, dff)),
            _resident((dff, d)),
            _resident((1, d)),
        ],
        out_specs=pl.BlockSpec((tm, d), lambda i: (i, 0)),
        out_shape=jax.ShapeDtypeStruct((m, d), F32),
        scratch_shapes=[pltpu.VMEM((tm, d), BF16), pltpu.VMEM((tm, d), F32)],
        compiler_params=_params("parallel"),
        name="mixer_out_mlp",
    )(a, w_out, x, gate_mix, g.reshape(1, d), scale, shift, gate, w1, w2, final_g.reshape(1, d))


def _hgrn_proj_kernel(x_ref, g_ref, sc_ref, sh_ref, w_ref, lb_ref, q_ref, v_ref, gt_ref, dec_ref,
                      hn_ref):
    d = q_ref.shape[1]
    hn_ref[...] = _rms_mod(x_ref[...], g_ref[...], sc_ref[...], sh_ref[...]).astype(BF16)

    def silu(a):
        half = 0.5 * a
        return half + half * jnp.tanh(half)

    slab = min(d, HGRN_PROJ_SLAB)
    for c0 in range(0, d, slab):
        cols = slice(c0, c0 + slab)

        def proj(group):
            return jnp.dot(hn_ref[...], w_ref[:, group * d + c0:group * d + c0 + slab],
                           preferred_element_type=F32)

        q_ref[:, cols] = silu(proj(0)).astype(BF16)
        v_ref[:, cols] = proj(1).astype(BF16)
        for dirn in range(2):
            z = proj(2 + dirn)
            lb = lb_ref[dirn:dirn + 1, cols]
            one_minus_lb = 1.0 - lb
            dec_ref[:, 2 * dirn * d + c0:2 * dirn * d + c0 + slab] = jnp.log2(
                lb + one_minus_lb * (1.0 / (1.0 + jnp.exp(-z))))
            dec_ref[:, (2 * dirn + 1) * d + c0:(2 * dirn + 1) * d + c0 + slab] = (
                jnp.log2(one_minus_lb) - jnp.log2(1.0 + jnp.exp(z)))
        gt_ref[:, cols] = silu(proj(4)).astype(BF16)


def hgrn_projection(x, g, scale, shift, w, lbs, seq):
    m, d = x.shape
    tm = min(512, seq)
    per_seq = seq // tm
    row = lambda width: pl.BlockSpec((tm, width), lambda i: (i, 0))
    return pl.pallas_call(
        _hgrn_proj_kernel,
        grid=(m // tm,),
        in_specs=[
            row(d),
            _resident((1, d)),
            pl.BlockSpec((None, 1, d), lambda i: (i // per_seq, 0, 0)),
            pl.BlockSpec((None, 1, d), lambda i: (i // per_seq, 0, 0)),
            _resident((d, 5 * d)),
            _resident((2, d)),
        ],
        out_specs=[row(d), row(d), row(d), row(4 * d)],
        out_shape=[jax.ShapeDtypeStruct((m, d), BF16), jax.ShapeDtypeStruct((m, d), BF16),
                   jax.ShapeDtypeStruct((m, d), BF16), jax.ShapeDtypeStruct((m, 4 * d), F32)],
        scratch_shapes=[pltpu.VMEM((tm, d), BF16)],
        compiler_params=_params("parallel"),
        name="hgrn_projection",
    )(x, g.reshape(1, d), scale, shift, w, lbs)


def _gla_kernel(*refs, rev, n_chunks):
    if rev:
        (q_ref, v_ref, g2_ref, lk2_ref, tri_ref, e_ref, of_ref, gt_ref, ng_ref, o_ref,
         st_s, b2_s, gam_s, qe_s, dec_s, upd_s, seen_s, att_s, p_s, dg_s) = refs
    else:
        (q_ref, v_ref, g2_ref, lk2_ref, tri_ref, e_ref, o_ref,
         st_s, b2_s, gam_s, qe_s, dec_s, upd_s, seen_s, att_s, p_s, dg_s) = refs
    c_len, sub, pack = GLA_CHUNK, GLA_SUB, BF16_ROWS
    n_sub = c_len // sub
    nt = (((1,), (1,)), ((), ()))
    tn = (((0,), (0,)), ((), ()))

    @pl.when(pl.program_id(2) == 0)
    def _():
        st_s[...] = jnp.zeros_like(st_s)

    tri2 = tri_ref[...]

    def sums(ci):
        rows = slice(ci * c_len, (ci + 1) * c_len)
        g2 = g2_ref[rows, :]
        hi = g2.astype(BF16)
        lo = (g2 - hi.astype(F32)).astype(BF16)
        b2 = jnp.dot(tri2, jnp.concatenate([hi, lo], axis=0), preferred_element_type=F32)
        b2_s[rows, :] = b2
        gam_s[rows, :] = lk2_ref[rows, :] - b2

    def padded_rows(x, lo, hi, total):
        lo_t, hi_t = lo // pack * pack, -(-hi // pack) * pack
        parts = ([jnp.zeros((lo - lo_t, HEAD_DIM), F32)] if lo > lo_t else []) + [x]
        if hi_t > hi:
            parts.append(jnp.zeros((hi_t - hi, HEAD_DIM), F32))
        tiles = [jnp.concatenate(parts, axis=0).astype(BF16)]
        if lo_t:
            tiles.insert(0, jnp.zeros((lo_t, HEAD_DIM), BF16))
        if total > hi_t:
            tiles.append(jnp.zeros((total - hi_t, HEAD_DIM), BF16))
        return jnp.concatenate(tiles, axis=0)

    def scores(ci):
        off = ci * c_len
        rows = slice(off, off + c_len)
        q = q_ref[rows, :].astype(F32)
        b2 = b2_s[rows, :]
        gam = gam_s[rows, :]
        b2_end = b2[0:1] if rev else b2[c_len - 1:c_len]
        qe_s[rows, :] = (q * jnp.exp2(b2)).astype(BF16)
        upd_s[ci] = lax.dot_general(v_ref[rows, :], jnp.exp2(b2_end + gam).astype(BF16), tn,
                                    preferred_element_type=F32)
        dec_s[ci] = jnp.exp2(b2_end)
        atts = []
        for i in range(n_sub):
            r0 = sub * i
            q_i, b_i = q[r0:r0 + sub], b2[r0:r0 + sub]
            if rev:
                others = (r0 + sub, c_len)
                edge = b2[r0 + sub:r0 + sub + 1] if i < n_sub - 1 else None
            else:
                others = (0, r0)
                edge = b2[r0 - 1:r0] if i > 0 else None
            if edge is None:
                atts.append(jnp.zeros((sub, HEAD_DIM), F32))
            else:
                qt = padded_rows(q_i * jnp.exp2(b_i - edge), 0, sub, pack)
                kt = padded_rows(jnp.exp2(edge + gam[others[0]:others[1]]), others[0], others[1],
                                 HEAD_DIM)
                atts.append(lax.dot_general(qt, kt, nt, preferred_element_type=F32)[:sub])
        att_s[rows, :] = jnp.concatenate(atts, axis=0)
        for r0 in range(0, c_len, pack):
            q_t, b_t = q[r0:r0 + pack], b2[r0:r0 + pack]
            for sl in range(sub):
                row0 = off + r0 + sl
                g_rows = jnp.concatenate(
                    [jnp.broadcast_to(gam_s[row0 + u * sub:row0 + u * sub + 1, :], (sub, HEAD_DIM))
                     for u in range(pack // sub)], axis=0)
                tile = q_t * jnp.exp2(jnp.minimum(b_t + g_rows, 0.0))
                p_s[off + r0:off + r0 + pack, sl * HEAD_DIM:(sl + 1) * HEAD_DIM] = tile.astype(BF16)

    row = lax.broadcasted_iota(jnp.int32, (c_len, HEAD_DIM), 0)
    col = lax.broadcasted_iota(jnp.int32, (c_len, HEAD_DIM), 1)
    keep = (col >= row) if rev else (col <= row)
    v_pad = jnp.zeros((HEAD_DIM - c_len, HEAD_DIM), BF16)

    def outputs(ci):
        rows = slice(ci * c_len, (ci + 1) * c_len)
        dg = dg_s[rows, :]
        diag = [dg[0:sub]] + [pltpu.roll(dg[sub * i:sub * (i + 1)], sub * i, 1)
                              for i in range(1, n_sub)]
        att = att_s[rows, :] + jnp.concatenate(diag, axis=0)
        att = jnp.where(keep, att, 0.0).astype(BF16)
        o = (jnp.dot(att, jnp.concatenate([v_ref[rows, :], v_pad], axis=0),
                     preferred_element_type=F32)
             + lax.dot_general(qe_s[rows, :], seen_s[ci], nt, preferred_element_type=F32))
        if rev:
            tot = of_ref[rows, :] + o
            ms = jnp.mean(tot * tot, axis=-1, keepdims=True)
            y = tot * lax.rsqrt(ms + NORM_EPS) * ng_ref[...] * gt_ref[rows, :].astype(F32)
            o_ref[rows, :] = y.astype(o_ref.dtype)
        else:
            o_ref[rows, :] = o

    for ci in range(n_chunks):
        sums(ci)
    for ci in range(n_chunks):
        scores(ci)
    dg_s[...] = jnp.dot(p_s[...], e_ref[...], preferred_element_type=F32)
    st = st_s[...]
    for ci in (reversed(range(n_chunks)) if rev else range(n_chunks)):
        seen_s[ci] = st.astype(BF16)
        st = st * dec_s[ci] + upd_s[ci]
    st_s[...] = st
    for ci in range(n_chunks):
        outputs(ci)


def _gla_tables(rev):
    idx = jnp.arange(GLA_CHUNK)
    tri = (idx[:, None] <= idx[None, :]) if rev else (idx[:, None] >= idx[None, :])
    tri2 = jnp.concatenate([tri, tri], axis=1).astype(BF16)
    s_of_row = jnp.arange(GLA_SUB * HEAD_DIM) // HEAD_DIM
    place = jnp.arange(HEAD_DIM)[None, :] == s_of_row[:, None]
    return tri2, place.astype(BF16)


def gla_direction(q, v, gate, dec, norm_g, o_fwd, batch, seq, rev):
    m, d = q.shape
    heads = d // HEAD_DIM
    t_blk = min(1024, seq)
    n_t = seq // t_blk
    n_chunks = t_blk // GLA_CHUNK
    tri2, place = _gla_tables(rev)

    def rowblk(b, h, t):
        return b * n_t + ((n_t - 1 - t) if rev else t)

    def col(group):
        return pl.BlockSpec((t_blk, HEAD_DIM), lambda b, h, t: (rowblk(b, h, t), group * heads + h))

    in_specs = [col(0), col(0), col(2 if rev else 0), col(3 if rev else 1),
                pl.BlockSpec(tri2.shape, lambda b, h, t: (0, 0)),
                pl.BlockSpec(place.shape, lambda b, h, t: (0, 0))]
    args = [q, v, dec, dec, tri2, place]
    if rev:
        in_specs += [col(0), col(0), pl.BlockSpec((1, HEAD_DIM), lambda b, h, t: (0, 0))]
        args += [o_fwd, gate, norm_g.reshape(1, HEAD_DIM)]
    return pl.pallas_call(
        functools.partial(_gla_kernel, rev=rev, n_chunks=n_chunks),
        grid=(batch, heads, n_t),
        in_specs=in_specs,
        out_specs=col(0),
        out_shape=jax.ShapeDtypeStruct((m, d), BF16 if rev else F32),
        scratch_shapes=[pltpu.VMEM((HEAD_DIM, HEAD_DIM), F32),
                        pltpu.VMEM((t_blk, HEAD_DIM), F32),
                        pltpu.VMEM((t_blk, HEAD_DIM), F32),
                        pltpu.VMEM((t_blk, HEAD_DIM), BF16),
                        pltpu.VMEM((n_chunks, 1, HEAD_DIM), F32),
                        pltpu.VMEM((n_chunks, HEAD_DIM, HEAD_DIM), F32),
                        pltpu.VMEM((n_chunks, HEAD_DIM, HEAD_DIM), BF16),
                        pltpu.VMEM((t_blk, HEAD_DIM), F32),
                        pltpu.VMEM((t_blk, GLA_SUB * HEAD_DIM), BF16),
                        pltpu.VMEM((t_blk, HEAD_DIM), F32)],
        compiler_params=_params("parallel", "parallel", "arbitrary"),
        name="gla_rev" if rev else "gla_fwd",
    )(*args)


def _hyena_proj_kernel(x_ref, xp_ref, xn_ref, g_ref, sc_ref, sh_ref, w_ref, b_ref, cw_ref, cb_ref,
                       x0_ref, z_ref, hn_ref, u_ref, *, per_seq, cs):
    i = pl.program_id(0)
    tm, d = x_ref.shape
    halo = xp_ref.shape[0]
    keep_prev = jnp.where(i % per_seq == 0, 0.0, 1.0)
    keep_next = jnp.where(i % per_seq == per_seq - 1, 0.0, 1.0)

    def norm(ref):
        return _rms_mod(ref[...], g_ref[...], sc_ref[...], sh_ref[...]).astype(BF16)

    hn_ref[0:halo, :] = norm(xp_ref)
    hn_ref[halo:halo + tm, :] = norm(x_ref)
    hn_ref[halo + tm:, :] = norm(xn_ref)
    row = lax.broadcasted_iota(jnp.int32, (tm + 2 * halo, cs), 0)
    keep = jnp.where(row < halo, keep_prev, jnp.where(row >= halo + tm, keep_next, 1.0))

    for c0 in range(0, d, cs):
        def conv(group):
            cols = slice(group * d + c0, group * d + c0 + cs)
            u = jnp.dot(hn_ref[...], w_ref[:, cols], preferred_element_type=F32) + b_ref[:, cols]
            u_ref[group] = u * keep
            return (u_ref[group, halo - 1:halo - 1 + tm, :] * cw_ref[0:1, cols]
                    + u_ref[group, halo:halo + tm, :] * cw_ref[1:2, cols]
                    + u_ref[group, halo + 1:halo + 1 + tm, :] * cw_ref[2:3, cols]
                    + cb_ref[:, cols])

        x0_ref[:, c0:c0 + cs] = conv(0).astype(x0_ref.dtype)
        z_ref[:, c0:c0 + cs] = (conv(1) * conv(2)).astype(z_ref.dtype)


def hyena_projection(x, g, scale, shift, w, bias, conv_w, conv_b, seq):
    m, d = x.shape
    tm = min(512, seq)
    per_seq = seq // tm
    halo = BF16_ROWS
    per_tile = tm // halo
    n_halo = m // halo
    cs = _tile(d, 512, 128)
    vec = pl.BlockSpec((None, 1, d), lambda i: (i // per_seq, 0, 0))
    return pl.pallas_call(
        functools.partial(_hyena_proj_kernel, per_seq=per_seq, cs=cs),
        grid=(m // tm,),
        in_specs=[
            pl.BlockSpec((tm, d), lambda i: (i, 0)),
            pl.BlockSpec((halo, d), lambda i: (jnp.maximum(i * per_tile - 1, 0), 0)),
            pl.BlockSpec((halo, d), lambda i: (jnp.minimum((i + 1) * per_tile, n_halo - 1), 0)),
            _resident((1, d)),
            vec, vec,
            _resident((d, 3 * d)),
            _resident((1, 3 * d)),
            _resident((3, 3 * d)),
            _resident((1, 3 * d)),
        ],
        out_specs=[pl.BlockSpec((tm, d), lambda i: (i, 0)),
                   pl.BlockSpec((tm, d), lambda i: (i, 0))],
        out_shape=[jax.ShapeDtypeStruct((m, d), BF16), jax.ShapeDtypeStruct((m, d), BF16)],
        scratch_shapes=[pltpu.VMEM((tm + 2 * halo, d), BF16),
                        pltpu.VMEM((3, tm + 2 * halo, cs), F32)],
        compiler_params=_params("parallel"),
        name="hyena_projection",
    )(x, x, x, g.reshape(1, d), scale, shift, w, bias.reshape(1, 3 * d), conv_w,
      conv_b.reshape(1, 3 * d))


def _filter_kernel(emb_ref, t_ref, keep_ref, win_ref, wmid_ref, b_ref, freq_ref, wout_ref,
                   delta_ref, o_ref):
    def split(a):
        hi = a.astype(BF16)
        return hi, (a - hi.astype(F32)).astype(BF16)

    def dot3(a, w, dims):
        a_hi, a_lo = split(a)
        w_hi, w_lo = split(w)
        mm = functools.partial(lax.dot_general, dimension_numbers=dims, preferred_element_type=F32)
        return mm(a_hi, w_hi) + mm(a_hi, w_lo) + mm(a_lo, w_hi)

    nn = (((1,), (0,)), ((), ()))
    tn = (((0,), (0,)), ((), ()))
    freq = freq_ref[...]
    h = jnp.sin(freq * (dot3(win_ref[...], emb_ref[...], nn) + b_ref[:, 0:1]))
    for mth in range(wmid_ref.shape[0]):
        h = jnp.sin(freq * (dot3(wmid_ref[mth], h, nn) + b_ref[:, mth + 1:mth + 2]))
    hf = dot3(h, wout_ref[...], tn)
    o_ref[...] = hf * jnp.exp(-t_ref[...] * delta_ref[...]) * keep_ref[...]


def hyena_filter_circular(seq, d, w_in, w_mid, b, freq, w_out):
    emb_dim, width = w_in.shape
    bands = (emb_dim - 1) // 2
    mrow = jnp.arange(2 * seq)
    pos = jnp.where(mrow < seq, mrow, jnp.where(mrow == seq, 0, 2 * seq - mrow)).astype(F32)[:, None]
    t_c = pos / (seq - 1.0)
    band = jnp.linspace(1e-4, bands - 1.0, bands, dtype=F32)
    ang = (2.0 * math.pi / seq) * pos * band
    emb_c = jnp.concatenate([t_c, jnp.cos(ang), -jnp.sin(ang)], axis=-1)
    emb_c = jnp.pad(emb_c, ((0, 0), (0, HY_FILTER_PAD - emb_dim)))
    keep = (mrow != seq).astype(F32)[:, None]
    w_in_p = jnp.pad(w_in.astype(F32), ((0, HY_FILTER_PAD - emb_dim), (0, 0)))
    deltas = jnp.abs(jnp.linspace(HY_MIN_DECAY, HY_MAX_DECAY, d, dtype=F32)).reshape(1, d)
    tr = min(512, seq)
    half = seq // tr
    n_mid = w_mid.shape[0]
    return pl.pallas_call(
        _filter_kernel,
        grid=(2 * seq // tr,),
        in_specs=[
            pl.BlockSpec((HY_FILTER_PAD, tr), lambda i: (0, i)),
            pl.BlockSpec((tr, 1), lambda i: (i, 0)),
            pl.BlockSpec((tr, 1), lambda i: (i, 0)),
            pl.BlockSpec((width, HY_FILTER_PAD), lambda i: (0, 0)),
            pl.BlockSpec((n_mid, width, width), lambda i: (0, 0, 0)),
            pl.BlockSpec((width, n_mid + 1), lambda i: (0, 0)),
            pl.BlockSpec((width, 1), lambda i: (0, 0)),
            pl.BlockSpec((width, d), lambda i: (0, i // half)),
            pl.BlockSpec((1, d), lambda i: (0, 0)),
        ],
        out_specs=pl.BlockSpec((tr, d), lambda i: (i, 0)),
        out_shape=jax.ShapeDtypeStruct((2 * seq, d), F32),
        compiler_params=_params("parallel"),
        name="hyena_filter",
    )(emb_c.T, t_c, keep, w_in_p.T, jnp.swapaxes(w_mid.astype(F32), 1, 2), b.astype(F32).T,
      freq.astype(F32).reshape(width, 1), w_out.astype(F32), deltas)


def _swap_major_sublane(x):
    return jnp.swapaxes(x, 0, 1)


def _dft_stage1_kernel(m_ref, x_ref, o_ref):
    x32 = x_ref[...].astype(F32)
    outs = []
    for g in range(x32.shape[1] // SUBLANES):
        xt = _swap_major_sublane(x32[:, g * SUBLANES:(g + 1) * SUBLANES, :]).astype(BF16)
        r = jnp.stack([jnp.dot(m_ref[...], xt[j], preferred_element_type=F32)
                       for j in range(SUBLANES)], axis=0)
        outs.append(_swap_major_sublane(r))
    o_ref[...] = jnp.concatenate(outs, axis=1).astype(o_ref.dtype)


def dft_stage1(mat, x):
    r, kdim = mat.shape
    _, n2, d = x.shape
    cw = _tile(d, 512, 128)
    return pl.pallas_call(
        _dft_stage1_kernel,
        grid=(n2 // BF16_ROWS, d // cw),
        in_specs=[pl.BlockSpec((r, kdim), lambda j, c: (0, 0)),
                  pl.BlockSpec((kdim, BF16_ROWS, cw), lambda j, c: (0, j, c))],
        out_specs=pl.BlockSpec((r, BF16_ROWS, cw), lambda j, c: (0, j, c)),
        out_shape=jax.ShapeDtypeStruct((r, n2, d), BF16),
        compiler_params=_params("parallel", "parallel"),
        name="dft_stage1",
    )(mat, x)


def _spectrum_kernel(g_ref, a_ref, o_ref):
    n2 = a_ref.shape[1]
    a = a_ref[...].reshape(2 * n2, a_ref.shape[2])
    o_ref[...] = jnp.dot(g_ref[...], a, preferred_element_type=F32).astype(o_ref.dtype).reshape(
        o_ref.shape)


def filter_spectrum(g_tab, a):
    _, n1, n2, d = a.shape
    return pl.pallas_call(
        _spectrum_kernel,
        grid=(n1,),
        in_specs=[pl.BlockSpec((None, 2 * n2, 2 * n2), lambda k: (k, 0, 0)),
                  pl.BlockSpec((2, None, n2, d), lambda k: (0, k, 0, 0))],
        out_specs=pl.BlockSpec((None, 2, n2, d), lambda k: (k, 0, 0, 0)),
        out_shape=jax.ShapeDtypeStruct((n1, 2, n2, d), BF16),
        compiler_params=_params("parallel"),
        name="filter_spectrum",
    )(g_tab, a)


def _freq_mul_kernel(g_ref, gi_ref, a_ref, h_ref, o_ref):
    n2 = a_ref.shape[1]
    d = a_ref.shape[2]
    a = a_ref[...].reshape(2 * n2, d)
    x = jnp.dot(g_ref[...], a, preferred_element_type=F32)
    xr, xi = x[:n2], x[n2:]
    hr, hi = h_ref[0].astype(F32), h_ref[1].astype(F32)
    p = jnp.concatenate([xr * hr - xi * hi, xr * hi + xi * hr], axis=0).astype(BF16)
    o_ref[...] = jnp.dot(gi_ref[...], p, preferred_element_type=F32).astype(o_ref.dtype).reshape(o_ref.shape)


def freq_multiply(g_tab, gi_tab, a, h):
    _, n1, n2, d = a.shape
    return pl.pallas_call(
        _freq_mul_kernel,
        grid=(n1,),
        in_specs=[pl.BlockSpec((None, 2 * n2, 2 * n2), lambda k: (k, 0, 0)),
                  pl.BlockSpec((None, 2 * n2, 2 * n2), lambda k: (k, 0, 0)),
                  pl.BlockSpec((2, None, n2, d), lambda k: (0, k, 0, 0)),
                  pl.BlockSpec((None, 2, n2, d), lambda k: (k, 0, 0, 0))],
        out_specs=pl.BlockSpec((2, None, n2, d), lambda k: (0, k, 0, 0)),
        out_shape=jax.ShapeDtypeStruct((2, n1, n2, d), BF16),
        compiler_params=_params("parallel"),
        name="freq_multiply",
    )(g_tab, gi_tab, a, h)


def _conv_out_kernel(m_ref, t_ref, x0_ref, z_ref, skip_ref, o_ref):
    t32 = t_ref[...].astype(F32)
    ys = []
    for g in range(t32.shape[1] // SUBLANES):
        tt = _swap_major_sublane(t32[:, g * SUBLANES:(g + 1) * SUBLANES, :]).astype(BF16)
        y = jnp.stack([jnp.dot(m_ref[...], tt[j], preferred_element_type=F32)
                       for j in range(SUBLANES)], axis=0)
        ys.append(_swap_major_sublane(y))
    y = jnp.concatenate(ys, axis=1)
    o_ref[...] = (x0_ref[...].astype(F32)
                  * (y + skip_ref[...] * z_ref[...].astype(F32))).astype(o_ref.dtype)


def conv_output(mat, t, x0, z, skip):
    r, kdim = mat.shape
    _, n2, d = t.shape
    cw = _tile(d, 256, 128)
    blk = lambda rows: pl.BlockSpec((rows, BF16_ROWS, cw), lambda j, c: (0, j, c))
    return pl.pallas_call(
        _conv_out_kernel,
        grid=(n2 // BF16_ROWS, d // cw),
        in_specs=[pl.BlockSpec((r, kdim), lambda j, c: (0, 0)),
                  blk(kdim), blk(r), blk(r),
                  pl.BlockSpec((1, cw), lambda j, c: (0, c))],
        out_specs=blk(r),
        out_shape=jax.ShapeDtypeStruct((r, n2, d), BF16),
        compiler_params=_params("parallel", "parallel"),
        name="dft_inverse_out",
    )(mat, t, x0, z, skip)


def _dft_tables(seq):
    n = 2 * seq
    n2 = FFT_N2
    n1 = n // n2
    lh = seq // n2

    def cs(idx, mod):
        ang = (2.0 * math.pi / mod) * (idx % mod).astype(F32)
        return jnp.cos(ang), jnp.sin(ang)

    k1 = jnp.arange(n1)
    cr, sr = cs(k1[:, None] * jnp.arange(lh)[None, :], n1)
    f_sig = jnp.block([[cr, sr], [-sr, cr]])
    cf, sf = cs(k1[:, None] * k1[None, :], n1)
    f_filt = jnp.concatenate([cf, -sf], axis=0)
    idx = jnp.arange(n2)[None, None, :] * (k1[:, None, None] + n1 * jnp.arange(n2)[None, :, None])
    gr, gs = cs(idx, n)
    g_tab = jnp.concatenate([jnp.concatenate([gr, gs], axis=2),
                             jnp.concatenate([-gs, gr], axis=2)], axis=1)
    gi_tab = jnp.swapaxes(g_tab, 1, 2)
    er, es = cs(jnp.arange(lh)[:, None] * k1[None, :], n1)
    f_inv = jnp.block([[er, -es], [es, er]]) / n
    return (f_sig.astype(BF16), f_filt.astype(BF16), g_tab.astype(BF16), gi_tab.astype(BF16),
            f_inv.astype(BF16))


def hyena_long_conv(x0, z, filt, skip, tables, batch, seq):
    assert batch == 2, "the two sequences ride as real / imaginary parts of one complex signal"
    f_sig, f_filt, g_tab, gi_tab, f_inv = tables
    m, d = z.shape
    n2 = FFT_N2
    n1 = 2 * seq // n2
    lh = seq // n2
    a_f = dft_stage1(f_filt, filt.reshape(n1, n2, d))
    h = filter_spectrum(g_tab, a_f.reshape(2, n1, n2, d))
    a = dft_stage1(f_sig, z.reshape(batch * lh, n2, d))
    t = freq_multiply(g_tab, gi_tab, a.reshape(2, n1, n2, d), h)
    y = conv_output(f_inv, t.reshape(2 * n1, n2, d), x0.reshape(batch * lh, n2, d),
                    z.reshape(batch * lh, n2, d), skip.astype(F32).reshape(1, d))
    return y.reshape(m, d)


def kernel(x, c, ada_w, ada_b, norm_g, hg_w_in, hg_lower_bounds, hg_norm_g, hg_w_out,
           hy_w_in, hy_b_in, hy_conv_w, hy_conv_b, hy_filt_w_in, hy_filt_w_mid, hy_filt_b,
           hy_filt_freq, hy_filt_w_out, hy_skip, hy_w_out, mlp_w1, mlp_w2, final_g):
    batch, seq, d = x.shape
    depth = ada_w.shape[0]
    n_mixers = 2
    assert d % HEAD_DIM == 0 and seq % max(GLA_CHUNK, FFT_N2) == 0

    mod = adaln_all(c, ada_w, ada_b)
    mod = mod.reshape(2 * depth, batch, 3, 1, d)

    lbs = jax.nn.softmax(hg_lower_bounds.astype(F32), axis=1)
    lbs = jnp.cumsum(lbs, axis=1) - lbs[:, :1]

    tables = _dft_tables(seq)

    xf = x.reshape(batch * seq, d)
    for i in range(depth):
        j = i // n_mixers
        shift, scale, gate = mod[2 * i, :, 0], mod[2 * i, :, 1], mod[2 * i, :, 2]
        if i % n_mixers == 0:
            q, v, gt, dec = hgrn_projection(xf, norm_g[i, 0], scale, shift, hg_w_in[j].astype(BF16),
                                            lbs[:, j], seq)
            o_f = gla_direction(q, v, gt, dec, hg_norm_g[j], None, batch, seq, rev=False)
            mix = gla_direction(q, v, gt, dec, hg_norm_g[j], o_f, batch, seq, rev=True)
            w_out = hg_w_out[j]
        else:
            x0, z = hyena_projection(xf, norm_g[i, 0], scale, shift, hy_w_in[j].astype(BF16),
                                     hy_b_in[j], hy_conv_w[j], hy_conv_b[j], seq)
            filt = hyena_filter_circular(seq, d, hy_filt_w_in[j], hy_filt_w_mid[j], hy_filt_b[j],
                                         hy_filt_freq[j], hy_filt_w_out[j])
            mix = hyena_long_conv(x0, z, filt, hy_skip[j], tables, batch, seq)
            w_out = hy_w_out[j]
        gate_mix = gate
        shift, scale, gate = mod[2 * i + 1, :, 0], mod[2 * i + 1, :, 1], mod[2 * i + 1, :, 2]
        xf = mixer_out_mlp(mix, w_out.astype(BF16), xf, gate_mix, norm_g[i, 1], scale, shift, gate,
                           mlp_w1[i].astype(BF16), mlp_w2[i].astype(BF16), final_g, seq,
                           final_norm=(i == depth - 1))
    return xf.reshape(batch, seq, d)
```

```python
import functools
import math

import jax
import jax.numpy as jnp
from jax import lax
from jax.experimental import pallas as pl
from jax.experimental.pallas import tpu as pltpu

F32 = jnp.float32
BF16 = jnp.bfloat16

NORM_EPS = 1e-6
HEAD_DIM = 128
GLA_CHUNK = 64
GLA_SUB = 8
GLA_BLOCK = 2048
SUBLANES = 8
BF16_ROWS = 16
HGRN_PROJ_SLAB = 256
FFT_N2 = 128
DFT_K1_BLOCK = 4
HY_FILTER_PAD = 64
HY_MAX_DECAY = math.log(1e-2) / 0.3
HY_MIN_DECAY = math.log(1e-2) / 1.5
VMEM_LIMIT = 56 * 1024 * 1024


def _params(*sem):
    return pltpu.CompilerParams(dimension_semantics=sem, vmem_limit_bytes=VMEM_LIMIT)


def _tile(n, target, unit):
    t = min(n, target) // unit * unit
    while n % t:
        t -= unit
    return t


def _rms_mod(x, g, scale, shift):
    ms = jnp.mean(x * x, axis=-1, keepdims=True)
    y = x * lax.rsqrt(ms + NORM_EPS) * g
    return y * (1.0 + scale) + shift


def _adaln_kernel(s_ref, w_ref, b_ref, o_ref):
    s = s_ref[...]
    s = s * jax.nn.sigmoid(s)
    o_ref[...] = jnp.dot(s, w_ref[...], preferred_element_type=F32,
                         precision=lax.Precision.HIGHEST) + b_ref[...]


def adaln_all(c, ada_w, ada_b):
    depth, two, d, d3 = ada_w.shape
    n = depth * two
    b = c.shape[0]
    rows = -(-b // SUBLANES) * SUBLANES
    cp = jnp.zeros((rows, d), F32).at[:b].set(c)
    tn = _tile(d3, 1536, 128)
    out = pl.pallas_call(
        _adaln_kernel,
        grid=(n, d3 // tn),
        in_specs=[
            pl.BlockSpec((rows, d), lambda i, j: (0, 0)),
            pl.BlockSpec((None, d, tn), lambda i, j: (i, 0, j)),
            pl.BlockSpec((None, 1, tn), lambda i, j: (i, 0, j)),
        ],
        out_specs=pl.BlockSpec((None, rows, tn), lambda i, j: (i, 0, j)),
        out_shape=jax.ShapeDtypeStruct((n, rows, d3), F32),
        compiler_params=_params("parallel", "parallel"),
        name="adaln",
    )(cp, ada_w.reshape(n, d, d3), ada_b.reshape(n, 1, d3))
    return out[:, :b]


def _resident(shape):
    return pl.BlockSpec(shape, lambda i: (0,) * len(shape), pipeline_mode=pl.Buffered(1))


def _out_mlp_kernel(a_ref, wo_ref, x_ref, gm_ref, g_ref, sc_ref, sh_ref, gate_ref, w1_ref, w2_ref,
                    fg_ref, o_ref, hn_ref, acc_ref, *, final_norm, tf):
    o_ref[...] = x_ref[...] + gm_ref[...] * jnp.dot(a_ref[...], wo_ref[...],
                                                     preferred_element_type=F32)
    hn_ref[...] = _rms_mod(o_ref[...], g_ref[...], sc_ref[...], sh_ref[...]).astype(BF16)
    for k in range(w1_ref.shape[1] // tf):
        cols = slice(k * tf, (k + 1) * tf)
        h = jnp.dot(hn_ref[...], w1_ref[:, cols], preferred_element_type=F32)
        h = jnp.square(jnp.maximum(h, 0.0)).astype(BF16)
        part = jnp.dot(h, w2_ref[cols, :], preferred_element_type=F32)
        if k == 0:
            acc_ref[...] = part
        else:
            acc_ref[...] += part
    out = o_ref[...] + gate_ref[...] * acc_ref[...]
    if final_norm:
        ms = jnp.mean(out * out, axis=-1, keepdims=True)
        out = out * lax.rsqrt(ms + NORM_EPS) * fg_ref[...]
    o_ref[...] = out


def mixer_out_mlp(a, w_out, x, gate_mix, g, scale, shift, gate, w1, w2, final_g, seq, final_norm):
    m, d = x.shape
    dff = w1.shape[1]
    tm = min(512, seq)
    tf = _tile(dff, 1024, 128)
    per_seq = seq // tm
    vec = pl.BlockSpec((None, 1, d), lambda i: (i // per_seq, 0, 0))
    return pl.pallas_call(
        functools.partial(_out_mlp_kernel, final_norm=final_norm, tf=tf),
        grid=(m // tm,),
        in_specs=[
            pl.BlockSpec((tm, d), lambda i: (i, 0)),
            _resident((d, d)),
            pl.BlockSpec((tm, d), lambda i: (i, 0)),
            vec,
            _resident((1, d)),
            vec, vec, vec,
            _resident((d, dff)),
            _resident((dff, d)),
            _resident((1, d)),
        ],
        out_specs=pl.BlockSpec((tm, d), lambda i: (i, 0)),
        out_shape=jax.ShapeDtypeStruct((m, d), F32),
        scratch_shapes=[pltpu.VMEM((tm, d), BF16), pltpu.VMEM((tm, d), F32)],
        compiler_params=_params("parallel"),
        name="mixer_out_mlp",
    )(a, w_out, x, gate_mix, g.reshape(1, d), scale, shift, gate, w1, w2, final_g.reshape(1, d))


def _hgrn_proj_kernel(x_ref, g_ref, sc_ref, sh_ref, w_ref, lb_ref, q_ref, v_ref, gt_ref, dec_ref,
                      hn_ref):
    d = q_ref.shape[1]
    hn_ref[...] = _rms_mod(x_ref[...], g_ref[...], sc_ref[...], sh_ref[...]).astype(BF16)

    def silu(a):
        half = 0.5 * a
        return half + half * jnp.tanh(half)

    slab = min(d, HGRN_PROJ_SLAB)
    for c0 in range(0, d, slab):
        cols = slice(c0, c0 + slab)

        def proj(group):
            return jnp.dot(hn_ref[...], w_ref[:, group * d + c0:group * d + c0 + slab],
                           preferred_element_type=F32)

        q_ref[:, cols] = silu(proj(0)).astype(BF16)
        v_ref[:, cols] = proj(1).astype(BF16)
        for dirn in range(2):
            z = proj(2 + dirn)
            lb = lb_ref[dirn:dirn + 1, cols]
            one_minus_lb = 1.0 - lb
            dec_ref[:, 2 * dirn * d + c0:2 * dirn * d + c0 + slab] = jnp.log2(
                lb + one_minus_lb * (1.0 / (1.0 + jnp.exp(-z))))
            dec_ref[:, (2 * dirn + 1) * d + c0:(2 * dirn + 1) * d + c0 + slab] = (
                jnp.log2(one_minus_lb) - jnp.log2(1.0 + jnp.exp(z)))
        gt_ref[:, cols] = silu(proj(4)).astype(BF16)


def hgrn_projection(x, g, scale, shift, w, lbs, seq):
    m, d = x.shape
    tm = min(512, seq)
    per_seq = seq // tm
    row = lambda width: pl.BlockSpec((tm, width), lambda i: (i, 0))
    return pl.pallas_call(
        _hgrn_proj_kernel,
        grid=(m // tm,),
        in_specs=[
            row(d),
            _resident((1, d)),
            pl.BlockSpec((None, 1, d), lambda i: (i // per_seq, 0, 0)),
            pl.BlockSpec((None, 1, d), lambda i: (i // per_seq, 0, 0)),
            _resident((d, 5 * d)),
            _resident((2, d)),
        ],
        out_specs=[row(d), row(d), row(d), row(4 * d)],
        out_shape=[jax.ShapeDtypeStruct((m, d), BF16), jax.ShapeDtypeStruct((m, d), BF16),
                   jax.ShapeDtypeStruct((m, d), BF16), jax.ShapeDtypeStruct((m, 4 * d), F32)],
        scratch_shapes=[pltpu.VMEM((tm, d), BF16)],
        compiler_params=_params("parallel"),
        name="hgrn_projection",
    )(x, g.reshape(1, d), scale, shift, w, lbs)


def _gla_kernel(*refs, rev, n_chunks):
    if rev:
        (q_ref, v_ref, g2_ref, lk2_ref, tri_ref, e_ref, of_ref, gt_ref, ng_ref, o_ref,
         st_s, b2_s, gam_s, qe_s, dec_s, upd_s, seen_s, att_s, p_s, dg_s) = refs
    else:
        (q_ref, v_ref, g2_ref, lk2_ref, tri_ref, e_ref, o_ref,
         st_s, b2_s, gam_s, qe_s, dec_s, upd_s, seen_s, att_s, p_s, dg_s) = refs
    c_len, sub, pack = GLA_CHUNK, GLA_SUB, BF16_ROWS
    n_sub = c_len // sub
    nt = (((1,), (1,)), ((), ()))
    tn = (((0,), (0,)), ((), ()))

    @pl.when(pl.program_id(2) == 0)
    def _():
        st_s[...] = jnp.zeros_like(st_s)

    tri2 = tri_ref[...]

    def sums(ci):
        rows = slice(ci * c_len, (ci + 1) * c_len)
        g2 = g2_ref[rows, :]
        hi = g2.astype(BF16)
        lo = (g2 - hi.astype(F32)).astype(BF16)
        b2 = jnp.dot(tri2, jnp.concatenate([hi, lo], axis=0), preferred_element_type=F32)
        b2_s[rows, :] = b2
        gam_s[rows, :] = lk2_ref[rows, :] - b2

    def padded_rows(x, lo, hi, total):
        lo_t, hi_t = lo // pack * pack, -(-hi // pack) * pack
        parts = ([jnp.zeros((lo - lo_t, HEAD_DIM), F32)] if lo > lo_t else []) + [x]
        if hi_t > hi:
            parts.append(jnp.zeros((hi_t - hi, HEAD_DIM), F32))
        tiles = [jnp.concatenate(parts, axis=0).astype(BF16)]
        if lo_t:
            tiles.insert(0, jnp.zeros((lo_t, HEAD_DIM), BF16))
        if total > hi_t:
            tiles.append(jnp.zeros((total - hi_t, HEAD_DIM), BF16))
        return jnp.concatenate(tiles, axis=0)

    def scores(ci):
        off = ci * c_len
        rows = slice(off, off + c_len)
        q = q_ref[rows, :].astype(F32)
        b2 = b2_s[rows, :]
        gam = gam_s[rows, :]
        b2_end = b2[0:1] if rev else b2[c_len - 1:c_len]
        qe_s[rows, :] = (q * jnp.exp2(b2)).astype(BF16)
        upd_s[ci] = lax.dot_general(v_ref[rows, :], jnp.exp2(b2_end + gam).astype(BF16), tn,
                                    preferred_element_type=F32)
        dec_s[ci] = jnp.exp2(b2_end)
        atts = []
        for i in range(n_sub):
            r0 = sub * i
            q_i, b_i = q[r0:r0 + sub], b2[r0:r0 + sub]
            if rev:
                others = (r0 + sub, c_len)
                edge = b2[r0 + sub:r0 + sub + 1] if i < n_sub - 1 else None
            else:
                others = (0, r0)
                edge = b2[r0 - 1:r0] if i > 0 else None
            if edge is None:
                atts.append(jnp.zeros((sub, HEAD_DIM), F32))
            else:
                qt = padded_rows(q_i * jnp.exp2(b_i - edge), 0, sub, pack)
                kt = padded_rows(jnp.exp2(edge + gam[others[0]:others[1]]), others[0], others[1],
                                 HEAD_DIM)
                atts.append(lax.dot_general(qt, kt, nt, preferred_element_type=F32)[:sub])
        att_s[rows, :] = jnp.concatenate(atts, axis=0)
        for r0 in range(0, c_len, pack):
            q_t, b_t = q[r0:r0 + pack], b2[r0:r0 + pack]
            for sl in range(sub):
                row0 = off + r0 + sl
                g_rows = jnp.concatenate(
                    [jnp.broadcast_to(gam_s[row0 + u * sub:row0 + u * sub + 1, :], (sub, HEAD_DIM))
                     for u in range(pack // sub)], axis=0)
                tile = q_t * jnp.exp2(jnp.minimum(b_t + g_rows, 0.0))
                p_s[off + r0:off + r0 + pack, sl * HEAD_DIM:(sl + 1) * HEAD_DIM] = tile.astype(BF16)

    row = lax.broadcasted_iota(jnp.int32, (c_len, HEAD_DIM), 0)
    col = lax.broadcasted_iota(jnp.int32, (c_len, HEAD_DIM), 1)
    keep = (col >= row) if rev else (col <= row)
    v_pad = jnp.zeros((HEAD_DIM - c_len, HEAD_DIM), BF16)

    def outputs(ci):
        rows = slice(ci * c_len, (ci + 1) * c_len)
        dg = dg_s[rows, :]
        diag = [dg[0:sub]] + [pltpu.roll(dg[sub * i:sub * (i + 1)], sub * i, 1)
                              for i in range(1, n_sub)]
        att = att_s[rows, :] + jnp.concatenate(diag, axis=0)
        att = jnp.where(keep, att, 0.0).astype(BF16)
        o = (jnp.dot(att, jnp.concatenate([v_ref[rows, :], v_pad], axis=0),
                     preferred_element_type=F32)
             + lax.dot_general(qe_s[rows, :], seen_s[ci], nt, preferred_element_type=F32))
        if rev:
            tot = of_ref[rows, :] + o
            ms = jnp.mean(tot * tot, axis=-1, keepdims=True)
            y = tot * lax.rsqrt(ms + NORM_EPS) * ng_ref[...] * gt_ref[rows, :].astype(F32)
            o_ref[rows, :] = y.astype(o_ref.dtype)
        else:
            o_ref[rows, :] = o

    for ci in range(n_chunks):
        sums(ci)
    for ci in range(n_chunks):
        scores(ci)
    dg_s[...] = jnp.dot(p_s[...], e_ref[...], preferred_element_type=F32)
    st = st_s[...]
    for ci in (reversed(range(n_chunks)) if rev else range(n_chunks)):
        seen_s[ci] = st.astype(BF16)
        st = st * dec_s[ci] + upd_s[ci]
    st_s[...] = st
    for ci in range(n_chunks):
        outputs(ci)


def _gla_tables(rev):
    idx = jnp.arange(GLA_CHUNK)
    tri = (idx[:, None] <= idx[None, :]) if rev else (idx[:, None] >= idx[None, :])
    tri2 = jnp.concatenate([tri, tri], axis=1).astype(BF16)
    s_of_row = jnp.arange(GLA_SUB * HEAD_DIM) // HEAD_DIM
    place = jnp.arange(HEAD_DIM)[None, :] == s_of_row[:, None]
    return tri2, place.astype(BF16)


def gla_direction(q, v, gate, dec, norm_g, o_fwd, batch, seq, rev):
    m, d = q.shape
    heads = d // HEAD_DIM
    t_blk = min(GLA_BLOCK, seq)
    n_t = seq // t_blk
    n_chunks = t_blk // GLA_CHUNK
    tri2, place = _gla_tables(rev)

    def rowblk(b, h, t):
        return b * n_t + ((n_t - 1 - t) if rev else t)

    def col(group):
        return pl.BlockSpec((t_blk, HEAD_DIM), lambda b, h, t: (rowblk(b, h, t), group * heads + h))

    in_specs = [col(0), col(0), col(2 if rev else 0), col(3 if rev else 1),
                pl.BlockSpec(tri2.shape, lambda b, h, t: (0, 0)),
                pl.BlockSpec(place.shape, lambda b, h, t: (0, 0))]
    args = [q, v, dec, dec, tri2, place]
    if rev:
        in_specs += [col(0), col(0), pl.BlockSpec((1, HEAD_DIM), lambda b, h, t: (0, 0))]
        args += [o_fwd, gate, norm_g.reshape(1, HEAD_DIM)]
    return pl.pallas_call(
        functools.partial(_gla_kernel, rev=rev, n_chunks=n_chunks),
        grid=(batch, heads, n_t),
        in_specs=in_specs,
        out_specs=col(0),
        out_shape=jax.ShapeDtypeStruct((m, d), BF16 if rev else F32),
        scratch_shapes=[pltpu.VMEM((HEAD_DIM, HEAD_DIM), F32),
                        pltpu.VMEM((t_blk, HEAD_DIM), F32),
                        pltpu.VMEM((t_blk, HEAD_DIM), F32),
                        pltpu.VMEM((t_blk, HEAD_DIM), BF16),
                        pltpu.VMEM((n_chunks, 1, HEAD_DIM), F32),
                        pltpu.VMEM((n_chunks, HEAD_DIM, HEAD_DIM), F32),
                        pltpu.VMEM((n_chunks, HEAD_DIM, HEAD_DIM), BF16),
                        pltpu.VMEM((t_blk, HEAD_DIM), F32),
                        pltpu.VMEM((t_blk, GLA_SUB * HEAD_DIM), BF16),
                        pltpu.VMEM((t_blk, HEAD_DIM), F32)],
        compiler_params=_params("parallel", "parallel", "arbitrary"),
        name="gla_rev" if rev else "gla_fwd",
    )(*args)


def _hyena_proj_kernel(x_ref, xp_ref, xn_ref, g_ref, sc_ref, sh_ref, w_ref, b_ref, cw_ref, cb_ref,
                       x0_ref, z_ref, hn_ref, u_ref, *, per_seq, cs):
    i = pl.program_id(0)
    tm, d = x_ref.shape
    halo = xp_ref.shape[0]
    keep_prev = jnp.where(i % per_seq == 0, 0.0, 1.0)
    keep_next = jnp.where(i % per_seq == per_seq - 1, 0.0, 1.0)

    def norm(ref):
        return _rms_mod(ref[...], g_ref[...], sc_ref[...], sh_ref[...]).astype(BF16)

    hn_ref[0:halo, :] = norm(xp_ref)
    hn_ref[halo:halo + tm, :] = norm(x_ref)
    hn_ref[halo + tm:, :] = norm(xn_ref)
    row = lax.broadcasted_iota(jnp.int32, (tm + 2 * halo, cs), 0)
    keep = jnp.where(row < halo, keep_prev, jnp.where(row >= halo + tm, keep_next, 1.0))

    for c0 in range(0, d, cs):
        def conv(group):
            cols = slice(group * d + c0, group * d + c0 + cs)
            u = jnp.dot(hn_ref[...], w_ref[:, cols], preferred_element_type=F32) + b_ref[:, cols]
            u_ref[group] = u * keep
            return (u_ref[group, halo - 1:halo - 1 + tm, :] * cw_ref[0:1, cols]
                    + u_ref[group, halo:halo + tm, :] * cw_ref[1:2, cols]
                    + u_ref[group, halo + 1:halo + 1 + tm, :] * cw_ref[2:3, cols]
                    + cb_ref[:, cols])

        x0_ref[:, c0:c0 + cs] = conv(0).astype(x0_ref.dtype)
        z_ref[:, c0:c0 + cs] = (conv(1) * conv(2)).astype(z_ref.dtype)


def hyena_projection(x, g, scale, shift, w, bias, conv_w, conv_b, seq):
    m, d = x.shape
    tm = min(512, seq)
    per_seq = seq // tm
    halo = BF16_ROWS
    per_tile = tm // halo
    n_halo = m // halo
    cs = _tile(d, 512, 128)
    vec = pl.BlockSpec((None, 1, d), lambda i: (i // per_seq, 0, 0))
    return pl.pallas_call(
        functools.partial(_hyena_proj_kernel, per_seq=per_seq, cs=cs),
        grid=(m // tm,),
        in_specs=[
            pl.BlockSpec((tm, d), lambda i: (i, 0)),
            pl.BlockSpec((halo, d), lambda i: (jnp.maximum(i * per_tile - 1, 0), 0)),
            pl.BlockSpec((halo, d), lambda i: (jnp.minimum((i + 1) * per_tile, n_halo - 1), 0)),
            _resident((1, d)),
            vec, vec,
            _resident((d, 3 * d)),
            _resident((1, 3 * d)),
            _resident((3, 3 * d)),
            _resident((1, 3 * d)),
        ],
        out_specs=[pl.BlockSpec((tm, d), lambda i: (i, 0)),
                   pl.BlockSpec((tm, d), lambda i: (i, 0))],
        out_shape=[jax.ShapeDtypeStruct((m, d), BF16), jax.ShapeDtypeStruct((m, d), BF16)],
        scratch_shapes=[pltpu.VMEM((tm + 2 * halo, d), BF16),
                        pltpu.VMEM((3, tm + 2 * halo, cs), F32)],
        compiler_params=_params("parallel"),
        name="hyena_projection",
    )(x, x, x, g.reshape(1, d), scale, shift, w, bias.reshape(1, 3 * d), conv_w,
      conv_b.reshape(1, 3 * d))


def _filter_kernel(emb_ref, t_ref, keep_ref, win_ref, wmid_ref, b_ref, freq_ref, wout_ref,
                   delta_ref, o_ref):
    def split(a):
        hi = a.astype(BF16)
        return hi, (a - hi.astype(F32)).astype(BF16)

    def dot3(a, w, dims):
        a_hi, a_lo = split(a)
        w_hi, w_lo = split(w)
        mm = functools.partial(lax.dot_general, dimension_numbers=dims, preferred_element_type=F32)
        return mm(a_hi, w_hi) + mm(a_hi, w_lo) + mm(a_lo, w_hi)

    nn = (((1,), (0,)), ((), ()))
    tn = (((0,), (0,)), ((), ()))
    freq = freq_ref[...]
    h = jnp.sin(freq * (dot3(win_ref[...], emb_ref[...], nn) + b_ref[:, 0:1]))
    for mth in range(wmid_ref.shape[0]):
        h = jnp.sin(freq * (dot3(wmid_ref[mth], h, nn) + b_ref[:, mth + 1:mth + 2]))
    hf = dot3(h, wout_ref[...], tn)
    o_ref[...] = hf * jnp.exp(-t_ref[...] * delta_ref[...]) * keep_ref[...]


def hyena_filter_circular(seq, d, w_in, w_mid, b, freq, w_out):
    emb_dim, width = w_in.shape
    bands = (emb_dim - 1) // 2
    mrow = jnp.arange(2 * seq)
    pos = jnp.where(mrow < seq, mrow, jnp.where(mrow == seq, 0, 2 * seq - mrow)).astype(F32)[:, None]
    t_c = pos / (seq - 1.0)
    band = jnp.linspace(1e-4, bands - 1.0, bands, dtype=F32)
    ang = (2.0 * math.pi / seq) * pos * band
    emb_c = jnp.concatenate([t_c, jnp.cos(ang), -jnp.sin(ang)], axis=-1)
    emb_c = jnp.pad(emb_c, ((0, 0), (0, HY_FILTER_PAD - emb_dim)))
    keep = (mrow != seq).astype(F32)[:, None]
    w_in_p = jnp.pad(w_in.astype(F32), ((0, HY_FILTER_PAD - emb_dim), (0, 0)))
    deltas = jnp.abs(jnp.linspace(HY_MIN_DECAY, HY_MAX_DECAY, d, dtype=F32)).reshape(1, d)
    tr = min(512, seq)
    half = seq // tr
    n_mid = w_mid.shape[0]
    return pl.pallas_call(
        _filter_kernel,
        grid=(2 * seq // tr,),
        in_specs=[
            pl.BlockSpec((HY_FILTER_PAD, tr), lambda i: (0, i)),
            pl.BlockSpec((tr, 1), lambda i: (i, 0)),
            pl.BlockSpec((tr, 1), lambda i: (i, 0)),
            pl.BlockSpec((width, HY_FILTER_PAD), lambda i: (0, 0)),
            pl.BlockSpec((n_mid, width, width), lambda i: (0, 0, 0)),
            pl.BlockSpec((width, n_mid + 1), lambda i: (0, 0)),
            pl.BlockSpec((width, 1), lambda i: (0, 0)),
            pl.BlockSpec((width, d), lambda i: (0, i // half)),
            pl.BlockSpec((1, d), lambda i: (0, 0)),
        ],
        out_specs=pl.BlockSpec((tr, d), lambda i: (i, 0)),
        out_shape=jax.ShapeDtypeStruct((2 * seq, d), F32),
        compiler_params=_params("parallel"),
        name="hyena_filter",
    )(emb_c.T, t_c, keep, w_in_p.T, jnp.swapaxes(w_mid.astype(F32), 1, 2), b.astype(F32).T,
      freq.astype(F32).reshape(width, 1), w_out.astype(F32), deltas)


def _swap_major_sublane(x):
    return jnp.swapaxes(x, 0, 1)


def _dft_stage1_kernel(m_ref, x_ref, o_ref):
    x32 = x_ref[...].astype(F32)
    outs = []
    for g in range(x32.shape[1] // SUBLANES):
        xt = _swap_major_sublane(x32[:, g * SUBLANES:(g + 1) * SUBLANES, :]).astype(BF16)
        r = jnp.stack([jnp.dot(m_ref[...], xt[j], preferred_element_type=F32)
                       for j in range(SUBLANES)], axis=0)
        outs.append(_swap_major_sublane(r))
    o_ref[...] = jnp.concatenate(outs, axis=1).astype(o_ref.dtype)


def dft_stage1(mat, x):
    r, kdim = mat.shape
    _, n2, d = x.shape
    cw = _tile(d, 512, 128)
    return pl.pallas_call(
        _dft_stage1_kernel,
        grid=(n2 // BF16_ROWS, d // cw),
        in_specs=[pl.BlockSpec((r, kdim), lambda j, c: (0, 0)),
                  pl.BlockSpec((kdim, BF16_ROWS, cw), lambda j, c: (0, j, c))],
        out_specs=pl.BlockSpec((r, BF16_ROWS, cw), lambda j, c: (0, j, c)),
        out_shape=jax.ShapeDtypeStruct((r, n2, d), BF16),
        compiler_params=_params("parallel", "parallel"),
        name="dft_stage1",
    )(mat, x)


def _stacked(a_ref, u):
    return jnp.concatenate([a_ref[0, u], a_ref[1, u]], axis=0)


def _spectrum_kernel(g_ref, a_ref, o_ref):
    n2 = a_ref.shape[2]
    for u in range(a_ref.shape[1]):
        h = jnp.dot(g_ref[u], _stacked(a_ref, u), preferred_element_type=F32).astype(o_ref.dtype)
        o_ref[u, 0] = h[:n2]
        o_ref[u, 1] = h[n2:]


def filter_spectrum(g_tab, a):
    _, n1, n2, d = a.shape
    kb = _tile(n1, DFT_K1_BLOCK, 1)
    return pl.pallas_call(
        _spectrum_kernel,
        grid=(n1 // kb,),
        in_specs=[pl.BlockSpec((kb, 2 * n2, 2 * n2), lambda k: (k, 0, 0)),
                  pl.BlockSpec((2, kb, n2, d), lambda k: (0, k, 0, 0))],
        out_specs=pl.BlockSpec((kb, 2, n2, d), lambda k: (k, 0, 0, 0)),
        out_shape=jax.ShapeDtypeStruct((n1, 2, n2, d), BF16),
        compiler_params=_params("parallel"),
        name="filter_spectrum",
    )(g_tab, a)


def _freq_mul_kernel(g_ref, a_ref, h_ref, o_ref):
    n2 = a_ref.shape[2]
    tn = (((0,), (0,)), ((), ()))
    for u in range(a_ref.shape[1]):
        x = jnp.dot(g_ref[u], _stacked(a_ref, u), preferred_element_type=F32)
        xr, xi = x[:n2], x[n2:]
        hr, hi = h_ref[u, 0].astype(F32), h_ref[u, 1].astype(F32)
        p = jnp.concatenate([xr * hr - xi * hi, xr * hi + xi * hr], axis=0).astype(BF16)
        t = lax.dot_general(g_ref[u], p, tn, preferred_element_type=F32).astype(o_ref.dtype)
        o_ref[0, u] = t[:n2]
        o_ref[1, u] = t[n2:]


def freq_multiply(g_tab, a, h):
    _, n1, n2, d = a.shape
    kb = _tile(n1, DFT_K1_BLOCK, 1)
    return pl.pallas_call(
        _freq_mul_kernel,
        grid=(n1 // kb,),
        in_specs=[pl.BlockSpec((kb, 2 * n2, 2 * n2), lambda k: (k, 0, 0)),
                  pl.BlockSpec((2, kb, n2, d), lambda k: (0, k, 0, 0)),
                  pl.BlockSpec((kb, 2, n2, d), lambda k: (k, 0, 0, 0))],
        out_specs=pl.BlockSpec((2, kb, n2, d), lambda k: (0, k, 0, 0)),
        out_shape=jax.ShapeDtypeStruct((2, n1, n2, d), BF16),
        compiler_params=_params("parallel"),
        name="freq_multiply",
    )(g_tab, a, h)


def _conv_out_kernel(m_ref, t_ref, x0_ref, z_ref, skip_ref, o_ref):
    t32 = t_ref[...].astype(F32)
    ys = []
    for g in range(t32.shape[1] // SUBLANES):
        tt = _swap_major_sublane(t32[:, g * SUBLANES:(g + 1) * SUBLANES, :]).astype(BF16)
        y = jnp.stack([jnp.dot(m_ref[...], tt[j], preferred_element_type=F32)
                       for j in range(SUBLANES)], axis=0)
        ys.append(_swap_major_sublane(y))
    y = jnp.concatenate(ys, axis=1)
    o_ref[...] = (x0_ref[...].astype(F32)
                  * (y + skip_ref[...] * z_ref[...].astype(F32))).astype(o_ref.dtype)


def conv_output(mat, t, x0, z, skip):
    r, kdim = mat.shape
    _, n2, d = t.shape
    cw = _tile(d, 256, 128)
    blk = lambda rows: pl.BlockSpec((rows, BF16_ROWS, cw), lambda j, c: (0, j, c))
    return pl.pallas_call(
        _conv_out_kernel,
        grid=(n2 // BF16_ROWS, d // cw),
        in_specs=[pl.BlockSpec((r, kdim), lambda j, c: (0, 0)),
                  blk(kdim), blk(r), blk(r),
                  pl.BlockSpec((1, cw), lambda j, c: (0, c))],
        out_specs=blk(r),
        out_shape=jax.ShapeDtypeStruct((r, n2, d), BF16),
        compiler_params=_params("parallel", "parallel"),
        name="dft_inverse_out",
    )(mat, t, x0, z, skip)


def _dft_tables(seq):
    n = 2 * seq
    n2 = FFT_N2
    n1 = n // n2
    lh = seq // n2

    def cs(idx, mod):
        ang = (2.0 * math.pi / mod) * (idx % mod).astype(F32)
        return jnp.cos(ang), jnp.sin(ang)

    k1 = jnp.arange(n1)
    cr, sr = cs(k1[:, None] * jnp.arange(lh)[None, :], n1)
    f_sig = jnp.block([[cr, sr], [-sr, cr]])
    cf, sf = cs(k1[:, None] * k1[None, :], n1)
    f_filt = jnp.concatenate([cf, -sf], axis=0)
    idx = jnp.arange(n2)[None, None, :] * (k1[:, None, None] + n1 * jnp.arange(n2)[None, :, None])
    gr, gs = cs(idx, n)
    g_tab = jnp.concatenate([jnp.concatenate([gr, gs], axis=2),
                             jnp.concatenate([-gs, gr], axis=2)], axis=1)
    er, es = cs(jnp.arange(lh)[:, None] * k1[None, :], n1)
    f_inv = jnp.block([[er, -es], [es, er]]) / n
    return f_sig.astype(BF16), f_filt.astype(BF16), g_tab.astype(BF16), f_inv.astype(BF16)


def hyena_long_conv(x0, z, filt, skip, tables, batch, seq):
    assert batch == 2, "the two sequences ride as real / imaginary parts of one complex signal"
    f_sig, f_filt, g_tab, f_inv = tables
    m, d = z.shape
    n2 = FFT_N2
    n1 = 2 * seq // n2
    lh = seq // n2
    a_f = dft_stage1(f_filt, filt.reshape(n1, n2, d))
    h = filter_spectrum(g_tab, a_f.reshape(2, n1, n2, d))
    a = dft_stage1(f_sig, z.reshape(batch * lh, n2, d))
    t = freq_multiply(g_tab, a.reshape(2, n1, n2, d), h)
    y = conv_output(f_inv, t.reshape(2 * n1, n2, d), x0.reshape(batch * lh, n2, d),
                    z.reshape(batch * lh, n2, d), skip.astype(F32).reshape(1, d))
    return y.reshape(m, d)


def kernel(x, c, ada_w, ada_b, norm_g, hg_w_in, hg_lower_bounds, hg_norm_g, hg_w_out,
           hy_w_in, hy_b_in, hy_conv_w, hy_conv_b, hy_filt_w_in, hy_filt_w_mid, hy_filt_b,
           hy_filt_freq, hy_filt_w_out, hy_skip, hy_w_out, mlp_w1, mlp_w2, final_g):
    batch, seq, d = x.shape
    depth = ada_w.shape[0]
    n_mixers = 2
    assert d % HEAD_DIM == 0 and seq % max(GLA_CHUNK, FFT_N2) == 0

    mod = adaln_all(c, ada_w, ada_b)
    mod = mod.reshape(2 * depth, batch, 3, 1, d)

    lbs = jax.nn.softmax(hg_lower_bounds.astype(F32), axis=1)
    lbs = jnp.cumsum(lbs, axis=1) - lbs[:, :1]

    tables = _dft_tables(seq)

    xf = x.reshape(batch * seq, d)
    for i in range(depth):
        j = i // n_mixers
        shift, scale, gate = mod[2 * i, :, 0], mod[2 * i, :, 1], mod[2 * i, :, 2]
        if i % n_mixers == 0:
            q, v, gt, dec = hgrn_projection(xf, norm_g[i, 0], scale, shift, hg_w_in[j].astype(BF16),
                                            lbs[:, j], seq)
            o_f = gla_direction(q, v, gt, dec, hg_norm_g[j], None, batch, seq, rev=False)
            mix = gla_direction(q, v, gt, dec, hg_norm_g[j], o_f, batch, seq, rev=True)
            w_out = hg_w_out[j]
        else:
            x0, z = hyena_projection(xf, norm_g[i, 0], scale, shift, hy_w_in[j].astype(BF16),
                                     hy_b_in[j], hy_conv_w[j], hy_conv_b[j], seq)
            filt = hyena_filter_circular(seq, d, hy_filt_w_in[j], hy_filt_w_mid[j], hy_filt_b[j],
                                         hy_filt_freq[j], hy_filt_w_out[j])
            mix = hyena_long_conv(x0, z, filt, hy_skip[j], tables, batch, seq)
            w_out = hy_w_out[j]
        gate_mix = gate
        shift, scale, gate = mod[2 * i + 1, :, 0], mod[2 * i + 1, :, 1], mod[2 * i + 1, :, 2]
        xf = mixer_out_mlp(mix, w_out.astype(BF16), xf, gate_mix, norm_g[i, 1], scale, shift, gate,
                           mlp_w1[i].astype(BF16), mlp_w2[i].astype(BF16), final_g, seq,
                           final_norm=(i == depth - 1))
    return xf.reshape(batch, seq, d)
```

```python
import functools
import math

import jax
import jax.numpy as jnp
from jax import lax
from jax.experimental import pallas as pl
from jax.experimental.pallas import tpu as pltpu

F32 = jnp.float32
BF16 = jnp.bfloat16

NORM_EPS = 1e-6
HEAD_DIM = 128
GLA_CHUNK = 64
GLA_SUB = 8
GLA_BLOCK = 4096
SUBLANES = 8
BF16_ROWS = 16
HGRN_PROJ_SLAB = 256
FFT_N2 = 128
DFT_K1_BLOCK = 8
HY_FILTER_PAD = 64
HY_MAX_DECAY = math.log(1e-2) / 0.3
HY_MIN_DECAY = math.log(1e-2) / 1.5
VMEM_LIMIT = 56 * 1024 * 1024


def _params(*sem):
    return pltpu.CompilerParams(dimension_semantics=sem, vmem_limit_bytes=VMEM_LIMIT)


def _tile(n, target, unit):
    t = min(n, target) // unit * unit
    while n % t:
        t -= unit
    return t


def _rms_mod(x, g, scale, shift):
    ms = jnp.mean(x * x, axis=-1, keepdims=True)
    y = x * lax.rsqrt(ms + NORM_EPS) * g
    return y * (1.0 + scale) + shift


MATMUL_NN = (((1,), (0,)), ((), ()))
MATMUL_NT = (((1,), (1,)), ((), ()))
MATMUL_TN = (((0,), (0,)), ((), ()))


def _dot3(a, w, dims):
    def split(x):
        hi = x.astype(BF16)
        return hi, (x - hi.astype(F32)).astype(BF16)

    a_hi, a_lo = split(a)
    w_hi, w_lo = split(w)
    mm = functools.partial(lax.dot_general, dimension_numbers=dims, preferred_element_type=F32)
    return mm(a_hi, w_hi) + mm(a_hi, w_lo) + mm(a_lo, w_hi)


def _adaln_kernel(s_ref, w_ref, b_ref, o_ref):
    s = s_ref[...]
    s = s * jax.nn.sigmoid(s)
    o_ref[...] = _dot3(s, w_ref[...], MATMUL_NN) + b_ref[...]


def adaln_all(c, ada_w, ada_b):
    depth, two, d, d3 = ada_w.shape
    n = depth * two
    b = c.shape[0]
    rows = -(-b // SUBLANES) * SUBLANES
    cp = jnp.zeros((rows, d), F32).at[:b].set(c)
    tn = _tile(d3, 1536, 128)
    out = pl.pallas_call(
        _adaln_kernel,
        grid=(n, d3 // tn),
        in_specs=[
            pl.BlockSpec((rows, d), lambda i, j: (0, 0)),
            pl.BlockSpec((None, d, tn), lambda i, j: (i, 0, j)),
            pl.BlockSpec((None, 1, tn), lambda i, j: (i, 0, j)),
        ],
        out_specs=pl.BlockSpec((None, rows, tn), lambda i, j: (i, 0, j)),
        out_shape=jax.ShapeDtypeStruct((n, rows, d3), F32),
        compiler_params=_params("parallel", "parallel"),
        name="adaln",
    )(cp, ada_w.reshape(n, d, d3), ada_b.reshape(n, 1, d3))
    return out[:, :b]


def _resident(shape):
    return pl.BlockSpec(shape, lambda i: (0,) * len(shape), pipeline_mode=pl.Buffered(1))


def _out_mlp_kernel(a_ref, wo_ref, x_ref, gm_ref, g_ref, sc_ref, sh_ref, gate_ref, w1_ref, w2_ref,
                    fg_ref, o_ref, hn_ref, acc_ref, *, final_norm, tf):
    o_ref[...] = x_ref[...] + gm_ref[...] * jnp.dot(a_ref[...], wo_ref[...],
                                                     preferred_element_type=F32)
    hn_ref[...] = _rms_mod(o_ref[...], g_ref[...], sc_ref[...], sh_ref[...]).astype(BF16)
    for k in range(w1_ref.shape[1] // tf):
        cols = slice(k * tf, (k + 1) * tf)
        h = jnp.dot(hn_ref[...], w1_ref[:, cols], preferred_element_type=F32)
        h = jnp.square(jnp.maximum(h, 0.0)).astype(BF16)
        part = jnp.dot(h, w2_ref[cols, :], preferred_element_type=F32)
        if k == 0:
            acc_ref[...] = part
        else:
            acc_ref[...] += part
    out = o_ref[...] + gate_ref[...] * acc_ref[...]
    if final_norm:
        ms = jnp.mean(out * out, axis=-1, keepdims=True)
        out = out * lax.rsqrt(ms + NORM_EPS) * fg_ref[...]
    o_ref[...] = out


def mixer_out_mlp(a, w_out, x, gate_mix, g, scale, shift, gate, w1, w2, final_g, seq, final_norm):
    m, d = x.shape
    dff = w1.shape[1]
    tm = min(512, seq)
    tf = _tile(dff, 1024, 128)
    per_seq = seq // tm
    vec = pl.BlockSpec((None, 1, d), lambda i: (i // per_seq, 0, 0))
    return pl.pallas_call(
        functools.partial(_out_mlp_kernel, final_norm=final_norm, tf=tf),
        grid=(m // tm,),
        in_specs=[
            pl.BlockSpec((tm, d), lambda i: (i, 0)),
            _resident((d, d)),
            pl.BlockSpec((tm, d), lambda i: (i, 0)),
            vec,
            _resident((1, d)),
            vec, vec, vec,
            _resident((d, dff)),
            _resident((dff, d)),
            _resident((1, d)),
        ],
        out_specs=pl.BlockSpec((tm, d), lambda i: (i, 0)),
        out_shape=jax.ShapeDtypeStruct((m, d), F32),
        scratch_shapes=[pltpu.VMEM((tm, d), BF16), pltpu.VMEM((tm, d), F32)],
        compiler_params=_params("parallel"),
        name="mixer_out_mlp",
    )(a, w_out, x, gate_mix, g.reshape(1, d), scale, shift, gate, w1, w2, final_g.reshape(1, d))


def _hgrn_proj_kernel(x_ref, g_ref, sc_ref, sh_ref, w_ref, lb_ref, q_ref, v_ref, gt_ref, dec_ref,
                      hn_ref):
    d = q_ref.shape[1]
    hn_ref[...] = _rms_mod(x_ref[...], g_ref[...], sc_ref[...], sh_ref[...]).astype(BF16)

    def silu(a):
        half = 0.5 * a
        return half + half * jnp.tanh(half)

    slab = min(d, HGRN_PROJ_SLAB)
    for c0 in range(0, d, slab):
        cols = slice(c0, c0 + slab)

        def proj(group):
            return jnp.dot(hn_ref[...], w_ref[:, group * d + c0:group * d + c0 + slab],
                           preferred_element_type=F32)

        q_ref[:, cols] = silu(proj(0)).astype(BF16)
        v_ref[:, cols] = proj(1).astype(BF16)
        for dirn in range(2):
            z = proj(2 + dirn)
            lb = lb_ref[dirn:dirn + 1, cols]
            one_minus_lb = 1.0 - lb
            dec_ref[:, 2 * dirn * d + c0:2 * dirn * d + c0 + slab] = jnp.log2(
                lb + one_minus_lb * (1.0 / (1.0 + jnp.exp(-z))))
            dec_ref[:, (2 * dirn + 1) * d + c0:(2 * dirn + 1) * d + c0 + slab] = (
                jnp.log2(one_minus_lb) - jnp.log2(1.0 + jnp.exp(z)))
        gt_ref[:, cols] = silu(proj(4)).astype(BF16)


def hgrn_projection(x, g, scale, shift, w, lbs, seq):
    m, d = x.shape
    tm = min(512, seq)
    per_seq = seq // tm
    row = lambda width: pl.BlockSpec((tm, width), lambda i: (i, 0))
    return pl.pallas_call(
        _hgrn_proj_kernel,
        grid=(m // tm,),
        in_specs=[
            row(d),
            _resident((1, d)),
            pl.BlockSpec((None, 1, d), lambda i: (i // per_seq, 0, 0)),
            pl.BlockSpec((None, 1, d), lambda i: (i // per_seq, 0, 0)),
            _resident((d, 5 * d)),
            _resident((2, d)),
        ],
        out_specs=[row(d), row(d), row(d), row(4 * d)],
        out_shape=[jax.ShapeDtypeStruct((m, d), BF16), jax.ShapeDtypeStruct((m, d), BF16),
                   jax.ShapeDtypeStruct((m, d), BF16), jax.ShapeDtypeStruct((m, 4 * d), F32)],
        scratch_shapes=[pltpu.VMEM((tm, d), BF16)],
        compiler_params=_params("parallel"),
        name="hgrn_projection",
    )(x, g.reshape(1, d), scale, shift, w, lbs)


def _gla_kernel(*refs, rev, n_chunks):
    if rev:
        (q_ref, v_ref, g2_ref, lk2_ref, tri_ref, e_ref, of_ref, gt_ref, ng_ref, o_ref,
         st_s, b2_s, gam_s, qe_s, dec_s, upd_s, seen_s, att_s, p_s, dg_s) = refs
    else:
        (q_ref, v_ref, g2_ref, lk2_ref, tri_ref, e_ref, o_ref,
         st_s, b2_s, gam_s, qe_s, dec_s, upd_s, seen_s, att_s, p_s, dg_s) = refs
    c_len, sub, pack = GLA_CHUNK, GLA_SUB, BF16_ROWS
    n_sub = c_len // sub
    nt, tn = MATMUL_NT, MATMUL_TN

    @pl.when(pl.program_id(2) == 0)
    def _():
        st_s[...] = jnp.zeros_like(st_s)

    tri2 = tri_ref[...]

    def sums(ci):
        rows = slice(ci * c_len, (ci + 1) * c_len)
        g2 = g2_ref[rows, :]
        hi = g2.astype(BF16)
        lo = (g2 - hi.astype(F32)).astype(BF16)
        b2 = jnp.dot(tri2, jnp.concatenate([hi, lo], axis=0), preferred_element_type=F32)
        b2_s[rows, :] = b2
        gam_s[rows, :] = lk2_ref[rows, :] - b2

    def padded_rows(x, lo, hi, total):
        lo_t, hi_t = lo // pack * pack, -(-hi // pack) * pack
        parts = ([jnp.zeros((lo - lo_t, HEAD_DIM), F32)] if lo > lo_t else []) + [x]
        if hi_t > hi:
            parts.append(jnp.zeros((hi_t - hi, HEAD_DIM), F32))
        tiles = [jnp.concatenate(parts, axis=0).astype(BF16)]
        if lo_t:
            tiles.insert(0, jnp.zeros((lo_t, HEAD_DIM), BF16))
        if total > hi_t:
            tiles.append(jnp.zeros((total - hi_t, HEAD_DIM), BF16))
        return jnp.concatenate(tiles, axis=0)

    def scores(ci):
        off = ci * c_len
        rows = slice(off, off + c_len)
        q = q_ref[rows, :].astype(F32)
        b2 = b2_s[rows, :]
        gam = gam_s[rows, :]
        b2_end = b2[0:1] if rev else b2[c_len - 1:c_len]
        qe_s[rows, :] = (q * jnp.exp2(b2)).astype(BF16)
        upd_s[ci] = lax.dot_general(v_ref[rows, :], jnp.exp2(b2_end + gam).astype(BF16), tn,
                                    preferred_element_type=F32)
        dec_s[ci] = jnp.exp2(b2_end)
        atts = []
        for i in range(n_sub):
            r0 = sub * i
            q_i, b_i = q[r0:r0 + sub], b2[r0:r0 + sub]
            if rev:
                others = (r0 + sub, c_len)
                edge = b2[r0 + sub:r0 + sub + 1] if i < n_sub - 1 else None
            else:
                others = (0, r0)
                edge = b2[r0 - 1:r0] if i > 0 else None
            if edge is None:
                atts.append(jnp.zeros((sub, HEAD_DIM), F32))
            else:
                qt = padded_rows(q_i * jnp.exp2(b_i - edge), 0, sub, pack)
                kt = padded_rows(jnp.exp2(edge + gam[others[0]:others[1]]), others[0], others[1],
                                 HEAD_DIM)
                atts.append(lax.dot_general(qt, kt, nt, preferred_element_type=F32)[:sub])
        att_s[rows, :] = jnp.concatenate(atts, axis=0)
        for r0 in range(0, c_len, pack):
            q_t, b_t = q[r0:r0 + pack], b2[r0:r0 + pack]
            for sl in range(sub):
                row0 = off + r0 + sl
                g_rows = jnp.concatenate(
                    [jnp.broadcast_to(gam_s[row0 + u * sub:row0 + u * sub + 1, :], (sub, HEAD_DIM))
                     for u in range(pack // sub)], axis=0)
                tile = q_t * jnp.exp2(jnp.minimum(b_t + g_rows, 0.0))
                p_s[off + r0:off + r0 + pack, sl * HEAD_DIM:(sl + 1) * HEAD_DIM] = tile.astype(BF16)

    row = lax.broadcasted_iota(jnp.int32, (c_len, HEAD_DIM), 0)
    col = lax.broadcasted_iota(jnp.int32, (c_len, HEAD_DIM), 1)
    keep = (col >= row) if rev else (col <= row)
    v_pad = jnp.zeros((HEAD_DIM - c_len, HEAD_DIM), BF16)

    def outputs(ci):
        rows = slice(ci * c_len, (ci + 1) * c_len)
        dg = dg_s[rows, :]
        diag = [dg[0:sub]] + [pltpu.roll(dg[sub * i:sub * (i + 1)], sub * i, 1)
                              for i in range(1, n_sub)]
        att = att_s[rows, :] + jnp.concatenate(diag, axis=0)
        att = jnp.where(keep, att, 0.0).astype(BF16)
        o = (jnp.dot(att, jnp.concatenate([v_ref[rows, :], v_pad], axis=0),
                     preferred_element_type=F32)
             + lax.dot_general(qe_s[rows, :], seen_s[ci], nt, preferred_element_type=F32))
        if rev:
            tot = of_ref[rows, :] + o
            ms = jnp.mean(tot * tot, axis=-1, keepdims=True)
            y = tot * lax.rsqrt(ms + NORM_EPS) * ng_ref[...] * gt_ref[rows, :].astype(F32)
            o_ref[rows, :] = y.astype(o_ref.dtype)
        else:
            o_ref[rows, :] = o

    for ci in range(n_chunks):
        sums(ci)
    for ci in range(n_chunks):
        scores(ci)
    dg_s[...] = jnp.dot(p_s[...], e_ref[...], preferred_element_type=F32)
    st = st_s[...]
    for ci in (reversed(range(n_chunks)) if rev else range(n_chunks)):
        seen_s[ci] = st.astype(BF16)
        st = st * dec_s[ci] + upd_s[ci]
    st_s[...] = st
    for ci in range(n_chunks):
        outputs(ci)


def _gla_tables(rev):
    idx = jnp.arange(GLA_CHUNK)
    tri = (idx[:, None] <= idx[None, :]) if rev else (idx[:, None] >= idx[None, :])
    tri2 = jnp.concatenate([tri, tri], axis=1).astype(BF16)
    s_of_row = jnp.arange(GLA_SUB * HEAD_DIM) // HEAD_DIM
    place = jnp.arange(HEAD_DIM)[None, :] == s_of_row[:, None]
    return tri2, place.astype(BF16)


def gla_direction(q, v, gate, dec, norm_g, o_fwd, batch, seq, rev):
    m, d = q.shape
    heads = d // HEAD_DIM
    t_blk = min(GLA_BLOCK, seq)
    n_t = seq // t_blk
    n_chunks = t_blk // GLA_CHUNK
    tri2, place = _gla_tables(rev)

    def rowblk(b, h, t):
        return b * n_t + ((n_t - 1 - t) if rev else t)

    def col(group):
        return pl.BlockSpec((t_blk, HEAD_DIM), lambda b, h, t: (rowblk(b, h, t), group * heads + h))

    in_specs = [col(0), col(0), col(2 if rev else 0), col(3 if rev else 1),
                pl.BlockSpec(tri2.shape, lambda b, h, t: (0, 0)),
                pl.BlockSpec(place.shape, lambda b, h, t: (0, 0))]
    args = [q, v, dec, dec, tri2, place]
    if rev:
        in_specs += [col(0), col(0), pl.BlockSpec((1, HEAD_DIM), lambda b, h, t: (0, 0))]
        args += [o_fwd, gate, norm_g.reshape(1, HEAD_DIM)]
    return pl.pallas_call(
        functools.partial(_gla_kernel, rev=rev, n_chunks=n_chunks),
        grid=(batch, heads, n_t),
        in_specs=in_specs,
        out_specs=col(0),
        out_shape=jax.ShapeDtypeStruct((m, d), BF16 if rev else F32),
        scratch_shapes=[pltpu.VMEM((HEAD_DIM, HEAD_DIM), F32),
                        pltpu.VMEM((t_blk, HEAD_DIM), F32),
                        pltpu.VMEM((t_blk, HEAD_DIM), F32),
                        pltpu.VMEM((t_blk, HEAD_DIM), BF16),
                        pltpu.VMEM((n_chunks, 1, HEAD_DIM), F32),
                        pltpu.VMEM((n_chunks, HEAD_DIM, HEAD_DIM), F32),
                        pltpu.VMEM((n_chunks, HEAD_DIM, HEAD_DIM), BF16),
                        pltpu.VMEM((t_blk, HEAD_DIM), F32),
                        pltpu.VMEM((t_blk, GLA_SUB * HEAD_DIM), BF16),
                        pltpu.VMEM((t_blk, HEAD_DIM), F32)],
        compiler_params=_params("parallel", "parallel", "arbitrary"),
        name="gla_rev" if rev else "gla_fwd",
    )(*args)


def _hyena_proj_kernel(x_ref, xp_ref, xn_ref, g_ref, sc_ref, sh_ref, w_ref, b_ref, cw_ref, cb_ref,
                       x0_ref, z_ref, hn_ref, u_ref, *, per_seq, cs):
    i = pl.program_id(0)
    tm, d = x_ref.shape
    halo = xp_ref.shape[0]
    keep_prev = jnp.where(i % per_seq == 0, 0.0, 1.0)
    keep_next = jnp.where(i % per_seq == per_seq - 1, 0.0, 1.0)

    def norm(ref):
        return _rms_mod(ref[...], g_ref[...], sc_ref[...], sh_ref[...]).astype(BF16)

    hn_ref[0:halo, :] = norm(xp_ref)
    hn_ref[halo:halo + tm, :] = norm(x_ref)
    hn_ref[halo + tm:, :] = norm(xn_ref)
    row = lax.broadcasted_iota(jnp.int32, (tm + 2 * halo, cs), 0)
    keep = jnp.where(row < halo, keep_prev, jnp.where(row >= halo + tm, keep_next, 1.0))

    for c0 in range(0, d, cs):
        def conv(group):
            cols = slice(group * d + c0, group * d + c0 + cs)
            u = jnp.dot(hn_ref[...], w_ref[:, cols], preferred_element_type=F32) + b_ref[:, cols]
            u_ref[group] = u * keep
            return (u_ref[group, halo - 1:halo - 1 + tm, :] * cw_ref[0:1, cols]
                    + u_ref[group, halo:halo + tm, :] * cw_ref[1:2, cols]
                    + u_ref[group, halo + 1:halo + 1 + tm, :] * cw_ref[2:3, cols]
                    + cb_ref[:, cols])

        x0_ref[:, c0:c0 + cs] = conv(0).astype(x0_ref.dtype)
        z_ref[:, c0:c0 + cs] = (conv(1) * conv(2)).astype(z_ref.dtype)


def hyena_projection(x, g, scale, shift, w, bias, conv_w, conv_b, seq):
    m, d = x.shape
    tm = min(512, seq)
    per_seq = seq // tm
    halo = BF16_ROWS
    per_tile = tm // halo
    n_halo = m // halo
    cs = _tile(d, 512, 128)
    vec = pl.BlockSpec((None, 1, d), lambda i: (i // per_seq, 0, 0))
    return pl.pallas_call(
        functools.partial(_hyena_proj_kernel, per_seq=per_seq, cs=cs),
        grid=(m // tm,),
        in_specs=[
            pl.BlockSpec((tm, d), lambda i: (i, 0)),
            pl.BlockSpec((halo, d), lambda i: (jnp.maximum(i * per_tile - 1, 0), 0)),
            pl.BlockSpec((halo, d), lambda i: (jnp.minimum((i + 1) * per_tile, n_halo - 1), 0)),
            _resident((1, d)),
            vec, vec,
            _resident((d, 3 * d)),
            _resident((1, 3 * d)),
            _resident((3, 3 * d)),
            _resident((1, 3 * d)),
        ],
        out_specs=[pl.BlockSpec((tm, d), lambda i: (i, 0)),
                   pl.BlockSpec((tm, d), lambda i: (i, 0))],
        out_shape=[jax.ShapeDtypeStruct((m, d), BF16), jax.ShapeDtypeStruct((m, d), BF16)],
        scratch_shapes=[pltpu.VMEM((tm + 2 * halo, d), BF16),
                        pltpu.VMEM((3, tm + 2 * halo, cs), F32)],
        compiler_params=_params("parallel"),
        name="hyena_projection",
    )(x, x, x, g.reshape(1, d), scale, shift, w, bias.reshape(1, 3 * d), conv_w,
      conv_b.reshape(1, 3 * d))


def _filter_kernel(emb_ref, t_ref, keep_ref, win_ref, wmid_ref, b_ref, freq_ref, wout_ref,
                   delta_ref, o_ref):
    freq = freq_ref[...]
    h = jnp.sin(freq * (_dot3(win_ref[...], emb_ref[...], MATMUL_NN) + b_ref[:, 0:1]))
    for mth in range(wmid_ref.shape[0]):
        h = jnp.sin(freq * (_dot3(wmid_ref[mth], h, MATMUL_NN) + b_ref[:, mth + 1:mth + 2]))
    hf = _dot3(h, wout_ref[...], MATMUL_TN)
    o_ref[...] = hf * jnp.exp(-t_ref[...] * delta_ref[...]) * keep_ref[...]


def hyena_filter_circular(seq, d, w_in, w_mid, b, freq, w_out):
    emb_dim, width = w_in.shape
    bands = (emb_dim - 1) // 2
    mrow = jnp.arange(2 * seq)
    pos = jnp.where(mrow < seq, mrow, jnp.where(mrow == seq, 0, 2 * seq - mrow)).astype(F32)[:, None]
    t_c = pos / (seq - 1.0)
    band = jnp.linspace(1e-4, bands - 1.0, bands, dtype=F32)
    ang = (2.0 * math.pi / seq) * pos * band
    emb_c = jnp.concatenate([t_c, jnp.cos(ang), -jnp.sin(ang)], axis=-1)
    emb_c = jnp.pad(emb_c, ((0, 0), (0, HY_FILTER_PAD - emb_dim)))
    keep = (mrow != seq).astype(F32)[:, None]
    w_in_p = jnp.pad(w_in.astype(F32), ((0, HY_FILTER_PAD - emb_dim), (0, 0)))
    deltas = jnp.abs(jnp.linspace(HY_MIN_DECAY, HY_MAX_DECAY, d, dtype=F32)).reshape(1, d)
    tr = min(512, seq)
    half = seq // tr
    n_mid = w_mid.shape[0]
    return pl.pallas_call(
        _filter_kernel,
        grid=(2 * seq // tr,),
        in_specs=[
            pl.BlockSpec((HY_FILTER_PAD, tr), lambda i: (0, i)),
            pl.BlockSpec((tr, 1), lambda i: (i, 0)),
            pl.BlockSpec((tr, 1), lambda i: (i, 0)),
            pl.BlockSpec((width, HY_FILTER_PAD), lambda i: (0, 0)),
            pl.BlockSpec((n_mid, width, width), lambda i: (0, 0, 0)),
            pl.BlockSpec((width, n_mid + 1), lambda i: (0, 0)),
            pl.BlockSpec((width, 1), lambda i: (0, 0)),
            pl.BlockSpec((width, d), lambda i: (0, i // half)),
            pl.BlockSpec((1, d), lambda i: (0, 0)),
        ],
        out_specs=pl.BlockSpec((tr, d), lambda i: (i, 0)),
        out_shape=jax.ShapeDtypeStruct((2 * seq, d), F32),
        compiler_params=_params("parallel"),
        name="hyena_filter",
    )(emb_c.T, t_c, keep, w_in_p.T, jnp.swapaxes(w_mid.astype(F32), 1, 2), b.astype(F32).T,
      freq.astype(F32).reshape(width, 1), w_out.astype(F32), deltas)


def _swap_major_sublane(x):
    return jnp.swapaxes(x, 0, 1)


def _dft_stage1_kernel(m_ref, x_ref, o_ref):
    x32 = x_ref[...].astype(F32)
    outs = []
    for g in range(x32.shape[1] // SUBLANES):
        xt = _swap_major_sublane(x32[:, g * SUBLANES:(g + 1) * SUBLANES, :]).astype(BF16)
        r = jnp.stack([jnp.dot(m_ref[...], xt[j], preferred_element_type=F32)
                       for j in range(SUBLANES)], axis=0)
        outs.append(_swap_major_sublane(r))
    o_ref[...] = jnp.concatenate(outs, axis=1).astype(o_ref.dtype)


def dft_stage1(mat, x):
    r, kdim = mat.shape
    _, n2, d = x.shape
    cw = _tile(d, 512, 128)
    return pl.pallas_call(
        _dft_stage1_kernel,
        grid=(n2 // BF16_ROWS, d // cw),
        in_specs=[pl.BlockSpec((r, kdim), lambda j, c: (0, 0)),
                  pl.BlockSpec((kdim, BF16_ROWS, cw), lambda j, c: (0, j, c))],
        out_specs=pl.BlockSpec((r, BF16_ROWS, cw), lambda j, c: (0, j, c)),
        out_shape=jax.ShapeDtypeStruct((r, n2, d), BF16),
        compiler_params=_params("parallel", "parallel"),
        name="dft_stage1",
    )(mat, x)


def _stacked(a_ref, u):
    return jnp.concatenate([a_ref[0, u], a_ref[1, u]], axis=0)


def _spectrum_kernel(g_ref, a_ref, o_ref):
    n2 = a_ref.shape[2]
    for u in range(a_ref.shape[1]):
        h = jnp.dot(g_ref[u], _stacked(a_ref, u), preferred_element_type=F32).astype(o_ref.dtype)
        o_ref[u, 0] = h[:n2]
        o_ref[u, 1] = h[n2:]


def filter_spectrum(g_tab, a):
    _, n1, n2, d = a.shape
    kb = _tile(n1, DFT_K1_BLOCK, 1)
    return pl.pallas_call(
        _spectrum_kernel,
        grid=(n1 // kb,),
        in_specs=[pl.BlockSpec((kb, 2 * n2, 2 * n2), lambda k: (k, 0, 0)),
                  pl.BlockSpec((2, kb, n2, d), lambda k: (0, k, 0, 0))],
        out_specs=pl.BlockSpec((kb, 2, n2, d), lambda k: (k, 0, 0, 0)),
        out_shape=jax.ShapeDtypeStruct((n1, 2, n2, d), BF16),
        compiler_params=_params("parallel"),
        name="filter_spectrum",
    )(g_tab, a)


def _freq_mul_kernel(g_ref, a_ref, h_ref, o_ref):
    n2 = a_ref.shape[2]
    tn = MATMUL_TN
    for u in range(a_ref.shape[1]):
        x = jnp.dot(g_ref[u], _stacked(a_ref, u), preferred_element_type=F32)
        xr, xi = x[:n2], x[n2:]
        hr, hi = h_ref[u, 0].astype(F32), h_ref[u, 1].astype(F32)
        p = jnp.concatenate([xr * hr - xi * hi, xr * hi + xi * hr], axis=0).astype(BF16)
        t = lax.dot_general(g_ref[u], p, tn, preferred_element_type=F32).astype(o_ref.dtype)
        o_ref[0, u] = t[:n2]
        o_ref[1, u] = t[n2:]


def freq_multiply(g_tab, a, h):
    _, n1, n2, d = a.shape
    kb = _tile(n1, DFT_K1_BLOCK, 1)
    return pl.pallas_call(
        _freq_mul_kernel,
        grid=(n1 // kb,),
        in_specs=[pl.BlockSpec((kb, 2 * n2, 2 * n2), lambda k: (k, 0, 0)),
                  pl.BlockSpec((2, kb, n2, d), lambda k: (0, k, 0, 0)),
                  pl.BlockSpec((kb, 2, n2, d), lambda k: (k, 0, 0, 0))],
        out_specs=pl.BlockSpec((2, kb, n2, d), lambda k: (0, k, 0, 0)),
        out_shape=jax.ShapeDtypeStruct((2, n1, n2, d), BF16),
        compiler_params=_params("parallel"),
        name="freq_multiply",
    )(g_tab, a, h)


def _conv_out_kernel(m_ref, t_ref, x0_ref, z_ref, skip_ref, o_ref):
    t32 = t_ref[...].astype(F32)
    ys = []
    for g in range(t32.shape[1] // SUBLANES):
        tt = _swap_major_sublane(t32[:, g * SUBLANES:(g + 1) * SUBLANES, :]).astype(BF16)
        y = jnp.stack([jnp.dot(m_ref[...], tt[j], preferred_element_type=F32)
                       for j in range(SUBLANES)], axis=0)
        ys.append(_swap_major_sublane(y))
    y = jnp.concatenate(ys, axis=1)
    o_ref[...] = (x0_ref[...].astype(F32)
                  * (y + skip_ref[...] * z_ref[...].astype(F32))).astype(o_ref.dtype)


def conv_output(mat, t, x0, z, skip):
    r, kdim = mat.shape
    _, n2, d = t.shape
    cw = _tile(d, 256, 128)
    blk = lambda rows: pl.BlockSpec((rows, BF16_ROWS, cw), lambda j, c: (0, j, c))
    return pl.pallas_call(
        _conv_out_kernel,
        grid=(n2 // BF16_ROWS, d // cw),
        in_specs=[pl.BlockSpec((r, kdim), lambda j, c: (0, 0)),
                  blk(kdim), blk(r), blk(r),
                  pl.BlockSpec((1, cw), lambda j, c: (0, c))],
        out_specs=blk(r),
        out_shape=jax.ShapeDtypeStruct((r, n2, d), BF16),
        compiler_params=_params("parallel", "parallel"),
        name="dft_inverse_out",
    )(mat, t, x0, z, skip)


def _dft_tables(seq):
    n = 2 * seq
    n2 = FFT_N2
    n1 = n // n2
    lh = seq // n2

    def cs(idx, mod):
        ang = (2.0 * math.pi / mod) * (idx % mod).astype(F32)
        return jnp.cos(ang), jnp.sin(ang)

    k1 = jnp.arange(n1)
    cr, sr = cs(k1[:, None] * jnp.arange(lh)[None, :], n1)
    f_sig = jnp.block([[cr, sr], [-sr, cr]])
    cf, sf = cs(k1[:, None] * k1[None, :], n1)
    f_filt = jnp.concatenate([cf, -sf], axis=0)
    ca, sa = cs(k1[:, None] * jnp.arange(n2)[None, :], n)
    cb, sb = cs(jnp.arange(n2)[:, None] * jnp.arange(n2)[None, :], n2)
    gr = ca[:, None, :] * cb[None] - sa[:, None, :] * sb[None]
    gs = sa[:, None, :] * cb[None] + ca[:, None, :] * sb[None]
    g_tab = jnp.concatenate([jnp.concatenate([gr, gs], axis=2),
                             jnp.concatenate([-gs, gr], axis=2)], axis=1)
    er, es = cs(jnp.arange(lh)[:, None] * k1[None, :], n1)
    f_inv = jnp.block([[er, -es], [es, er]]) / n
    return f_sig.astype(BF16), f_filt.astype(BF16), g_tab.astype(BF16), f_inv.astype(BF16)


def hyena_long_conv(x0, z, filt, skip, tables, batch, seq):
    assert batch == 2, "the two sequences ride as real / imaginary parts of one complex signal"
    f_sig, f_filt, g_tab, f_inv = tables
    m, d = z.shape
    n2 = FFT_N2
    n1 = 2 * seq // n2
    lh = seq // n2
    a_f = dft_stage1(f_filt, filt.reshape(n1, n2, d))
    h = filter_spectrum(g_tab, a_f.reshape(2, n1, n2, d))
    a = dft_stage1(f_sig, z.reshape(batch * lh, n2, d))
    t = freq_multiply(g_tab, a.reshape(2, n1, n2, d), h)
    y = conv_output(f_inv, t.reshape(2 * n1, n2, d), x0.reshape(batch * lh, n2, d),
                    z.reshape(batch * lh, n2, d), skip.astype(F32).reshape(1, d))
    return y.reshape(m, d)


def kernel(x, c, ada_w, ada_b, norm_g, hg_w_in, hg_lower_bounds, hg_norm_g, hg_w_out,
           hy_w_in, hy_b_in, hy_conv_w, hy_conv_b, hy_filt_w_in, hy_filt_w_mid, hy_filt_b,
           hy_filt_freq, hy_filt_w_out, hy_skip, hy_w_out, mlp_w1, mlp_w2, final_g):
    batch, seq, d = x.shape
    depth = ada_w.shape[0]
    n_mixers = 2
    assert d % HEAD_DIM == 0 and seq % max(GLA_CHUNK, FFT_N2) == 0

    mod = adaln_all(c, ada_w, ada_b)
    mod = mod.reshape(2 * depth, batch, 3, 1, d)

    lbs = jax.nn.softmax(hg_lower_bounds.astype(F32), axis=1)
    lbs = jnp.cumsum(lbs, axis=1) - lbs[:, :1]

    tables = _dft_tables(seq)

    xf = x.reshape(batch * seq, d)
    for i in range(depth):
        j = i // n_mixers
        shift, scale, gate = mod[2 * i, :, 0], mod[2 * i, :, 1], mod[2 * i, :, 2]
        if i % n_mixers == 0:
            q, v, gt, dec = hgrn_projection(xf, norm_g[i, 0], scale, shift, hg_w_in[j].astype(BF16),
                                            lbs[:, j], seq)
            o_f = gla_direction(q, v, gt, dec, hg_norm_g[j], None, batch, seq, rev=False)
            mix = gla_direction(q, v, gt, dec, hg_norm_g[j], o_f, batch, seq, rev=True)
            w_out = hg_w_out[j]
        else:
            x0, z = hyena_projection(xf, norm_g[i, 0], scale, shift, hy_w_in[j].astype(BF16),
                                     hy_b_in[j], hy_conv_w[j], hy_conv_b[j], seq)
            filt = hyena_filter_circular(seq, d, hy_filt_w_in[j], hy_filt_w_mid[j], hy_filt_b[j],
                                         hy_filt_freq[j], hy_filt_w_out[j])
            mix = hyena_long_conv(x0, z, filt, hy_skip[j], tables, batch, seq)
            w_out = hy_w_out[j]
        gate_mix = gate
        shift, scale, gate = mod[2 * i + 1, :, 0], mod[2 * i + 1, :, 1], mod[2 * i + 1, :, 2]
        xf = mixer_out_mlp(mix, w_out.astype(BF16), xf, gate_mix, norm_g[i, 1], scale, shift, gate,
                           mlp_w1[i].astype(BF16), mlp_w2[i].astype(BF16), final_g, seq,
                           final_norm=(i == depth - 1))
    return xf.reshape(batch, seq, d)
```

```python
import functools
import math

import jax
import jax.numpy as jnp
from jax import lax
from jax.experimental import pallas as pl
from jax.experimental.pallas import tpu as pltpu

F32 = jnp.float32
BF16 = jnp.bfloat16

NORM_EPS = 1e-6
HEAD_DIM = 128
GLA_CHUNK = 64
GLA_SUB = 8
GLA_BLOCK = 4096
SUBLANES = 8
BF16_ROWS = 16
HGRN_PROJ_SLAB = 256
FFT_N2 = 128
DFT_K1_BLOCK = 8
HY_FILTER_PAD = 64
HY_MAX_DECAY = math.log(1e-2) / 0.3
HY_MIN_DECAY = math.log(1e-2) / 1.5
VMEM_LIMIT = 56 * 1024 * 1024


def _params(*sem):
    return pltpu.CompilerParams(dimension_semantics=sem, vmem_limit_bytes=VMEM_LIMIT)


def _tile(n, target, unit):
    t = min(n, target) // unit * unit
    while n % t:
        t -= unit
    return t


def _rms_mod(x, g, scale, shift):
    ms = jnp.mean(x * x, axis=-1, keepdims=True)
    y = x * lax.rsqrt(ms + NORM_EPS) * g
    return y * (1.0 + scale) + shift


MATMUL_NN = (((1,), (0,)), ((), ()))
MATMUL_NT = (((1,), (1,)), ((), ()))
MATMUL_TN = (((0,), (0,)), ((), ()))


def _dot3(a, w, dims):
    def split(x):
        hi = x.astype(BF16)
        return hi, (x - hi.astype(F32)).astype(BF16)

    a_hi, a_lo = split(a)
    w_hi, w_lo = split(w)
    mm = functools.partial(lax.dot_general, dimension_numbers=dims, preferred_element_type=F32)
    return mm(a_hi, w_hi) + mm(a_hi, w_lo) + mm(a_lo, w_hi)


def _adaln_kernel(s_ref, w_ref, b_ref, o_ref):
    s = s_ref[...]
    s = s * jax.nn.sigmoid(s)
    o_ref[...] = _dot3(s, w_ref[...], MATMUL_NN) + b_ref[...]


def adaln_all(c, ada_w, ada_b):
    depth, two, d, d3 = ada_w.shape
    n = depth * two
    b = c.shape[0]
    rows = -(-b // SUBLANES) * SUBLANES
    cp = jnp.zeros((rows, d), F32).at[:b].set(c)
    tn = _tile(d3, 1536, 128)
    out = pl.pallas_call(
        _adaln_kernel,
        grid=(n, d3 // tn),
        in_specs=[
            pl.BlockSpec((rows, d), lambda i, j: (0, 0)),
            pl.BlockSpec((None, d, tn), lambda i, j: (i, 0, j)),
            pl.BlockSpec((None, 1, tn), lambda i, j: (i, 0, j)),
        ],
        out_specs=pl.BlockSpec((None, rows, tn), lambda i, j: (i, 0, j)),
        out_shape=jax.ShapeDtypeStruct((n, rows, d3), F32),
        compiler_params=_params("parallel", "parallel"),
        name="adaln",
    )(cp, ada_w.reshape(n, d, d3), ada_b.reshape(n, 1, d3))
    return out[:, :b]


def _resident(shape):
    return pl.BlockSpec(shape, lambda i: (0,) * len(shape), pipeline_mode=pl.Buffered(1))


def _out_mlp_kernel(a_ref, wo_ref, x_ref, gm_ref, g_ref, sc_ref, sh_ref, gate_ref, w1_ref, w2_ref,
                    fg_ref, o_ref, hn_ref, acc_ref, *, final_norm, tf):
    o_ref[...] = x_ref[...] + gm_ref[...] * jnp.dot(a_ref[...], wo_ref[...].astype(BF16),
                                                     preferred_element_type=F32)
    hn_ref[...] = _rms_mod(o_ref[...], g_ref[...], sc_ref[...], sh_ref[...]).astype(BF16)
    for k in range(w1_ref.shape[1] // tf):
        cols = slice(k * tf, (k + 1) * tf)
        h = jnp.dot(hn_ref[...], w1_ref[:, cols].astype(BF16), preferred_element_type=F32)
        h = jnp.square(jnp.maximum(h, 0.0)).astype(BF16)
        part = jnp.dot(h, w2_ref[cols, :].astype(BF16), preferred_element_type=F32)
        if k == 0:
            acc_ref[...] = part
        else:
            acc_ref[...] += part
    out = o_ref[...] + gate_ref[...] * acc_ref[...]
    if final_norm:
        ms = jnp.mean(out * out, axis=-1, keepdims=True)
        out = out * lax.rsqrt(ms + NORM_EPS) * fg_ref[...]
    o_ref[...] = out


def mixer_out_mlp(a, w_out, x, gate_mix, g, scale, shift, gate, w1, w2, final_g, seq, final_norm):
    m, d = x.shape
    dff = w1.shape[1]
    tm = min(512, seq)
    tf = _tile(dff, 1024, 128)
    per_seq = seq // tm
    vec = pl.BlockSpec((None, 1, d), lambda i: (i // per_seq, 0, 0))
    return pl.pallas_call(
        functools.partial(_out_mlp_kernel, final_norm=final_norm, tf=tf),
        grid=(m // tm,),
        in_specs=[
            pl.BlockSpec((tm, d), lambda i: (i, 0)),
            _resident((d, d)),
            pl.BlockSpec((tm, d), lambda i: (i, 0)),
            vec,
            _resident((1, d)),
            vec, vec, vec,
            _resident((d, dff)),
            _resident((dff, d)),
            _resident((1, d)),
        ],
        out_specs=pl.BlockSpec((tm, d), lambda i: (i, 0)),
        out_shape=jax.ShapeDtypeStruct((m, d), F32),
        scratch_shapes=[pltpu.VMEM((tm, d), BF16), pltpu.VMEM((tm, d), F32)],
        compiler_params=_params("parallel"),
        name="mixer_out_mlp",
    )(a, w_out, x, gate_mix, g.reshape(1, d), scale, shift, gate, w1, w2, final_g.reshape(1, d))


def _hgrn_proj_kernel(x_ref, g_ref, sc_ref, sh_ref, w_ref, lb_ref, q_ref, v_ref, gt_ref, dec_ref,
                      hn_ref):
    d = q_ref.shape[1]
    hn_ref[...] = _rms_mod(x_ref[...], g_ref[...], sc_ref[...], sh_ref[...]).astype(BF16)

    def silu(a):
        half = 0.5 * a
        return half + half * jnp.tanh(half)

    slab = min(d, HGRN_PROJ_SLAB)
    for c0 in range(0, d, slab):
        cols = slice(c0, c0 + slab)

        def proj(group):
            return jnp.dot(hn_ref[...], w_ref[:, group * d + c0:group * d + c0 + slab],
                           preferred_element_type=F32)

        q_ref[:, cols] = silu(proj(0)).astype(BF16)
        v_ref[:, cols] = proj(1).astype(BF16)
        for dirn in range(2):
            z = proj(2 + dirn)
            lb = lb_ref[dirn:dirn + 1, cols]
            one_minus_lb = 1.0 - lb
            dec_ref[:, 2 * dirn * d + c0:2 * dirn * d + c0 + slab] = jnp.log2(
                lb + one_minus_lb * (1.0 / (1.0 + jnp.exp(-z))))
            dec_ref[:, (2 * dirn + 1) * d + c0:(2 * dirn + 1) * d + c0 + slab] = (
                jnp.log2(one_minus_lb) - jnp.log2(1.0 + jnp.exp(z)))
        gt_ref[:, cols] = silu(proj(4)).astype(BF16)


def hgrn_projection(x, g, scale, shift, w, lbs, seq):
    m, d = x.shape
    tm = min(512, seq)
    per_seq = seq // tm
    row = lambda width: pl.BlockSpec((tm, width), lambda i: (i, 0))
    return pl.pallas_call(
        _hgrn_proj_kernel,
        grid=(m // tm,),
        in_specs=[
            row(d),
            _resident((1, d)),
            pl.BlockSpec((None, 1, d), lambda i: (i // per_seq, 0, 0)),
            pl.BlockSpec((None, 1, d), lambda i: (i // per_seq, 0, 0)),
            _resident((d, 5 * d)),
            _resident((2, d)),
        ],
        out_specs=[row(d), row(d), row(d), row(4 * d)],
        out_shape=[jax.ShapeDtypeStruct((m, d), BF16), jax.ShapeDtypeStruct((m, d), BF16),
                   jax.ShapeDtypeStruct((m, d), BF16), jax.ShapeDtypeStruct((m, 4 * d), F32)],
        scratch_shapes=[pltpu.VMEM((tm, d), BF16)],
        compiler_params=_params("parallel"),
        name="hgrn_projection",
    )(x, g.reshape(1, d), scale, shift, w, lbs)


def _gla_kernel(*refs, rev, n_chunks):
    if rev:
        (q_ref, v_ref, g2_ref, lk2_ref, tri_ref, e_ref, of_ref, gt_ref, ng_ref, o_ref,
         st_s, b2_s, gam_s, qe_s, dec_s, upd_s, seen_s, att_s, p_s, dg_s) = refs
    else:
        (q_ref, v_ref, g2_ref, lk2_ref, tri_ref, e_ref, o_ref,
         st_s, b2_s, gam_s, qe_s, dec_s, upd_s, seen_s, att_s, p_s, dg_s) = refs
    c_len, sub, pack = GLA_CHUNK, GLA_SUB, BF16_ROWS
    n_sub = c_len // sub
    nt, tn = MATMUL_NT, MATMUL_TN

    @pl.when(pl.program_id(2) == 0)
    def _():
        st_s[...] = jnp.zeros_like(st_s)

    tri2 = tri_ref[...]

    def sums(ci):
        rows = slice(ci * c_len, (ci + 1) * c_len)
        g2 = g2_ref[rows, :]
        hi = g2.astype(BF16)
        lo = (g2 - hi.astype(F32)).astype(BF16)
        b2 = jnp.dot(tri2, jnp.concatenate([hi, lo], axis=0), preferred_element_type=F32)
        b2_s[rows, :] = b2
        gam_s[rows, :] = lk2_ref[rows, :] - b2

    def padded_rows(x, lo, hi, total):
        lo_t, hi_t = lo // pack * pack, -(-hi // pack) * pack
        parts = ([jnp.zeros((lo - lo_t, HEAD_DIM), F32)] if lo > lo_t else []) + [x]
        if hi_t > hi:
            parts.append(jnp.zeros((hi_t - hi, HEAD_DIM), F32))
        tiles = [jnp.concatenate(parts, axis=0).astype(BF16)]
        if lo_t:
            tiles.insert(0, jnp.zeros((lo_t, HEAD_DIM), BF16))
        if total > hi_t:
            tiles.append(jnp.zeros((total - hi_t, HEAD_DIM), BF16))
        return jnp.concatenate(tiles, axis=0)

    def scores(ci):
        off = ci * c_len
        rows = slice(off, off + c_len)
        q = q_ref[rows, :].astype(F32)
        b2 = b2_s[rows, :]
        gam = gam_s[rows, :]
        b2_end = b2[0:1] if rev else b2[c_len - 1:c_len]
        qe_s[rows, :] = (q * jnp.exp2(b2)).astype(BF16)
        upd_s[ci] = lax.dot_general(v_ref[rows, :], jnp.exp2(b2_end + gam).astype(BF16), tn,
                                    preferred_element_type=F32)
        dec_s[ci] = jnp.exp2(b2_end)
        atts = []
        for i in range(n_sub):
            r0 = sub * i
            q_i, b_i = q[r0:r0 + sub], b2[r0:r0 + sub]
            if rev:
                others = (r0 + sub, c_len)
                edge = b2[r0 + sub:r0 + sub + 1] if i < n_sub - 1 else None
            else:
                others = (0, r0)
                edge = b2[r0 - 1:r0] if i > 0 else None
            if edge is None:
                atts.append(jnp.zeros((sub, HEAD_DIM), F32))
            else:
                qt = padded_rows(q_i * jnp.exp2(b_i - edge), 0, sub, pack)
                kt = padded_rows(jnp.exp2(edge + gam[others[0]:others[1]]), others[0], others[1],
                                 HEAD_DIM)
                atts.append(lax.dot_general(qt, kt, nt, preferred_element_type=F32)[:sub])
        att_s[rows, :] = jnp.concatenate(atts, axis=0)
        for r0 in range(0, c_len, pack):
            q_t, b_t = q[r0:r0 + pack], b2[r0:r0 + pack]
            for sl in range(sub):
                row0 = off + r0 + sl
                g_rows = jnp.concatenate(
                    [jnp.broadcast_to(gam_s[row0 + u * sub:row0 + u * sub + 1, :], (sub, HEAD_DIM))
                     for u in range(pack // sub)], axis=0)
                tile = q_t * jnp.exp2(jnp.minimum(b_t + g_rows, 0.0))
                p_s[off + r0:off + r0 + pack, sl * HEAD_DIM:(sl + 1) * HEAD_DIM] = tile.astype(BF16)

    row = lax.broadcasted_iota(jnp.int32, (c_len, HEAD_DIM), 0)
    col = lax.broadcasted_iota(jnp.int32, (c_len, HEAD_DIM), 1)
    keep = (col >= row) if rev else (col <= row)
    v_pad = jnp.zeros((HEAD_DIM - c_len, HEAD_DIM), BF16)

    def outputs(ci):
        rows = slice(ci * c_len, (ci + 1) * c_len)
        dg = dg_s[rows, :]
        diag = [dg[0:sub]] + [pltpu.roll(dg[sub * i:sub * (i + 1)], sub * i, 1)
                              for i in range(1, n_sub)]
        att = att_s[rows, :] + jnp.concatenate(diag, axis=0)
        att = jnp.where(keep, att, 0.0).astype(BF16)
        o = (jnp.dot(att, jnp.concatenate([v_ref[rows, :], v_pad], axis=0),
                     preferred_element_type=F32)
             + lax.dot_general(qe_s[rows, :], seen_s[ci], nt, preferred_element_type=F32))
        if rev:
            tot = of_ref[rows, :] + o
            ms = jnp.mean(tot * tot, axis=-1, keepdims=True)
            y = tot * lax.rsqrt(ms + NORM_EPS) * ng_ref[...] * gt_ref[rows, :].astype(F32)
            o_ref[rows, :] = y.astype(o_ref.dtype)
        else:
            o_ref[rows, :] = o

    for ci in range(n_chunks):
        sums(ci)
    for ci in range(n_chunks):
        scores(ci)
    dg_s[...] = jnp.dot(p_s[...], e_ref[...], preferred_element_type=F32)
    st = st_s[...]
    for ci in (reversed(range(n_chunks)) if rev else range(n_chunks)):
        seen_s[ci] = st.astype(BF16)
        st = st * dec_s[ci] + upd_s[ci]
    st_s[...] = st
    for ci in range(n_chunks):
        outputs(ci)


def _gla_tables(rev):
    idx = jnp.arange(GLA_CHUNK)
    tri = (idx[:, None] <= idx[None, :]) if rev else (idx[:, None] >= idx[None, :])
    tri2 = jnp.concatenate([tri, tri], axis=1).astype(BF16)
    s_of_row = jnp.arange(GLA_SUB * HEAD_DIM) // HEAD_DIM
    place = jnp.arange(HEAD_DIM)[None, :] == s_of_row[:, None]
    return tri2, place.astype(BF16)


def gla_direction(q, v, gate, dec, norm_g, o_fwd, batch, seq, rev):
    m, d = q.shape
    heads = d // HEAD_DIM
    t_blk = min(GLA_BLOCK, seq)
    n_t = seq // t_blk
    n_chunks = t_blk // GLA_CHUNK
    tri2, place = _gla_tables(rev)

    def rowblk(b, h, t):
        return b * n_t + ((n_t - 1 - t) if rev else t)

    def col(group):
        return pl.BlockSpec((t_blk, HEAD_DIM), lambda b, h, t: (rowblk(b, h, t), group * heads + h))

    in_specs = [col(0), col(0), col(2 if rev else 0), col(3 if rev else 1),
                pl.BlockSpec(tri2.shape, lambda b, h, t: (0, 0)),
                pl.BlockSpec(place.shape, lambda b, h, t: (0, 0))]
    args = [q, v, dec, dec, tri2, place]
    if rev:
        in_specs += [col(0), col(0), pl.BlockSpec((1, HEAD_DIM), lambda b, h, t: (0, 0))]
        args += [o_fwd, gate, norm_g.reshape(1, HEAD_DIM)]
    return pl.pallas_call(
        functools.partial(_gla_kernel, rev=rev, n_chunks=n_chunks),
        grid=(batch, heads, n_t),
        in_specs=in_specs,
        out_specs=col(0),
        out_shape=jax.ShapeDtypeStruct((m, d), BF16 if rev else F32),
        scratch_shapes=[pltpu.VMEM((HEAD_DIM, HEAD_DIM), F32),
                        pltpu.VMEM((t_blk, HEAD_DIM), F32),
                        pltpu.VMEM((t_blk, HEAD_DIM), F32),
                        pltpu.VMEM((t_blk, HEAD_DIM), BF16),
                        pltpu.VMEM((n_chunks, 1, HEAD_DIM), F32),
                        pltpu.VMEM((n_chunks, HEAD_DIM, HEAD_DIM), F32),
                        pltpu.VMEM((n_chunks, HEAD_DIM, HEAD_DIM), BF16),
                        pltpu.VMEM((t_blk, HEAD_DIM), F32),
                        pltpu.VMEM((t_blk, GLA_SUB * HEAD_DIM), BF16),
                        pltpu.VMEM((t_blk, HEAD_DIM), F32)],
        compiler_params=_params("parallel", "parallel", "arbitrary"),
        name="gla_rev" if rev else "gla_fwd",
    )(*args)


def _hyena_proj_kernel(x_ref, xp_ref, xn_ref, g_ref, sc_ref, sh_ref, w_ref, b_ref, cw_ref, cb_ref,
                       x0_ref, z_ref, hn_ref, u_ref, *, per_seq, cs):
    i = pl.program_id(0)
    tm, d = x_ref.shape
    halo = xp_ref.shape[0]
    keep_prev = jnp.where(i % per_seq == 0, 0.0, 1.0)
    keep_next = jnp.where(i % per_seq == per_seq - 1, 0.0, 1.0)

    def norm(ref):
        return _rms_mod(ref[...], g_ref[...], sc_ref[...], sh_ref[...]).astype(BF16)

    hn_ref[0:halo, :] = norm(xp_ref)
    hn_ref[halo:halo + tm, :] = norm(x_ref)
    hn_ref[halo + tm:, :] = norm(xn_ref)
    row = lax.broadcasted_iota(jnp.int32, (tm + 2 * halo, cs), 0)
    keep = jnp.where(row < halo, keep_prev, jnp.where(row >= halo + tm, keep_next, 1.0))

    for c0 in range(0, d, cs):
        def conv(group):
            cols = slice(group * d + c0, group * d + c0 + cs)
            u = jnp.dot(hn_ref[...], w_ref[:, cols], preferred_element_type=F32) + b_ref[:, cols]
            u_ref[group] = u * keep
            return (u_ref[group, halo - 1:halo - 1 + tm, :] * cw_ref[0:1, cols]
                    + u_ref[group, halo:halo + tm, :] * cw_ref[1:2, cols]
                    + u_ref[group, halo + 1:halo + 1 + tm, :] * cw_ref[2:3, cols]
                    + cb_ref[:, cols])

        x0_ref[:, c0:c0 + cs] = conv(0).astype(x0_ref.dtype)
        z_ref[:, c0:c0 + cs] = (conv(1) * conv(2)).astype(z_ref.dtype)


def hyena_projection(x, g, scale, shift, w, bias, conv_w, conv_b, seq):
    m, d = x.shape
    tm = min(512, seq)
    per_seq = seq // tm
    halo = BF16_ROWS
    per_tile = tm // halo
    n_halo = m // halo
    cs = _tile(d, 512, 128)
    vec = pl.BlockSpec((None, 1, d), lambda i: (i // per_seq, 0, 0))
    return pl.pallas_call(
        functools.partial(_hyena_proj_kernel, per_seq=per_seq, cs=cs),
        grid=(m // tm,),
        in_specs=[
            pl.BlockSpec((tm, d), lambda i: (i, 0)),
            pl.BlockSpec((halo, d), lambda i: (jnp.maximum(i * per_tile - 1, 0), 0)),
            pl.BlockSpec((halo, d), lambda i: (jnp.minimum((i + 1) * per_tile, n_halo - 1), 0)),
            _resident((1, d)),
            vec, vec,
            _resident((d, 3 * d)),
            _resident((1, 3 * d)),
            _resident((3, 3 * d)),
            _resident((1, 3 * d)),
        ],
        out_specs=[pl.BlockSpec((tm, d), lambda i: (i, 0)),
                   pl.BlockSpec((tm, d), lambda i: (i, 0))],
        out_shape=[jax.ShapeDtypeStruct((m, d), BF16), jax.ShapeDtypeStruct((m, d), BF16)],
        scratch_shapes=[pltpu.VMEM((tm + 2 * halo, d), BF16),
                        pltpu.VMEM((3, tm + 2 * halo, cs), F32)],
        compiler_params=_params("parallel"),
        name="hyena_projection",
    )(x, x, x, g.reshape(1, d), scale, shift, w, bias.reshape(1, 3 * d), conv_w,
      conv_b.reshape(1, 3 * d))


def _filter_kernel(emb_ref, t_ref, keep_ref, win_ref, wmid_ref, b_ref, freq_ref, wout_ref,
                   delta_ref, o_ref):
    freq = freq_ref[...]
    h = jnp.sin(freq * (_dot3(win_ref[...], emb_ref[...], MATMUL_NN) + b_ref[:, 0:1]))
    for mth in range(wmid_ref.shape[0]):
        h = jnp.sin(freq * (_dot3(wmid_ref[mth], h, MATMUL_NN) + b_ref[:, mth + 1:mth + 2]))
    hf = _dot3(h, wout_ref[...], MATMUL_TN)
    o_ref[...] = hf * jnp.exp(-t_ref[...] * delta_ref[...]) * keep_ref[...]


def hyena_filter_circular(seq, d, w_in, w_mid, b, freq, w_out):
    emb_dim, width = w_in.shape
    bands = (emb_dim - 1) // 2
    mrow = jnp.arange(2 * seq)
    pos = jnp.where(mrow < seq, mrow, jnp.where(mrow == seq, 0, 2 * seq - mrow)).astype(F32)[:, None]
    t_c = pos / (seq - 1.0)
    band = jnp.linspace(1e-4, bands - 1.0, bands, dtype=F32)
    ang = (2.0 * math.pi / seq) * pos * band
    emb_c = jnp.concatenate([t_c, jnp.cos(ang), -jnp.sin(ang)], axis=-1)
    emb_c = jnp.pad(emb_c, ((0, 0), (0, HY_FILTER_PAD - emb_dim)))
    keep = (mrow != seq).astype(F32)[:, None]
    w_in_p = jnp.pad(w_in.astype(F32), ((0, HY_FILTER_PAD - emb_dim), (0, 0)))
    deltas = jnp.abs(jnp.linspace(HY_MIN_DECAY, HY_MAX_DECAY, d, dtype=F32)).reshape(1, d)
    tr = min(512, seq)
    half = seq // tr
    n_mid = w_mid.shape[0]
    return pl.pallas_call(
        _filter_kernel,
        grid=(2 * seq // tr,),
        in_specs=[
            pl.BlockSpec((HY_FILTER_PAD, tr), lambda i: (0, i)),
            pl.BlockSpec((tr, 1), lambda i: (i, 0)),
            pl.BlockSpec((tr, 1), lambda i: (i, 0)),
            pl.BlockSpec((width, HY_FILTER_PAD), lambda i: (0, 0)),
            pl.BlockSpec((n_mid, width, width), lambda i: (0, 0, 0)),
            pl.BlockSpec((width, n_mid + 1), lambda i: (0, 0)),
            pl.BlockSpec((width, 1), lambda i: (0, 0)),
            pl.BlockSpec((width, d), lambda i: (0, i // half)),
            pl.BlockSpec((1, d), lambda i: (0, 0)),
        ],
        out_specs=pl.BlockSpec((tr, d), lambda i: (i, 0)),
        out_shape=jax.ShapeDtypeStruct((2 * seq, d), F32),
        compiler_params=_params("parallel"),
        name="hyena_filter",
    )(emb_c.T, t_c, keep, w_in_p.T, jnp.swapaxes(w_mid.astype(F32), 1, 2), b.astype(F32).T,
      freq.astype(F32).reshape(width, 1), w_out.astype(F32), deltas)


def _swap_major_sublane(x):
    return jnp.swapaxes(x, 0, 1)


def _dft_stage1_kernel(m_ref, x_ref, o_ref):
    x32 = x_ref[...].astype(F32)
    outs = []
    for g in range(x32.shape[1] // SUBLANES):
        xt = _swap_major_sublane(x32[:, g * SUBLANES:(g + 1) * SUBLANES, :]).astype(BF16)
        r = jnp.stack([jnp.dot(m_ref[...], xt[j], preferred_element_type=F32)
                       for j in range(SUBLANES)], axis=0)
        outs.append(_swap_major_sublane(r))
    o_ref[...] = jnp.concatenate(outs, axis=1).astype(o_ref.dtype)


def dft_stage1(mat, x):
    r, kdim = mat.shape
    _, n2, d = x.shape
    cw = _tile(d, 512, 128)
    return pl.pallas_call(
        _dft_stage1_kernel,
        grid=(n2 // BF16_ROWS, d // cw),
        in_specs=[pl.BlockSpec((r, kdim), lambda j, c: (0, 0)),
                  pl.BlockSpec((kdim, BF16_ROWS, cw), lambda j, c: (0, j, c))],
        out_specs=pl.BlockSpec((r, BF16_ROWS, cw), lambda j, c: (0, j, c)),
        out_shape=jax.ShapeDtypeStruct((r, n2, d), BF16),
        compiler_params=_params("parallel", "parallel"),
        name="dft_stage1",
    )(mat, x)


def _stacked(a_ref, u):
    return jnp.concatenate([a_ref[0, u], a_ref[1, u]], axis=0)


def _spectrum_kernel(g_ref, a_ref, o_ref):
    n2 = a_ref.shape[2]
    for u in range(a_ref.shape[1]):
        h = jnp.dot(g_ref[u], _stacked(a_ref, u), preferred_element_type=F32).astype(o_ref.dtype)
        o_ref[u, 0] = h[:n2]
        o_ref[u, 1] = h[n2:]


def filter_spectrum(g_tab, a):
    _, n1, n2, d = a.shape
    kb = _tile(n1, DFT_K1_BLOCK, 1)
    return pl.pallas_call(
        _spectrum_kernel,
        grid=(n1 // kb,),
        in_specs=[pl.BlockSpec((kb, 2 * n2, 2 * n2), lambda k: (k, 0, 0)),
                  pl.BlockSpec((2, kb, n2, d), lambda k: (0, k, 0, 0))],
        out_specs=pl.BlockSpec((kb, 2, n2, d), lambda k: (k, 0, 0, 0)),
        out_shape=jax.ShapeDtypeStruct((n1, 2, n2, d), BF16),
        compiler_params=_params("parallel"),
        name="filter_spectrum",
    )(g_tab, a)


def _freq_mul_kernel(g_ref, a_ref, h_ref, o_ref):
    n2 = a_ref.shape[2]
    tn = MATMUL_TN
    for u in range(a_ref.shape[1]):
        x = jnp.dot(g_ref[u], _stacked(a_ref, u), preferred_element_type=F32)
        xr, xi = x[:n2], x[n2:]
        hr, hi = h_ref[u, 0].astype(F32), h_ref[u, 1].astype(F32)
        p = jnp.concatenate([xr * hr - xi * hi, xr * hi + xi * hr], axis=0).astype(BF16)
        t = lax.dot_general(g_ref[u], p, tn, preferred_element_type=F32).astype(o_ref.dtype)
        o_ref[0, u] = t[:n2]
        o_ref[1, u] = t[n2:]


def freq_multiply(g_tab, a, h):
    _, n1, n2, d = a.shape
    kb = _tile(n1, DFT_K1_BLOCK, 1)
    return pl.pallas_call(
        _freq_mul_kernel,
        grid=(n1 // kb,),
        in_specs=[pl.BlockSpec((kb, 2 * n2, 2 * n2), lambda k: (k, 0, 0)),
                  pl.BlockSpec((2, kb, n2, d), lambda k: (0, k, 0, 0)),
                  pl.BlockSpec((kb, 2, n2, d), lambda k: (k, 0, 0, 0))],
        out_specs=pl.BlockSpec((2, kb, n2, d), lambda k: (0, k, 0, 0)),
        out_shape=jax.ShapeDtypeStruct((2, n1, n2, d), BF16),
        compiler_params=_params("parallel"),
        name="freq_multiply",
    )(g_tab, a, h)


def _conv_out_kernel(m_ref, t_ref, x0_ref, z_ref, skip_ref, o_ref):
    t32 = t_ref[...].astype(F32)
    ys = []
    for g in range(t32.shape[1] // SUBLANES):
        tt = _swap_major_sublane(t32[:, g * SUBLANES:(g + 1) * SUBLANES, :]).astype(BF16)
        y = jnp.stack([jnp.dot(m_ref[...], tt[j], preferred_element_type=F32)
                       for j in range(SUBLANES)], axis=0)
        ys.append(_swap_major_sublane(y))
    y = jnp.concatenate(ys, axis=1)
    o_ref[...] = (x0_ref[...].astype(F32)
                  * (y + skip_ref[...] * z_ref[...].astype(F32))).astype(o_ref.dtype)


def conv_output(mat, t, x0, z, skip):
    r, kdim = mat.shape
    _, n2, d = t.shape
    cw = _tile(d, 256, 128)
    blk = lambda rows: pl.BlockSpec((rows, BF16_ROWS, cw), lambda j, c: (0, j, c))
    return pl.pallas_call(
        _conv_out_kernel,
        grid=(n2 // BF16_ROWS, d // cw),
        in_specs=[pl.BlockSpec((r, kdim), lambda j, c: (0, 0)),
                  blk(kdim), blk(r), blk(r),
                  pl.BlockSpec((1, cw), lambda j, c: (0, c))],
        out_specs=blk(r),
        out_shape=jax.ShapeDtypeStruct((r, n2, d), BF16),
        compiler_params=_params("parallel", "parallel"),
        name="dft_inverse_out",
    )(mat, t, x0, z, skip)


def _dft_tables(seq):
    n = 2 * seq
    n2 = FFT_N2
    n1 = n // n2
    lh = seq // n2

    def cs(idx, mod):
        ang = (2.0 * math.pi / mod) * (idx % mod).astype(F32)
        return jnp.cos(ang), jnp.sin(ang)

    k1 = jnp.arange(n1)
    cr, sr = cs(k1[:, None] * jnp.arange(lh)[None, :], n1)
    f_sig = jnp.block([[cr, sr], [-sr, cr]])
    cf, sf = cs(k1[:, None] * k1[None, :], n1)
    f_filt = jnp.concatenate([cf, -sf], axis=0)
    ca, sa = cs(k1[:, None] * jnp.arange(n2)[None, :], n)
    cb, sb = cs(jnp.arange(n2)[:, None] * jnp.arange(n2)[None, :], n2)
    gr = ca[:, None, :] * cb[None] - sa[:, None, :] * sb[None]
    gs = sa[:, None, :] * cb[None] + ca[:, None, :] * sb[None]
    g_tab = jnp.concatenate([jnp.concatenate([gr, gs], axis=2),
                             jnp.concatenate([-gs, gr], axis=2)], axis=1)
    er, es = cs(jnp.arange(lh)[:, None] * k1[None, :], n1)
    f_inv = jnp.block([[er, -es], [es, er]]) / n
    return f_sig.astype(BF16), f_filt.astype(BF16), g_tab.astype(BF16), f_inv.astype(BF16)


def hyena_long_conv(x0, z, filt, skip, tables, batch, seq):
    assert batch == 2, "the two sequences ride as real / imaginary parts of one complex signal"
    f_sig, f_filt, g_tab, f_inv = tables
    m, d = z.shape
    n2 = FFT_N2
    n1 = 2 * seq // n2
    lh = seq // n2
    a_f = dft_stage1(f_filt, filt.reshape(n1, n2, d))
    h = filter_spectrum(g_tab, a_f.reshape(2, n1, n2, d))
    a = dft_stage1(f_sig, z.reshape(batch * lh, n2, d))
    t = freq_multiply(g_tab, a.reshape(2, n1, n2, d), h)
    y = conv_output(f_inv, t.reshape(2 * n1, n2, d), x0.reshape(batch * lh, n2, d),
                    z.reshape(batch * lh, n2, d), skip.astype(F32).reshape(1, d))
    return y.reshape(m, d)


def kernel(x, c, ada_w, ada_b, norm_g, hg_w_in, hg_lower_bounds, hg_norm_g, hg_w_out,
           hy_w_in, hy_b_in, hy_conv_w, hy_conv_b, hy_filt_w_in, hy_filt_w_mid, hy_filt_b,
           hy_filt_freq, hy_filt_w_out, hy_skip, hy_w_out, mlp_w1, mlp_w2, final_g):
    batch, seq, d = x.shape
    depth = ada_w.shape[0]
    n_mixers = 2
    assert d % HEAD_DIM == 0 and seq % max(GLA_CHUNK, FFT_N2) == 0

    mod = adaln_all(c, ada_w, ada_b)
    mod = mod.reshape(2 * depth, batch, 3, 1, d)

    lbs = jax.nn.softmax(hg_lower_bounds.astype(F32), axis=1)
    lbs = jnp.cumsum(lbs, axis=1) - lbs[:, :1]

    tables = _dft_tables(seq)

    xf = x.reshape(batch * seq, d)
    for i in range(depth):
        j = i // n_mixers
        shift, scale, gate = mod[2 * i, :, 0], mod[2 * i, :, 1], mod[2 * i, :, 2]
        if i % n_mixers == 0:
            q, v, gt, dec = hgrn_projection(xf, norm_g[i, 0], scale, shift, hg_w_in[j].astype(BF16),
                                            lbs[:, j], seq)
            o_f = gla_direction(q, v, gt, dec, hg_norm_g[j], None, batch, seq, rev=False)
            mix = gla_direction(q, v, gt, dec, hg_norm_g[j], o_f, batch, seq, rev=True)
            w_out = hg_w_out[j]
        else:
            x0, z = hyena_projection(xf, norm_g[i, 0], scale, shift, hy_w_in[j].astype(BF16),
                                     hy_b_in[j], hy_conv_w[j], hy_conv_b[j], seq)
            filt = hyena_filter_circular(seq, d, hy_filt_w_in[j], hy_filt_w_mid[j], hy_filt_b[j],
                                         hy_filt_freq[j], hy_filt_w_out[j])
            mix = hyena_long_conv(x0, z, filt, hy_skip[j], tables, batch, seq)
            w_out = hy_w_out[j]
        gate_mix = gate
        shift, scale, gate = mod[2 * i + 1, :, 0], mod[2 * i + 1, :, 1], mod[2 * i + 1, :, 2]
        xf = mixer_out_mlp(mix, w_out, xf, gate_mix, norm_g[i, 1], scale, shift, gate,
                           mlp_w1[i], mlp_w2[i], final_g, seq, final_norm=(i == depth - 1))
    return xf.reshape(batch, seq, d)
```

```python
import functools
import math

import jax
import jax.numpy as jnp
from jax import lax
from jax.experimental import pallas as pl
from jax.experimental.pallas import tpu as pltpu

F32 = jnp.float32
BF16 = jnp.bfloat16

NORM_EPS = 1e-6
HEAD_DIM = 128
GLA_CHUNK = 64
GLA_SUB = 8
GLA_BLOCK = 4096
SUBLANES = 8
BF16_ROWS = 16
HGRN_PROJ_SLAB = 256
FFT_N2 = 128
DFT_K1_BLOCK = 8
HY_FILTER_PAD = 64
HY_MAX_DECAY = math.log(1e-2) / 0.3
HY_MIN_DECAY = math.log(1e-2) / 1.5
VMEM_LIMIT = 56 * 1024 * 1024


def _params(*sem):
    return pltpu.CompilerParams(dimension_semantics=sem, vmem_limit_bytes=VMEM_LIMIT)


def _tile(n, target, unit):
    t = min(n, target) // unit * unit
    while n % t:
        t -= unit
    return t


def _rms_mod(x, g, scale, shift):
    ms = jnp.mean(x * x, axis=-1, keepdims=True)
    y = x * lax.rsqrt(ms + NORM_EPS) * g
    return y * (1.0 + scale) + shift


MATMUL_NN = (((1,), (0,)), ((), ()))
MATMUL_NT = (((1,), (1,)), ((), ()))
MATMUL_TN = (((0,), (0,)), ((), ()))


def _dot3(a, w, dims):
    def split(x):
        hi = x.astype(BF16)
        return hi, (x - hi.astype(F32)).astype(BF16)

    a_hi, a_lo = split(a)
    w_hi, w_lo = split(w)
    mm = functools.partial(lax.dot_general, dimension_numbers=dims, preferred_element_type=F32)
    return mm(a_hi, w_hi) + mm(a_hi, w_lo) + mm(a_lo, w_hi)


def _adaln_kernel(s_ref, w_ref, b_ref, o_ref):
    s = s_ref[...]
    s = s * jax.nn.sigmoid(s)
    o_ref[...] = _dot3(s, w_ref[...], MATMUL_NN) + b_ref[...]


def adaln_all(c, ada_w, ada_b):
    depth, two, d, d3 = ada_w.shape
    n = depth * two
    b = c.shape[0]
    rows = -(-b // SUBLANES) * SUBLANES
    cp = jnp.zeros((rows, d), F32).at[:b].set(c)
    tn = _tile(d3, 1536, 128)
    out = pl.pallas_call(
        _adaln_kernel,
        grid=(n, d3 // tn),
        in_specs=[
            pl.BlockSpec((rows, d), lambda i, j: (0, 0)),
            pl.BlockSpec((None, d, tn), lambda i, j: (i, 0, j)),
            pl.BlockSpec((None, 1, tn), lambda i, j: (i, 0, j)),
        ],
        out_specs=pl.BlockSpec((None, rows, tn), lambda i, j: (i, 0, j)),
        out_shape=jax.ShapeDtypeStruct((n, rows, d3), F32),
        compiler_params=_params("parallel", "parallel"),
        name="adaln",
    )(cp, ada_w.reshape(n, d, d3), ada_b.reshape(n, 1, d3))
    return out[:, :b]


def _resident(shape):
    return pl.BlockSpec(shape, lambda i: (0,) * len(shape), pipeline_mode=pl.Buffered(1))


def _out_mlp_kernel(a_ref, wo_ref, x_ref, gm_ref, g_ref, sc_ref, sh_ref, gate_ref, w1_ref, w2_ref,
                    fg_ref, o_ref, hn_ref, acc_ref, *, final_norm, tf):
    o_ref[...] = x_ref[...] + gm_ref[...] * jnp.dot(a_ref[...], wo_ref[...].astype(BF16),
                                                     preferred_element_type=F32)
    hn_ref[...] = _rms_mod(o_ref[...], g_ref[...], sc_ref[...], sh_ref[...]).astype(BF16)
    for k in range(w1_ref.shape[1] // tf):
        cols = slice(k * tf, (k + 1) * tf)
        h = jnp.dot(hn_ref[...], w1_ref[:, cols].astype(BF16), preferred_element_type=F32)
        h = jnp.square(jnp.maximum(h, 0.0)).astype(BF16)
        part = jnp.dot(h, w2_ref[cols, :].astype(BF16), preferred_element_type=F32)
        if k == 0:
            acc_ref[...] = part
        else:
            acc_ref[...] += part
    out = o_ref[...] + gate_ref[...] * acc_ref[...]
    if final_norm:
        ms = jnp.mean(out * out, axis=-1, keepdims=True)
        out = out * lax.rsqrt(ms + NORM_EPS) * fg_ref[...]
    o_ref[...] = out


def mixer_out_mlp(a, w_out, j, x, gate_mix, g, scale, shift, gate, w1, w2, i, final_g, seq,
                  final_norm):
    m, d = x.shape
    dff = w1.shape[2]

    def layer(index, shape):
        return pl.BlockSpec((None,) + shape, lambda r: (index, 0, 0), pipeline_mode=pl.Buffered(1))

    tm = min(512, seq)
    tf = _tile(dff, 1024, 128)
    per_seq = seq // tm
    vec = pl.BlockSpec((None, 1, d), lambda r: (r // per_seq, 0, 0))
    return pl.pallas_call(
        functools.partial(_out_mlp_kernel, final_norm=final_norm, tf=tf),
        grid=(m // tm,),
        in_specs=[
            pl.BlockSpec((tm, d), lambda r: (r, 0)),
            layer(j, (d, d)),
            pl.BlockSpec((tm, d), lambda r: (r, 0)),
            vec,
            _resident((1, d)),
            vec, vec, vec,
            layer(i, (d, dff)),
            layer(i, (dff, d)),
            _resident((1, d)),
        ],
        out_specs=pl.BlockSpec((tm, d), lambda r: (r, 0)),
        out_shape=jax.ShapeDtypeStruct((m, d), F32),
        scratch_shapes=[pltpu.VMEM((tm, d), BF16), pltpu.VMEM((tm, d), F32)],
        compiler_params=_params("parallel"),
        name="mixer_out_mlp",
    )(a, w_out, x, gate_mix, g.reshape(1, d), scale, shift, gate, w1, w2, final_g.reshape(1, d))


def _hgrn_proj_kernel(x_ref, g_ref, sc_ref, sh_ref, w_ref, lb_ref, q_ref, v_ref, gt_ref, dec_ref,
                      hn_ref):
    d = q_ref.shape[1]
    hn_ref[...] = _rms_mod(x_ref[...], g_ref[...], sc_ref[...], sh_ref[...]).astype(BF16)

    def silu(a):
        half = 0.5 * a
        return half + half * jnp.tanh(half)

    slab = min(d, HGRN_PROJ_SLAB)
    for c0 in range(0, d, slab):
        cols = slice(c0, c0 + slab)

        def proj(group):
            return jnp.dot(hn_ref[...], w_ref[:, group * d + c0:group * d + c0 + slab],
                           preferred_element_type=F32)

        q_ref[:, cols] = silu(proj(0)).astype(BF16)
        v_ref[:, cols] = proj(1).astype(BF16)
        for dirn in range(2):
            z = proj(2 + dirn)
            lb = lb_ref[dirn:dirn + 1, cols]
            one_minus_lb = 1.0 - lb
            dec_ref[:, 2 * dirn * d + c0:2 * dirn * d + c0 + slab] = jnp.log2(
                lb + one_minus_lb * (1.0 / (1.0 + jnp.exp(-z))))
            dec_ref[:, (2 * dirn + 1) * d + c0:(2 * dirn + 1) * d + c0 + slab] = (
                jnp.log2(one_minus_lb) - jnp.log2(1.0 + jnp.exp(z)))
        gt_ref[:, cols] = silu(proj(4)).astype(BF16)


def hgrn_projection(x, g, scale, shift, w, lbs, seq):
    m, d = x.shape
    tm = min(512, seq)
    per_seq = seq // tm
    row = lambda width: pl.BlockSpec((tm, width), lambda i: (i, 0))
    return pl.pallas_call(
        _hgrn_proj_kernel,
        grid=(m // tm,),
        in_specs=[
            row(d),
            _resident((1, d)),
            pl.BlockSpec((None, 1, d), lambda i: (i // per_seq, 0, 0)),
            pl.BlockSpec((None, 1, d), lambda i: (i // per_seq, 0, 0)),
            _resident((d, 5 * d)),
            _resident((2, d)),
        ],
        out_specs=[row(d), row(d), row(d), row(4 * d)],
        out_shape=[jax.ShapeDtypeStruct((m, d), BF16), jax.ShapeDtypeStruct((m, d), BF16),
                   jax.ShapeDtypeStruct((m, d), BF16), jax.ShapeDtypeStruct((m, 4 * d), F32)],
        scratch_shapes=[pltpu.VMEM((tm, d), BF16)],
        compiler_params=_params("parallel"),
        name="hgrn_projection",
    )(x, g.reshape(1, d), scale, shift, w, lbs)


def _gla_kernel(*refs, rev, n_chunks):
    if rev:
        (q_ref, v_ref, g2_ref, lk2_ref, tri_ref, e_ref, of_ref, gt_ref, ng_ref, o_ref,
         st_s, b2_s, gam_s, qe_s, dec_s, upd_s, seen_s, att_s, p_s, dg_s) = refs
    else:
        (q_ref, v_ref, g2_ref, lk2_ref, tri_ref, e_ref, o_ref,
         st_s, b2_s, gam_s, qe_s, dec_s, upd_s, seen_s, att_s, p_s, dg_s) = refs
    c_len, sub, pack = GLA_CHUNK, GLA_SUB, BF16_ROWS
    n_sub = c_len // sub
    nt, tn = MATMUL_NT, MATMUL_TN

    @pl.when(pl.program_id(2) == 0)
    def _():
        st_s[...] = jnp.zeros_like(st_s)

    tri2 = tri_ref[...]

    def sums(ci):
        rows = slice(ci * c_len, (ci + 1) * c_len)
        g2 = g2_ref[rows, :]
        hi = g2.astype(BF16)
        lo = (g2 - hi.astype(F32)).astype(BF16)
        b2 = jnp.dot(tri2, jnp.concatenate([hi, lo], axis=0), preferred_element_type=F32)
        b2_s[rows, :] = b2
        gam_s[rows, :] = lk2_ref[rows, :] - b2

    def padded_rows(x, lo, hi, total):
        lo_t, hi_t = lo // pack * pack, -(-hi // pack) * pack
        parts = ([jnp.zeros((lo - lo_t, HEAD_DIM), F32)] if lo > lo_t else []) + [x]
        if hi_t > hi:
            parts.append(jnp.zeros((hi_t - hi, HEAD_DIM), F32))
        tiles = [jnp.concatenate(parts, axis=0).astype(BF16)]
        if lo_t:
            tiles.insert(0, jnp.zeros((lo_t, HEAD_DIM), BF16))
        if total > hi_t:
            tiles.append(jnp.zeros((total - hi_t, HEAD_DIM), BF16))
        return jnp.concatenate(tiles, axis=0)

    def scores(ci):
        off = ci * c_len
        rows = slice(off, off + c_len)
        q = q_ref[rows, :].astype(F32)
        b2 = b2_s[rows, :]
        gam = gam_s[rows, :]
        b2_end = b2[0:1] if rev else b2[c_len - 1:c_len]
        qe_s[rows, :] = (q * jnp.exp2(b2)).astype(BF16)
        upd_s[ci] = lax.dot_general(v_ref[rows, :], jnp.exp2(b2_end + gam).astype(BF16), tn,
                                    preferred_element_type=F32)
        dec_s[ci] = jnp.exp2(b2_end)
        atts = []
        for i in range(n_sub):
            r0 = sub * i
            q_i, b_i = q[r0:r0 + sub], b2[r0:r0 + sub]
            if rev:
                others = (r0 + sub, c_len)
                edge = b2[r0 + sub:r0 + sub + 1] if i < n_sub - 1 else None
            else:
                others = (0, r0)
                edge = b2[r0 - 1:r0] if i > 0 else None
            if edge is None:
                atts.append(jnp.zeros((sub, HEAD_DIM), F32))
            else:
                qt = padded_rows(q_i * jnp.exp2(b_i - edge), 0, sub, pack)
                kt = padded_rows(jnp.exp2(edge + gam[others[0]:others[1]]), others[0], others[1],
                                 HEAD_DIM)
                atts.append(lax.dot_general(qt, kt, nt, preferred_element_type=F32)[:sub])
        att_s[rows, :] = jnp.concatenate(atts, axis=0)
        for r0 in range(0, c_len, pack):
            q_t, b_t = q[r0:r0 + pack], b2[r0:r0 + pack]
            for sl in range(sub):
                row0 = off + r0 + sl
                g_rows = jnp.concatenate(
                    [jnp.broadcast_to(gam_s[row0 + u * sub:row0 + u * sub + 1, :], (sub, HEAD_DIM))
                     for u in range(pack // sub)], axis=0)
                tile = q_t * jnp.exp2(jnp.minimum(b_t + g_rows, 0.0))
                p_s[off + r0:off + r0 + pack, sl * HEAD_DIM:(sl + 1) * HEAD_DIM] = tile.astype(BF16)

    row = lax.broadcasted_iota(jnp.int32, (c_len, HEAD_DIM), 0)
    col = lax.broadcasted_iota(jnp.int32, (c_len, HEAD_DIM), 1)
    keep = (col >= row) if rev else (col <= row)
    v_pad = jnp.zeros((HEAD_DIM - c_len, HEAD_DIM), BF16)

    def outputs(ci):
        rows = slice(ci * c_len, (ci + 1) * c_len)
        dg = dg_s[rows, :]
        diag = [dg[0:sub]] + [pltpu.roll(dg[sub * i:sub * (i + 1)], sub * i, 1)
                              for i in range(1, n_sub)]
        att = att_s[rows, :] + jnp.concatenate(diag, axis=0)
        att = jnp.where(keep, att, 0.0).astype(BF16)
        o = (jnp.dot(att, jnp.concatenate([v_ref[rows, :], v_pad], axis=0),
                     preferred_element_type=F32)
             + lax.dot_general(qe_s[rows, :], seen_s[ci], nt, preferred_element_type=F32))
        if rev:
            tot = of_ref[rows, :] + o
            ms = jnp.mean(tot * tot, axis=-1, keepdims=True)
            y = tot * lax.rsqrt(ms + NORM_EPS) * ng_ref[...] * gt_ref[rows, :].astype(F32)
            o_ref[rows, :] = y.astype(o_ref.dtype)
        else:
            o_ref[rows, :] = o

    for ci in range(n_chunks):
        sums(ci)
    for ci in range(n_chunks):
        scores(ci)
    dg_s[...] = jnp.dot(p_s[...], e_ref[...], preferred_element_type=F32)
    st = st_s[...]
    for ci in (reversed(range(n_chunks)) if rev else range(n_chunks)):
        seen_s[ci] = st.astype(BF16)
        st = st * dec_s[ci] + upd_s[ci]
    st_s[...] = st
    for ci in range(n_chunks):
        outputs(ci)


def _gla_tables(rev):
    idx = jnp.arange(GLA_CHUNK)
    tri = (idx[:, None] <= idx[None, :]) if rev else (idx[:, None] >= idx[None, :])
    tri2 = jnp.concatenate([tri, tri], axis=1).astype(BF16)
    s_of_row = jnp.arange(GLA_SUB * HEAD_DIM) // HEAD_DIM
    place = jnp.arange(HEAD_DIM)[None, :] == s_of_row[:, None]
    return tri2, place.astype(BF16)


def gla_direction(q, v, gate, dec, norm_g, o_fwd, batch, seq, rev):
    m, d = q.shape
    heads = d // HEAD_DIM
    t_blk = min(GLA_BLOCK, seq)
    n_t = seq // t_blk
    n_chunks = t_blk // GLA_CHUNK
    tri2, place = _gla_tables(rev)

    def rowblk(b, h, t):
        return b * n_t + ((n_t - 1 - t) if rev else t)

    def col(group):
        return pl.BlockSpec((t_blk, HEAD_DIM), lambda b, h, t: (rowblk(b, h, t), group * heads + h))

    in_specs = [col(0), col(0), col(2 if rev else 0), col(3 if rev else 1),
                pl.BlockSpec(tri2.shape, lambda b, h, t: (0, 0)),
                pl.BlockSpec(place.shape, lambda b, h, t: (0, 0))]
    args = [q, v, dec, dec, tri2, place]
    if rev:
        in_specs += [col(0), col(0), pl.BlockSpec((1, HEAD_DIM), lambda b, h, t: (0, 0))]
        args += [o_fwd, gate, norm_g.reshape(1, HEAD_DIM)]
    return pl.pallas_call(
        functools.partial(_gla_kernel, rev=rev, n_chunks=n_chunks),
        grid=(batch, heads, n_t),
        in_specs=in_specs,
        out_specs=col(0),
        out_shape=jax.ShapeDtypeStruct((m, d), BF16 if rev else F32),
        scratch_shapes=[pltpu.VMEM((HEAD_DIM, HEAD_DIM), F32),
                        pltpu.VMEM((t_blk, HEAD_DIM), F32),
                        pltpu.VMEM((t_blk, HEAD_DIM), F32),
                        pltpu.VMEM((t_blk, HEAD_DIM), BF16),
                        pltpu.VMEM((n_chunks, 1, HEAD_DIM), F32),
                        pltpu.VMEM((n_chunks, HEAD_DIM, HEAD_DIM), F32),
                        pltpu.VMEM((n_chunks, HEAD_DIM, HEAD_DIM), BF16),
                        pltpu.VMEM((t_blk, HEAD_DIM), F32),
                        pltpu.VMEM((t_blk, GLA_SUB * HEAD_DIM), BF16),
                        pltpu.VMEM((t_blk, HEAD_DIM), F32)],
        compiler_params=_params("parallel", "parallel", "arbitrary"),
        name="gla_rev" if rev else "gla_fwd",
    )(*args)


def _hyena_proj_kernel(x_ref, xp_ref, xn_ref, g_ref, sc_ref, sh_ref, w_ref, b_ref, cw_ref, cb_ref,
                       x0_ref, z_ref, hn_ref, u_ref, *, per_seq, cs):
    i = pl.program_id(0)
    tm, d = x_ref.shape
    halo = xp_ref.shape[0]
    keep_prev = jnp.where(i % per_seq == 0, 0.0, 1.0)
    keep_next = jnp.where(i % per_seq == per_seq - 1, 0.0, 1.0)

    def norm(ref):
        return _rms_mod(ref[...], g_ref[...], sc_ref[...], sh_ref[...]).astype(BF16)

    hn_ref[0:halo, :] = norm(xp_ref)
    hn_ref[halo:halo + tm, :] = norm(x_ref)
    hn_ref[halo + tm:, :] = norm(xn_ref)
    row = lax.broadcasted_iota(jnp.int32, (tm + 2 * halo, cs), 0)
    keep = jnp.where(row < halo, keep_prev, jnp.where(row >= halo + tm, keep_next, 1.0))

    for c0 in range(0, d, cs):
        def conv(group):
            cols = slice(group * d + c0, group * d + c0 + cs)
            u = jnp.dot(hn_ref[...], w_ref[:, cols], preferred_element_type=F32) + b_ref[:, cols]
            u_ref[group] = u * keep
            return (u_ref[group, halo - 1:halo - 1 + tm, :] * cw_ref[0:1, cols]
                    + u_ref[group, halo:halo + tm, :] * cw_ref[1:2, cols]
                    + u_ref[group, halo + 1:halo + 1 + tm, :] * cw_ref[2:3, cols]
                    + cb_ref[:, cols])

        x0_ref[:, c0:c0 + cs] = conv(0).astype(x0_ref.dtype)
        z_ref[:, c0:c0 + cs] = (conv(1) * conv(2)).astype(z_ref.dtype)


def hyena_projection(x, g, scale, shift, w, bias, conv_w, conv_b, seq):
    m, d = x.shape
    tm = min(512, seq)
    per_seq = seq // tm
    halo = BF16_ROWS
    per_tile = tm // halo
    n_halo = m // halo
    cs = _tile(d, 512, 128)
    vec = pl.BlockSpec((None, 1, d), lambda i: (i // per_seq, 0, 0))
    return pl.pallas_call(
        functools.partial(_hyena_proj_kernel, per_seq=per_seq, cs=cs),
        grid=(m // tm,),
        in_specs=[
            pl.BlockSpec((tm, d), lambda i: (i, 0)),
            pl.BlockSpec((halo, d), lambda i: (jnp.maximum(i * per_tile - 1, 0), 0)),
            pl.BlockSpec((halo, d), lambda i: (jnp.minimum((i + 1) * per_tile, n_halo - 1), 0)),
            _resident((1, d)),
            vec, vec,
            _resident((d, 3 * d)),
            _resident((1, 3 * d)),
            _resident((3, 3 * d)),
            _resident((1, 3 * d)),
        ],
        out_specs=[pl.BlockSpec((tm, d), lambda i: (i, 0)),
                   pl.BlockSpec((tm, d), lambda i: (i, 0))],
        out_shape=[jax.ShapeDtypeStruct((m, d), BF16), jax.ShapeDtypeStruct((m, d), BF16)],
        scratch_shapes=[pltpu.VMEM((tm + 2 * halo, d), BF16),
                        pltpu.VMEM((3, tm + 2 * halo, cs), F32)],
        compiler_params=_params("parallel"),
        name="hyena_projection",
    )(x, x, x, g.reshape(1, d), scale, shift, w, bias.reshape(1, 3 * d), conv_w,
      conv_b.reshape(1, 3 * d))


def _filter_kernel(emb_ref, t_ref, keep_ref, win_ref, wmid_ref, b_ref, freq_ref, wout_ref,
                   delta_ref, o_ref):
    freq = freq_ref[...]
    h = jnp.sin(freq * (_dot3(win_ref[...], emb_ref[...], MATMUL_NN) + b_ref[:, 0:1]))
    for mth in range(wmid_ref.shape[0]):
        h = jnp.sin(freq * (_dot3(wmid_ref[mth], h, MATMUL_NN) + b_ref[:, mth + 1:mth + 2]))
    hf = _dot3(h, wout_ref[...], MATMUL_TN)
    o_ref[...] = hf * jnp.exp(-t_ref[...] * delta_ref[...]) * keep_ref[...]


def hyena_filter_circular(seq, d, w_in, w_mid, b, freq, w_out):
    emb_dim, width = w_in.shape
    bands = (emb_dim - 1) // 2
    mrow = jnp.arange(2 * seq)
    pos = jnp.where(mrow < seq, mrow, jnp.where(mrow == seq, 0, 2 * seq - mrow)).astype(F32)[:, None]
    t_c = pos / (seq - 1.0)
    band = jnp.linspace(1e-4, bands - 1.0, bands, dtype=F32)
    ang = (2.0 * math.pi / seq) * pos * band
    emb_c = jnp.concatenate([t_c, jnp.cos(ang), -jnp.sin(ang)], axis=-1)
    emb_c = jnp.pad(emb_c, ((0, 0), (0, HY_FILTER_PAD - emb_dim)))
    keep = (mrow != seq).astype(F32)[:, None]
    w_in_p = jnp.pad(w_in.astype(F32), ((0, HY_FILTER_PAD - emb_dim), (0, 0)))
    deltas = jnp.abs(jnp.linspace(HY_MIN_DECAY, HY_MAX_DECAY, d, dtype=F32)).reshape(1, d)
    tr = min(512, seq)
    half = seq // tr
    n_mid = w_mid.shape[0]
    return pl.pallas_call(
        _filter_kernel,
        grid=(2 * seq // tr,),
        in_specs=[
            pl.BlockSpec((HY_FILTER_PAD, tr), lambda i: (0, i)),
            pl.BlockSpec((tr, 1), lambda i: (i, 0)),
            pl.BlockSpec((tr, 1), lambda i: (i, 0)),
            pl.BlockSpec((width, HY_FILTER_PAD), lambda i: (0, 0)),
            pl.BlockSpec((n_mid, width, width), lambda i: (0, 0, 0)),
            pl.BlockSpec((width, n_mid + 1), lambda i: (0, 0)),
            pl.BlockSpec((width, 1), lambda i: (0, 0)),
            pl.BlockSpec((width, d), lambda i: (0, i // half)),
            pl.BlockSpec((1, d), lambda i: (0, 0)),
        ],
        out_specs=pl.BlockSpec((tr, d), lambda i: (i, 0)),
        out_shape=jax.ShapeDtypeStruct((2 * seq, d), F32),
        compiler_params=_params("parallel"),
        name="hyena_filter",
    )(emb_c.T, t_c, keep, w_in_p.T, jnp.swapaxes(w_mid.astype(F32), 1, 2), b.astype(F32).T,
      freq.astype(F32).reshape(width, 1), w_out.astype(F32), deltas)


def _swap_major_sublane(x):
    return jnp.swapaxes(x, 0, 1)


def _dft_stage1_kernel(m_ref, x_ref, o_ref):
    x32 = x_ref[...].astype(F32)
    outs = []
    for g in range(x32.shape[1] // SUBLANES):
        xt = _swap_major_sublane(x32[:, g * SUBLANES:(g + 1) * SUBLANES, :]).astype(BF16)
        r = jnp.stack([jnp.dot(m_ref[...], xt[j], preferred_element_type=F32)
                       for j in range(SUBLANES)], axis=0)
        outs.append(_swap_major_sublane(r))
    o_ref[...] = jnp.concatenate(outs, axis=1).astype(o_ref.dtype)


def dft_stage1(mat, x):
    r, kdim = mat.shape
    _, n2, d = x.shape
    cw = _tile(d, 512, 128)
    return pl.pallas_call(
        _dft_stage1_kernel,
        grid=(n2 // BF16_ROWS, d // cw),
        in_specs=[pl.BlockSpec((r, kdim), lambda j, c: (0, 0)),
                  pl.BlockSpec((kdim, BF16_ROWS, cw), lambda j, c: (0, j, c))],
        out_specs=pl.BlockSpec((r, BF16_ROWS, cw), lambda j, c: (0, j, c)),
        out_shape=jax.ShapeDtypeStruct((r, n2, d), BF16),
        compiler_params=_params("parallel", "parallel"),
        name="dft_stage1",
    )(mat, x)


def _stacked(a_ref, u):
    return jnp.concatenate([a_ref[0, u], a_ref[1, u]], axis=0)


def _spectrum_kernel(g_ref, a_ref, o_ref):
    n2 = a_ref.shape[2]
    for u in range(a_ref.shape[1]):
        h = jnp.dot(g_ref[u], _stacked(a_ref, u), preferred_element_type=F32).astype(o_ref.dtype)
        o_ref[u, 0] = h[:n2]
        o_ref[u, 1] = h[n2:]


def filter_spectrum(g_tab, a):
    _, n1, n2, d = a.shape
    kb = _tile(n1, DFT_K1_BLOCK, 1)
    return pl.pallas_call(
        _spectrum_kernel,
        grid=(n1 // kb,),
        in_specs=[pl.BlockSpec((kb, 2 * n2, 2 * n2), lambda k: (k, 0, 0)),
                  pl.BlockSpec((2, kb, n2, d), lambda k: (0, k, 0, 0))],
        out_specs=pl.BlockSpec((kb, 2, n2, d), lambda k: (k, 0, 0, 0)),
        out_shape=jax.ShapeDtypeStruct((n1, 2, n2, d), BF16),
        compiler_params=_params("parallel"),
        name="filter_spectrum",
    )(g_tab, a)


def _freq_mul_kernel(g_ref, a_ref, h_ref, o_ref):
    n2 = a_ref.shape[2]
    tn = MATMUL_TN
    for u in range(a_ref.shape[1]):
        x = jnp.dot(g_ref[u], _stacked(a_ref, u), preferred_element_type=F32)
        xr, xi = x[:n2], x[n2:]
        hr, hi = h_ref[u, 0].astype(F32), h_ref[u, 1].astype(F32)
        p = jnp.concatenate([xr * hr - xi * hi, xr * hi + xi * hr], axis=0).astype(BF16)
        t = lax.dot_general(g_ref[u], p, tn, preferred_element_type=F32).astype(o_ref.dtype)
        o_ref[0, u] = t[:n2]
        o_ref[1, u] = t[n2:]


def freq_multiply(g_tab, a, h):
    _, n1, n2, d = a.shape
    kb = _tile(n1, DFT_K1_BLOCK, 1)
    return pl.pallas_call(
        _freq_mul_kernel,
        grid=(n1 // kb,),
        in_specs=[pl.BlockSpec((kb, 2 * n2, 2 * n2), lambda k: (k, 0, 0)),
                  pl.BlockSpec((2, kb, n2, d), lambda k: (0, k, 0, 0)),
                  pl.BlockSpec((kb, 2, n2, d), lambda k: (k, 0, 0, 0))],
        out_specs=pl.BlockSpec((2, kb, n2, d), lambda k: (0, k, 0, 0)),
        out_shape=jax.ShapeDtypeStruct((2, n1, n2, d), BF16),
        compiler_params=_params("parallel"),
        name="freq_multiply",
    )(g_tab, a, h)


def _conv_out_kernel(m_ref, t_ref, x0_ref, z_ref, skip_ref, o_ref):
    t32 = t_ref[...].astype(F32)
    ys = []
    for g in range(t32.shape[1] // SUBLANES):
        tt = _swap_major_sublane(t32[:, g * SUBLANES:(g + 1) * SUBLANES, :]).astype(BF16)
        y = jnp.stack([jnp.dot(m_ref[...], tt[j], preferred_element_type=F32)
                       for j in range(SUBLANES)], axis=0)
        ys.append(_swap_major_sublane(y))
    y = jnp.concatenate(ys, axis=1)
    o_ref[...] = (x0_ref[...].astype(F32)
                  * (y + skip_ref[...] * z_ref[...].astype(F32))).astype(o_ref.dtype)


def conv_output(mat, t, x0, z, skip):
    r, kdim = mat.shape
    _, n2, d = t.shape
    cw = _tile(d, 256, 128)
    blk = lambda rows: pl.BlockSpec((rows, BF16_ROWS, cw), lambda j, c: (0, j, c))
    return pl.pallas_call(
        _conv_out_kernel,
        grid=(n2 // BF16_ROWS, d // cw),
        in_specs=[pl.BlockSpec((r, kdim), lambda j, c: (0, 0)),
                  blk(kdim), blk(r), blk(r),
                  pl.BlockSpec((1, cw), lambda j, c: (0, c))],
        out_specs=blk(r),
        out_shape=jax.ShapeDtypeStruct((r, n2, d), BF16),
        compiler_params=_params("parallel", "parallel"),
        name="dft_inverse_out",
    )(mat, t, x0, z, skip)


def _dft_tables(seq):
    n = 2 * seq
    n2 = FFT_N2
    n1 = n // n2
    lh = seq // n2

    def cs(idx, mod):
        ang = (2.0 * math.pi / mod) * (idx % mod).astype(F32)
        return jnp.cos(ang), jnp.sin(ang)

    k1 = jnp.arange(n1)
    cr, sr = cs(k1[:, None] * jnp.arange(lh)[None, :], n1)
    f_sig = jnp.block([[cr, sr], [-sr, cr]])
    cf, sf = cs(k1[:, None] * k1[None, :], n1)
    f_filt = jnp.concatenate([cf, -sf], axis=0)
    ca, sa = cs(k1[:, None] * jnp.arange(n2)[None, :], n)
    cb, sb = cs(jnp.arange(n2)[:, None] * jnp.arange(n2)[None, :], n2)
    gr = ca[:, None, :] * cb[None] - sa[:, None, :] * sb[None]
    gs = sa[:, None, :] * cb[None] + ca[:, None, :] * sb[None]
    g_tab = jnp.concatenate([jnp.concatenate([gr, gs], axis=2),
                             jnp.concatenate([-gs, gr], axis=2)], axis=1)
    er, es = cs(jnp.arange(lh)[:, None] * k1[None, :], n1)
    f_inv = jnp.block([[er, -es], [es, er]]) / n
    return f_sig.astype(BF16), f_filt.astype(BF16), g_tab.astype(BF16), f_inv.astype(BF16)


def hyena_long_conv(x0, z, filt, skip, tables, batch, seq):
    assert batch == 2, "the two sequences ride as real / imaginary parts of one complex signal"
    f_sig, f_filt, g_tab, f_inv = tables
    m, d = z.shape
    n2 = FFT_N2
    n1 = 2 * seq // n2
    lh = seq // n2
    a_f = dft_stage1(f_filt, filt.reshape(n1, n2, d))
    h = filter_spectrum(g_tab, a_f.reshape(2, n1, n2, d))
    a = dft_stage1(f_sig, z.reshape(batch * lh, n2, d))
    t = freq_multiply(g_tab, a.reshape(2, n1, n2, d), h)
    y = conv_output(f_inv, t.reshape(2 * n1, n2, d), x0.reshape(batch * lh, n2, d),
                    z.reshape(batch * lh, n2, d), skip.astype(F32).reshape(1, d))
    return y.reshape(m, d)


def kernel(x, c, ada_w, ada_b, norm_g, hg_w_in, hg_lower_bounds, hg_norm_g, hg_w_out,
           hy_w_in, hy_b_in, hy_conv_w, hy_conv_b, hy_filt_w_in, hy_filt_w_mid, hy_filt_b,
           hy_filt_freq, hy_filt_w_out, hy_skip, hy_w_out, mlp_w1, mlp_w2, final_g):
    batch, seq, d = x.shape
    depth = ada_w.shape[0]
    n_mixers = 2
    assert d % HEAD_DIM == 0 and seq % max(GLA_CHUNK, FFT_N2) == 0

    mod = adaln_all(c, ada_w, ada_b)
    mod = mod.reshape(2 * depth, batch, 3, 1, d)

    lbs = jax.nn.softmax(hg_lower_bounds.astype(F32), axis=1)
    lbs = jnp.cumsum(lbs, axis=1) - lbs[:, :1]

    tables = _dft_tables(seq)

    xf = x.reshape(batch * seq, d)
    for i in range(depth):
        j = i // n_mixers
        shift, scale, gate = mod[2 * i, :, 0], mod[2 * i, :, 1], mod[2 * i, :, 2]
        if i % n_mixers == 0:
            q, v, gt, dec = hgrn_projection(xf, norm_g[i, 0], scale, shift, hg_w_in[j].astype(BF16),
                                            lbs[:, j], seq)
            o_f = gla_direction(q, v, gt, dec, hg_norm_g[j], None, batch, seq, rev=False)
            mix = gla_direction(q, v, gt, dec, hg_norm_g[j], o_f, batch, seq, rev=True)
            w_out = hg_w_out
        else:
            x0, z = hyena_projection(xf, norm_g[i, 0], scale, shift, hy_w_in[j].astype(BF16),
                                     hy_b_in[j], hy_conv_w[j], hy_conv_b[j], seq)
            filt = hyena_filter_circular(seq, d, hy_filt_w_in[j], hy_filt_w_mid[j], hy_filt_b[j],
                                         hy_filt_freq[j], hy_filt_w_out[j])
            mix = hyena_long_conv(x0, z, filt, hy_skip[j], tables, batch, seq)
            w_out = hy_w_out
        gate_mix = gate
        shift, scale, gate = mod[2 * i + 1, :, 0], mod[2 * i + 1, :, 1], mod[2 * i + 1, :, 2]
        xf = mixer_out_mlp(mix, w_out, j, xf, gate_mix, norm_g[i, 1], scale, shift, gate,
                           mlp_w1, mlp_w2, i, final_g, seq, final_norm=(i == depth - 1))
    return xf.reshape(batch, seq, d)
```
